```python
import jax, jax.numpy as jnp
from jax import lax
import numpy as np

D_MODEL = 1024
BATCH = 2
SEQ = 8192
DEPTH = 2
DEC_BATCH = 128
DEC_SEQ = 4
PAST_LEN = 2048
PAGE_SIZE = 128

N_A_LAYERS = DEPTH // 2
N_B_LAYERS = DEPTH - N_A_LAYERS
D_RNN = D_MODEL
N_LRU_BLOCKS = 8
LRU_BLOCK = D_RNN // N_LRU_BLOCKS
LRU_C = 8.0
CONV_A_WIDTH = 4
N_HEADS = 16
HEAD_DIM = D_MODEL // N_HEADS
D_FF = 3 * D_MODEL
CONV_F_WIDTH = 3
Q_BLOCK = 128
FORGET_BIAS_OFFSET = 3.0
RMS_EPS = 1e-6

kernel_name = "yoco_hawk_fox_convffn_step"


def rmsnorm(x, g):
    xf = x.astype(jnp.float32)
    y = xf * lax.rsqrt(jnp.mean(xf * xf, axis=-1, keepdims=True) + RMS_EPS)
    return (y * g.astype(jnp.float32)).astype(x.dtype)


def causal_dwconv(x, prev, w, b):
    width = w.shape[0]
    t = x.shape[1]
    xp = jnp.concatenate([prev.astype(x.dtype), x], axis=1)
    y = xp[:, 0:t] * w[0]
    for j in range(1, width):
        y = y + xp[:, j:j + t] * w[j]
    return y + b, xp[:, xp.shape[1] - (width - 1):]


def _lin_combine(c1, c2):
    a1, b1 = c1
    a2, b2 = c2
    return a1 * a2, a2 * b1 + b2


def linear_recurrence(a, b, h0):
    b = b.at[:, 0].add(a[:, 0] * h0)
    _, h = lax.associative_scan(_lin_combine, (a, b), axis=1)
    return h, h[:, -1]


def rglru_block(x, p, i, conv_prev, h_prev):
    bsz, t, _ = x.shape
    h = rmsnorm(x, p["norm_a_g"][i])
    proj = jnp.einsum("btd,de->bte", h, p["w_a_in"][i])
    gate_branch, rec = proj[..., :D_RNN], proj[..., D_RNN:]
    xc, conv_new = causal_dwconv(rec, conv_prev, p["conv_a_w"][i], p["conv_a_b"][i])
    gates = jnp.einsum("btnc,ncg->btng", xc.reshape(bsz, t, N_LRU_BLOCKS, LRU_BLOCK),
                       p["w_a_gate"][i]) + p["b_a_gate"][i]
    gates = jax.nn.sigmoid(gates.astype(jnp.float32))
    r = gates[..., :LRU_BLOCK].reshape(bsz, t, D_RNN)
    ig = gates[..., LRU_BLOCK:].reshape(bsz, t, D_RNN)
    log_a = LRU_C * r * jax.nn.log_sigmoid(p["lru_lambda"][i].astype(jnp.float32))
    a = jnp.exp(log_a)
    b = jnp.sqrt(-jnp.expm1(2.0 * log_a)) * ig * xc.astype(jnp.float32)
    hs, h_last = linear_recurrence(a, b, h_prev.astype(jnp.float32))
    y = hs.astype(x.dtype) * jax.nn.gelu(gate_branch, approximate=True)
    return x + jnp.einsum("btr,rd->btd", y, p["w_a_out"][i]), conv_new, h_last


def conv_ffn(x, p, l, prev):
    h = rmsnorm(x, p["norm_f_g"][l])
    proj = jnp.einsum("btd,de->bte", h, p["w_f_in"][l])
    gate, up = proj[..., :D_FF], proj[..., D_FF:]
    gate, conv_new = causal_dwconv(gate, prev, p["conv_f_w"][l], p["conv_f_b"][l])
    y = jax.nn.gelu(gate, approximate=True) * up
    return x + jnp.einsum("btf,fd->btd", y, p["w_f_out"][l]), conv_new


def shared_kv(x, p):
    bsz, t, _ = x.shape
    hd = N_HEADS * HEAD_DIM
    h = rmsnorm(x, p["norm_kv_g"])
    proj = jnp.einsum("btd,de->bte", h, p["w_kv"])
    k = proj[..., :hd].reshape(bsz, t, N_HEADS, HEAD_DIM)
    v = proj[..., hd:2 * hd].reshape(bsz, t, N_HEADS, HEAD_DIM)
    logf = jax.nn.log_sigmoid(proj[..., 2 * hd:].astype(jnp.float32) + p["b_forget"].astype(jnp.float32))
    return k, v, logf


def fox_attention(x, p, j, k, v, fk, fq, kpos, qpos):
    bsz, t, _ = x.shape
    h = rmsnorm(x, p["norm_b_g"][j])
    q = jnp.einsum("btd,de->bte", h, p["w_q"][j]).reshape(bsz, t, N_HEADS, HEAD_DIM)
    qb = min(Q_BLOCK, t)
    nqb = t // qb
    q_blocks = q.reshape(bsz, nqb, qb, N_HEADS, HEAD_DIM).transpose(1, 0, 2, 3, 4)
    fq_blocks = fq.reshape(bsz, nqb, qb, N_HEADS).transpose(1, 0, 3, 2)
    pos_blocks = qpos.reshape(nqb, qb)
    fk_t = jnp.swapaxes(fk, 1, 2)
    scale = HEAD_DIM ** -0.5

    def one_block(args):
        q_blk, fq_blk, pos_blk = args
        s = jnp.einsum("bqhd,bkhd->bhqk", q_blk, k, preferred_element_type=jnp.float32) * scale
        s = s + fq_blk[..., :, None] - fk_t[:, :, None, :]
        mask = kpos[None, :] <= pos_blk[:, None]
        s = jnp.where(mask, s, -jnp.inf)
        pr = jax.nn.softmax(s, axis=-1)
        return jnp.einsum("bhqk,bkhd->bqhd", pr.astype(v.dtype), v)

    o = lax.map(one_block, (q_blocks, fq_blocks, pos_blocks))
    o = o.transpose(1, 0, 2, 3, 4).reshape(bsz, t, N_HEADS * HEAD_DIM)
    return x + jnp.einsum("bte,ed->btd", o, p["w_o"][j])


def trunk(x, lru_h0, lru_conv0, ffn_conv0, past, p):
    lru_h_new, lru_conv_new, ffn_conv_new = [], [], []
    layer = 0
    for i in range(N_A_LAYERS):
        x, c_new, h_new = rglru_block(x, p, i, lru_conv0[i], lru_h0[i])
        x, f_new = conv_ffn(x, p, layer, ffn_conv0[layer])
        lru_h_new.append(h_new)
        lru_conv_new.append(c_new)
        ffn_conv_new.append(f_new)
        layer += 1
    k_new, v_new, logf_new = shared_kv(x, p)
    if past is None:
        k_all, v_all, logf_all = k_new, v_new, logf_new
    else:
        k_all = jnp.concatenate([past[0], k_new.astype(past[0].dtype)], axis=1)
        v_all = jnp.concatenate([past[1], v_new.astype(past[1].dtype)], axis=1)
        logf_all = jnp.concatenate([past[2].astype(jnp.float32), logf_new], axis=1)
    f_cum = jnp.cumsum(logf_all, axis=1)
    t = x.shape[1]
    tk = k_all.shape[1]
    kpos = jnp.arange(tk, dtype=jnp.int32)
    qpos = kpos[tk - t:]
    fq = f_cum[:, tk - t:]
    for j in range(N_B_LAYERS):
        x = fox_attention(x, p, j, k_all, v_all, f_cum, fq, kpos, qpos)
        x, f_new = conv_ffn(x, p, layer, ffn_conv0[layer])
        ffn_conv_new.append(f_new)
        layer += 1
    y = rmsnorm(x, p["norm_out_g"])
    return (y, k_new, v_new, logf_new, jnp.stack(lru_h_new, axis=0),
            jnp.stack(lru_conv_new, axis=0), jnp.stack(ffn_conv_new, axis=0))


def setup_inputs(seed: int = 0) -> dict:
    key = jax.random.key(seed)
    ks = jax.random.split(key, 32)
    f32 = jnp.float32

    def nrm(k, shape, scale=1.0):
        return jax.random.normal(k, shape, f32) * scale

    n_pages = PAST_LEN // PAGE_SIZE
    n_used = DEC_BATCH * n_pages
    n_phys = n_used + max(1, n_used // 4)
    hd = N_HEADS * HEAD_DIM
    page_table = jax.random.permutation(ks[0], n_phys)[:n_used].reshape(DEC_BATCH, n_pages).astype(jnp.int32)
    u = jax.random.uniform(ks[1], (N_A_LAYERS, D_RNN), f32, minval=0.9, maxval=0.999)
    s = u ** (1.0 / LRU_C)
    lru_lambda = jnp.log(s) - jnp.log1p(-s)
    return {
        "x_prompt": nrm(ks[2], (BATCH, SEQ, D_MODEL)),
        "x_sample": nrm(ks[3], (DEC_BATCH, DEC_SEQ, D_MODEL)),
        "cache_k": nrm(ks[4], (n_phys, PAGE_SIZE, N_HEADS, HEAD_DIM)),
        "cache_v": nrm(ks[5], (n_phys, PAGE_SIZE, N_HEADS, HEAD_DIM)),
        "cache_logf": jax.nn.log_sigmoid(FORGET_BIAS_OFFSET + nrm(ks[6], (n_phys, PAGE_SIZE, N_HEADS))),
        "state_lru_h": nrm(ks[7], (N_A_LAYERS, DEC_BATCH, D_RNN), 0.5),
        "state_lru_conv": nrm(ks[8], (N_A_LAYERS, DEC_BATCH, CONV_A_WIDTH - 1, D_RNN)),
        "state_ffn_conv": nrm(ks[9], (DEPTH, DEC_BATCH, CONV_F_WIDTH - 1, D_FF)),
        "page_table": page_table,
        "norm_a_g": 1.0 + nrm(ks[10], (N_A_LAYERS, D_MODEL), 0.05),
        "w_a_in": nrm(ks[11], (N_A_LAYERS, D_MODEL, 2 * D_RNN), D_MODEL ** -0.5),
        "conv_a_w": nrm(ks[12], (N_A_LAYERS, CONV_A_WIDTH, D_RNN), CONV_A_WIDTH ** -0.5),
        "conv_a_b": nrm(ks[13], (N_A_LAYERS, D_RNN), 0.01),
        "w_a_gate": nrm(ks[14], (N_A_LAYERS, N_LRU_BLOCKS, LRU_BLOCK, 2 * LRU_BLOCK), LRU_BLOCK ** -0.5),
        "b_a_gate": nrm(ks[15], (N_A_LAYERS, N_LRU_BLOCKS, 2 * LRU_BLOCK), 0.01),
        "lru_lambda": lru_lambda,
        "w_a_out": nrm(ks[16], (N_A_LAYERS, D_RNN, D_MODEL), D_RNN ** -0.5),
        "norm_f_g": 1.0 + nrm(ks[17], (DEPTH, D_MODEL), 0.05),
        "w_f_in": nrm(ks[18], (DEPTH, D_MODEL, 2 * D_FF), D_MODEL ** -0.5),
        "conv_f_w": nrm(ks[19], (DEPTH, CONV_F_WIDTH, D_FF), CONV_F_WIDTH ** -0.5),
        "conv_f_b": nrm(ks[20], (DEPTH, D_FF), 0.01),
        "w_f_out": nrm(ks[21], (DEPTH, D_FF, D_MODEL), D_FF ** -0.5),
        "norm_kv_g": 1.0 + nrm(ks[22], (D_MODEL,), 0.05),
        "w_kv": nrm(ks[23], (D_MODEL, 2 * hd + N_HEADS), D_MODEL ** -0.5),
        "b_forget": FORGET_BIAS_OFFSET + nrm(ks[24], (N_HEADS,), 0.5),
        "norm_b_g": 1.0 + nrm(ks[25], (N_B_LAYERS, D_MODEL), 0.05),
        "w_q": nrm(ks[26], (N_B_LAYERS, D_MODEL, hd), D_MODEL ** -0.5),
        "w_o": nrm(ks[27], (N_B_LAYERS, hd, D_MODEL), hd ** -0.5),
        "norm_out_g": 1.0 + nrm(ks[28], (D_MODEL,), 0.05),
    }


def reference(x_prompt, x_sample, cache_k, cache_v, cache_logf, state_lru_h, state_lru_conv,
              state_ffn_conv, page_table, norm_a_g, w_a_in, conv_a_w, conv_a_b, w_a_gate, b_a_gate,
              lru_lambda, w_a_out, norm_f_g, w_f_in, conv_f_w, conv_f_b, w_f_out, norm_kv_g, w_kv,
              b_forget, norm_b_g, w_q, w_o, norm_out_g):
    p = dict(norm_a_g=norm_a_g, w_a_in=w_a_in, conv_a_w=conv_a_w, conv_a_b=conv_a_b,
             w_a_gate=w_a_gate, b_a_gate=b_a_gate, lru_lambda=lru_lambda, w_a_out=w_a_out,
             norm_f_g=norm_f_g, w_f_in=w_f_in, conv_f_w=conv_f_w, conv_f_b=conv_f_b, w_f_out=w_f_out,
             norm_kv_g=norm_kv_g, w_kv=w_kv, b_forget=b_forget, norm_b_g=norm_b_g, w_q=w_q, w_o=w_o,
             norm_out_g=norm_out_g)

    bsz = x_prompt.shape[0]
    h0 = jnp.zeros((N_A_LAYERS, bsz, D_RNN), jnp.float32)
    c0 = jnp.zeros((N_A_LAYERS, bsz, CONV_A_WIDTH - 1, D_RNN), x_prompt.dtype)
    f0 = jnp.zeros((DEPTH, bsz, CONV_F_WIDTH - 1, D_FF), x_prompt.dtype)
    (y_prompt, k_prompt, v_prompt, logf_prompt, lru_h_prompt, lru_conv_prompt,
     ffn_conv_prompt) = trunk(x_prompt, h0, c0, f0, None, p)

    dec_batch = x_sample.shape[0]
    past_len = page_table.shape[1] * cache_k.shape[1]
    k_past = cache_k[page_table].reshape(dec_batch, past_len, N_HEADS, HEAD_DIM)
    v_past = cache_v[page_table].reshape(dec_batch, past_len, N_HEADS, HEAD_DIM)
    logf_past = cache_logf[page_table].reshape(dec_batch, past_len, N_HEADS)
    (y_sample, k_sample, v_sample, logf_sample, lru_h_sample, lru_conv_sample,
     ffn_conv_sample) = trunk(x_sample, state_lru_h, state_lru_conv, state_ffn_conv,
                              (k_past, v_past, logf_past), p)

    return (y_prompt, y_sample, k_prompt, v_prompt, logf_prompt, lru_h_prompt, lru_conv_prompt,
            ffn_conv_prompt, k_sample, v_sample, logf_sample, lru_h_sample, lru_conv_sample,
            ffn_conv_sample)
```

```python
import functools

import numpy as np
import jax
import jax.numpy as jnp
from jax import lax
from jax.experimental import pallas as pl
from jax.experimental.pallas import tpu as pltpu

F32 = jnp.float32
BF16 = jnp.bfloat16

LANES = 128
SUBLANES = 8
VMEM_LIMIT_BYTES = 56 * 1024 * 1024

RMS_EPS = 1e-6
LRU_C = 8.0
CONV_A_WIDTH = 4
CONV_F_WIDTH = 3
MASK_VALUE = -1e30

AUG_LANES_PER_HEAD = 8


def _round_up(x, m):
    return (x + m - 1) // m * m


def _pick_tile(n, target):
    t = min(n, target)
    while n % t:
        t //= 2
    return t


def _rms_normalize(x):
    return x * lax.rsqrt(jnp.mean(x * x, axis=-1, keepdims=True) + RMS_EPS)


def _split3_bf16(x):
    hi = x.astype(BF16)
    r1 = x - hi.astype(F32)
    mid = r1.astype(BF16)
    lo = (r1 - mid.astype(F32)).astype(BF16)
    return hi, mid, lo


def _dot(a, b):
    return jnp.dot(a, b, preferred_element_type=F32)


def _dot_nt(a, b):
    return lax.dot_general(a, b, (((1,), (1,)), ((), ())), preferred_element_type=F32)


def _shifted_conv(buf_ref, w_ref, b_ref, *, width, hist, tile, stride):
    acc = None
    for j in range(width):
        back = (width - 1 - j) * stride
        term = buf_ref[hist - back:hist - back + tile, :] * w_ref[j:j + 1, :]
        acc = term if acc is None else acc + term
    return acc + b_ref[...]


def _rglru_kernel(x_ref, conv0_ref, h0_ref, g_ref, win_ref, cw_ref, cb_ref, wg_ref, bg_ref,
                  lam_ref, wout_ref, y_ref, convn_ref, hn_ref, rec_buf, a_buf, b_buf, h_carry,
                  *, stride, hist, tile):
    t = pl.program_id(1)
    d_rnn = rec_buf.shape[1]
    n_blocks, lru_block, _ = wg_ref.shape
    keep = (CONV_A_WIDTH - 1) * stride

    @pl.when(t == 0)
    def _():
        rec_buf[0:hist, :] = conv0_ref[0]
        h_carry[...] = h0_ref[0]

    @pl.when(t != 0)
    def _():
        rec_buf[0:hist, :] = rec_buf[tile:tile + hist, :]

    x = x_ref[...]
    hn = (_rms_normalize(x) * g_ref[...]).astype(BF16)
    proj = _dot(hn, win_ref[...])
    rec_buf[hist:hist + tile, :] = proj[:, d_rnn:]
    xc = _shifted_conv(rec_buf, cw_ref, cb_ref, width=CONV_A_WIDTH, hist=hist, tile=tile,
                       stride=stride)
    xcb = xc.astype(BF16)
    log_sig_lam = jax.nn.log_sigmoid(lam_ref[...])
    for n in range(n_blocks):
        cols = slice(n * lru_block, (n + 1) * lru_block)
        gates = jax.nn.sigmoid(_dot(xcb[:, cols], wg_ref[n]) + bg_ref[n])
        r = gates[:, :lru_block]
        ig = gates[:, lru_block:]
        log_a = LRU_C * r * log_sig_lam[:, cols]
        a = jnp.exp(log_a)
        a_buf[:, cols] = a
        one_minus_a2 = -jnp.tanh(log_a) * (a * a + 1.0)
        b_buf[:, cols] = jnp.sqrt(one_minus_a2) * ig * xc[:, cols]

    if stride == 1:
        row = lax.broadcasted_iota(jnp.int32, (SUBLANES, d_rnn), 0)

        def chunk(c, h_prev):
            rows = pl.ds(pl.multiple_of(c * SUBLANES, SUBLANES), SUBLANES)
            a8 = a_buf[rows, :]
            b8 = b_buf[rows, :]
            for s in (1, 2, 4):
                a_sh = pltpu.roll(a8, s, 0)
                b_sh = pltpu.roll(b8, s, 0)
                valid = row >= s
                b8 = jnp.where(valid, a8 * b_sh + b8, b8)
                a8 = jnp.where(valid, a8 * a_sh, a8)
            h8 = b8 + a8 * h_prev
            b_buf[rows, :] = h8
            return h8[SUBLANES - 1:SUBLANES, :]

        h_last = lax.fori_loop(0, tile // SUBLANES, chunk, h_carry[...])
    else:
        h_last = h_carry[...]
        for k in range(tile // stride):
            rows = slice(k * stride, (k + 1) * stride)
            h_last = a_buf[rows, :] * h_last + b_buf[rows, :]
            b_buf[rows, :] = h_last
    h_carry[...] = h_last
    hn_ref[0] = h_last
    convn_ref[0] = rec_buf[hist + tile - keep:hist + tile, :]

    gate_branch = proj[:, :d_rnn]
    yv = (b_buf[...] * jax.nn.gelu(gate_branch, approximate=True)).astype(BF16)
    y_ref[...] = x + _dot(yv, wout_ref[...])


def _rglru_layer(x2d, conv0, h0, g, w_in, conv_w, conv_b, w_gate, b_gate, lam, w_out,
                 *, groups, stride, tile):
    rows, d_model = x2d.shape
    d_rnn = w_out.shape[0]
    n_tiles = rows // groups // tile
    hist = conv0.shape[1]
    keep = (CONV_A_WIDTH - 1) * stride
    const2 = lambda b, t: (0, 0)
    const3 = lambda b, t: (0, 0, 0)
    kern = functools.partial(_rglru_kernel, stride=stride, hist=hist, tile=tile)
    return pl.pallas_call(
        kern,
        grid=(groups, n_tiles),
        in_specs=[
            pl.BlockSpec((tile, d_model), lambda b, t: (b * n_tiles + t, 0)),
            pl.BlockSpec((1, hist, d_rnn), lambda b, t: (b, 0, 0)),
            pl.BlockSpec((1, stride, d_rnn), lambda b, t: (b, 0, 0)),
            pl.BlockSpec((1, d_model), const2),
            pl.BlockSpec(w_in.shape, const2),
            pl.BlockSpec(conv_w.shape, const2),
            pl.BlockSpec((1, d_rnn), const2),
            pl.BlockSpec(w_gate.shape, const3),
            pl.BlockSpec(b_gate.shape, const3),
            pl.BlockSpec((1, d_rnn), const2),
            pl.BlockSpec(w_out.shape, const2),
        ],
        out_specs=[
            pl.BlockSpec((tile, d_model), lambda b, t: (b * n_tiles + t, 0)),
            pl.BlockSpec((1, keep, d_rnn), lambda b, t: (b, 0, 0)),
            pl.BlockSpec((1, stride, d_rnn), lambda b, t: (b, 0, 0)),
        ],
        out_shape=[
            jax.ShapeDtypeStruct((rows, d_model), F32),
            jax.ShapeDtypeStruct((groups, keep, d_rnn), F32),
            jax.ShapeDtypeStruct((groups, stride, d_rnn), F32),
        ],
        scratch_shapes=[
            pltpu.VMEM((hist + tile, d_rnn), F32),
            pltpu.VMEM((tile, d_rnn), F32),
            pltpu.VMEM((tile, d_rnn), F32),
            pltpu.VMEM((stride, d_rnn), F32),
        ],
        compiler_params=pltpu.CompilerParams(
            dimension_semantics=("arbitrary", "arbitrary"),
            vmem_limit_bytes=VMEM_LIMIT_BYTES),
        name="rglru_layer",
    )(x2d, conv0, h0, g, w_in, conv_w, conv_b, w_gate, b_gate, lam, w_out)


def _ffn_kernel(*refs, has_pre, has_final_norm, stride, hist, tile, n_chunks):
    refs = list(refs)
    x_ref = refs.pop(0)
    if has_pre:
        o_ref = refs.pop(0)
        wo_ref = refs.pop(0)
    gf_ref, wgate_ref, wup_ref, cw_ref, cb_ref, wout_ref, g0_ref = refs[:7]
    refs = refs[7:]
    if has_final_norm:
        gout_ref = refs.pop(0)
    y_ref, convn_ref, h_buf, gate_buf, hist_buf = refs
    t = pl.program_id(1)
    c = pl.program_id(2)
    keep = (CONV_F_WIDTH - 1) * stride

    @pl.when(c == 0)
    def _():
        x1 = x_ref[...]
        if has_pre:
            x1 = x1 + _dot(o_ref[...], wo_ref[...])
        y_ref[...] = x1
        h_buf[...] = (_rms_normalize(x1) * gf_ref[...]).astype(BF16)

    @pl.when(t == 0)
    def _():
        gate_buf[0:hist, :] = g0_ref[0]

    @pl.when(t != 0)
    def _():
        gate_buf[0:hist, :] = hist_buf[c]

    h = h_buf[...]
    gate_buf[hist:hist + tile, :] = _dot(h, wgate_ref[...])
    up = _dot(h, wup_ref[...])
    gate = _shifted_conv(gate_buf, cw_ref, cb_ref, width=CONV_F_WIDTH, hist=hist, tile=tile,
                         stride=stride)
    hist_buf[c] = gate_buf[tile:tile + hist, :]
    convn_ref[0] = gate_buf[hist + tile - keep:hist + tile, :]
    yv = (jax.nn.gelu(gate, approximate=True) * up).astype(BF16)
    y_ref[...] += _dot(yv, wout_ref[...])

    if has_final_norm:
        @pl.when(c == n_chunks - 1)
        def _():
            y_ref[...] = _rms_normalize(y_ref[...]) * gout_ref[...]


def _ffn_layer(x2d, g0, gf, w_in, conv_w, conv_b, w_out, *, groups, stride, tile,
               pre=None, final_gain=None):
    rows, d_model = x2d.shape
    d_ff = w_out.shape[0]
    ff_chunk = _pick_tile(d_ff, 1024)
    n_chunks = d_ff // ff_chunk
    n_tiles = rows // groups // tile
    hist = g0.shape[1]
    keep = (CONV_F_WIDTH - 1) * stride
    const2 = lambda b, t, c: (0, 0)
    row_map = lambda b, t, c: (b * n_tiles + t, 0)
    operands = [x2d]
    in_specs = [pl.BlockSpec((tile, d_model), row_map)]
    if pre is not None:
        o2d, w_o = pre
        operands += [o2d, w_o]
        in_specs += [pl.BlockSpec((tile, o2d.shape[1]), row_map), pl.BlockSpec(w_o.shape, const2)]
    operands += [gf, w_in, w_in, conv_w, conv_b, w_out, g0]
    in_specs += [
        pl.BlockSpec((1, d_model), const2),
        pl.BlockSpec((d_model, ff_chunk), lambda b, t, c: (0, c)),
        pl.BlockSpec((d_model, ff_chunk), lambda b, t, c: (0, n_chunks + c)),
        pl.BlockSpec((CONV_F_WIDTH, ff_chunk), lambda b, t, c: (0, c)),
        pl.BlockSpec((1, ff_chunk), lambda b, t, c: (0, c)),
        pl.BlockSpec((ff_chunk, d_model), lambda b, t, c: (c, 0)),
        pl.BlockSpec((1, hist, ff_chunk), lambda b, t, c: (b, 0, c)),
    ]
    if final_gain is not None:
        operands.append(final_gain)
        in_specs.append(pl.BlockSpec((1, d_model), const2))
    kern = functools.partial(_ffn_kernel, has_pre=pre is not None,
                             has_final_norm=final_gain is not None, stride=stride, hist=hist,
                             tile=tile, n_chunks=n_chunks)
    return pl.pallas_call(
        kern,
        grid=(groups, n_tiles, n_chunks),
        in_specs=in_specs,
        out_specs=[
            pl.BlockSpec((tile, d_model), row_map),
            pl.BlockSpec((1, keep, ff_chunk), lambda b, t, c: (b, 0, c)),
        ],
        out_shape=[
            jax.ShapeDtypeStruct((rows, d_model), F32),
            jax.ShapeDtypeStruct((groups, keep, d_ff), F32),
        ],
        scratch_shapes=[
            pltpu.VMEM((tile, d_model), BF16),
            pltpu.VMEM((hist + tile, ff_chunk), F32),
            pltpu.VMEM((n_chunks, hist, ff_chunk), F32),
        ],
        compiler_params=pltpu.CompilerParams(
            dimension_semantics=("arbitrary", "arbitrary", "arbitrary"),
            vmem_limit_bytes=VMEM_LIMIT_BYTES),
        name="conv_ffn",
    )(*operands)


def _kvq_kernel(*refs, n_heads, scale, with_aug, tile):
    (x_ref, gkv_ref, gb_ref, wk_ref, wv_ref, wf_ref, bf_ref, wq_ref) = refs[:8]
    if with_aug:
        tri_ref, pq_ref, pk_ref, oneq_ref, onek_ref = refs[8:13]
        k_ref, v_ref, lf_ref, q2_ref, k2_ref, vb_ref, f_carry = refs[13:]
    else:
        k_ref, v_ref, lf_ref, q_ref = refs[8:]
    n = _rms_normalize(x_ref[...])
    hk = (n * gkv_ref[...]).astype(BF16)
    hq = (n * gb_ref[...]).astype(BF16)
    k = _dot(hk, wk_ref[...])
    v = _dot(hk, wv_ref[...])
    z = _dot(hk, wf_ref[...]) + bf_ref[...]
    lane = lax.broadcasted_iota(jnp.int32, z.shape, 1)
    lf = jnp.where(lane < n_heads, jax.nn.log_sigmoid(z), 0.0)
    q = _dot(hq, wq_ref[...]) * scale
    k_ref[...] = k
    v_ref[...] = v
    lf_ref[...] = lf[:, :n_heads]
    if not with_aug:
        q_ref[...] = q
        return

    @pl.when(pl.program_id(1) == 0)
    def _():
        f_carry[...] = jnp.zeros_like(f_carry)

    tri = tri_ref[...]
    hi, mid, lo = _split3_bf16(lf)
    f_cum = _dot(tri, hi) + _dot(tri, mid) + _dot(tri, lo) + f_carry[...]
    f_carry[...] = f_cum[tile - 1:tile, :]

    f_cat = jnp.concatenate(_split3_bf16(f_cum), axis=1)
    aug_q = (_dot(f_cat, pq_ref[...]) + oneq_ref[...]).astype(BF16)
    aug_k = (_dot(f_cat, pk_ref[...]) + onek_ref[...]).astype(BF16)
    qb = q.astype(BF16)
    kb = k.astype(BF16)
    vb_ref[...] = v.astype(BF16)
    for hp in range(n_heads // 2):
        src = slice(hp * LANES, (hp + 1) * LANES)
        q2_ref[:, 2 * hp * LANES:(2 * hp + 1) * LANES] = qb[:, src]
        q2_ref[:, (2 * hp + 1) * LANES:(2 * hp + 2) * LANES] = aug_q[:, src]
        k2_ref[:, 2 * hp * LANES:(2 * hp + 1) * LANES] = kb[:, src]
        k2_ref[:, (2 * hp + 1) * LANES:(2 * hp + 2) * LANES] = aug_k[:, src]


def _aug_constants(n_heads, tile):
    n_pairs = n_heads // 2
    pq = np.zeros((3 * LANES, n_pairs * LANES), np.float32)
    pk = np.zeros((3 * LANES, n_pairs * LANES), np.float32)
    oneq = np.zeros((1, n_pairs * LANES), np.float32)
    onek = np.zeros((1, n_pairs * LANES), np.float32)
    for h in range(n_heads):
        base = (h // 2) * LANES + (h % 2) * AUG_LANES_PER_HEAD
        for piece in range(3):
            pq[piece * LANES + h, base + piece] = 1.0
            onek[0, base + piece] = 1.0
            pk[piece * LANES + h, base + 3 + piece] = -1.0
            oneq[0, base + 3 + piece] = 1.0
    tri = np.tril(np.ones((tile, tile), np.float32))
    return (jnp.asarray(tri, BF16), jnp.asarray(pq, BF16), jnp.asarray(pk, BF16),
            jnp.asarray(oneq), jnp.asarray(onek))


def _kvq_layer(x2d, gkv, gb, wk, wv, wf, bf, wq, *, n_heads, groups, tile, with_aug):
    rows, d_model = x2d.shape
    hd = wk.shape[1]
    n_tiles = rows // groups // tile
    scale = float(hd // n_heads) ** -0.5
    const2 = lambda b, t: (0, 0)
    row_map = lambda b, t: (b * n_tiles + t, 0)
    operands = [x2d, gkv, gb, wk, wv, wf, bf, wq]
    in_specs = [pl.BlockSpec((tile, d_model), row_map)]
    in_specs += [pl.BlockSpec(a.shape, const2) for a in operands[1:]]
    out_shape = [jax.ShapeDtypeStruct((rows, hd), F32), jax.ShapeDtypeStruct((rows, hd), F32),
                 jax.ShapeDtypeStruct((rows, n_heads), F32)]
    out_specs = [pl.BlockSpec((tile, hd), row_map), pl.BlockSpec((tile, hd), row_map),
                 pl.BlockSpec((tile, n_heads), row_map)]
    scratch = []
    if with_aug:
        consts = _aug_constants(n_heads, tile)
        operands += list(consts)
        in_specs += [pl.BlockSpec(a.shape, const2) for a in consts]
        out_shape += [jax.ShapeDtypeStruct((rows, 2 * hd), BF16),
                      jax.ShapeDtypeStruct((rows, 2 * hd), BF16),
                      jax.ShapeDtypeStruct((rows, hd), BF16)]
        out_specs += [pl.BlockSpec((tile, 2 * hd), row_map), pl.BlockSpec((tile, 2 * hd), row_map),
                      pl.BlockSpec((tile, hd), row_map)]
        scratch = [pltpu.VMEM((1, LANES), F32)]
    else:
        out_shape.append(jax.ShapeDtypeStruct((rows, hd), F32))
        out_specs.append(pl.BlockSpec((tile, hd), row_map))
    kern = functools.partial(_kvq_kernel, n_heads=n_heads, scale=scale, with_aug=with_aug,
                             tile=tile)
    return pl.pallas_call(
        kern,
        grid=(groups, n_tiles),
        in_specs=in_specs,
        out_specs=out_specs,
        out_shape=out_shape,
        scratch_shapes=scratch,
        compiler_params=pltpu.CompilerParams(
            dimension_semantics=("arbitrary", "arbitrary"),
            vmem_limit_bytes=VMEM_LIMIT_BYTES),
        name="kv_q_proj",
    )(*operands)


def _prompt_attn_kernel(q2_ref, k2_ref, v_ref, o_ref, qh_buf, m_buf, l_buf, acc_buf,
                        *, tq, tk, head_dim):
    qi = pl.program_id(2)
    ki = pl.program_id(3)
    last_k = ((qi + 1) * tq - 1) // tk

    @pl.when(ki == 0)
    def _():
        q2 = q2_ref[...]
        lane = lax.broadcasted_iota(jnp.int32, q2.shape, 1)
        aug = lane - LANES
        for j in range(2):
            own = ((lane >= j * head_dim) & (lane < (j + 1) * head_dim)) | (
                (aug >= j * AUG_LANES_PER_HEAD) & (aug < (j + 1) * AUG_LANES_PER_HEAD))
            qh_buf[j] = jnp.where(own, q2, jnp.zeros_like(q2))
        m_buf[...] = jnp.full_like(m_buf, MASK_VALUE)
        l_buf[...] = jnp.zeros_like(l_buf)
        acc_buf[...] = jnp.zeros_like(acc_buf)

    def step(masked):
        k2 = k2_ref[...]
        v = v_ref[...]
        for j in range(2):
            s = _dot_nt(qh_buf[j], k2)
            if masked:
                qpos = qi * tq + lax.broadcasted_iota(jnp.int32, s.shape, 0)
                kpos = ki * tk + lax.broadcasted_iota(jnp.int32, s.shape, 1)
                s = jnp.where(kpos <= qpos, s, MASK_VALUE)
            m_prev = m_buf[j]
            m_next = jnp.maximum(m_prev, jnp.max(s, axis=1, keepdims=True))
            alpha = jnp.exp(m_prev - m_next)
            p = jnp.exp(s - pltpu.repeat(m_next, tk // LANES, axis=1))
            l_buf[j] = alpha * l_buf[j] + jnp.sum(p, axis=1, keepdims=True)
            acc_buf[j] = alpha * acc_buf[j] + _dot(p.astype(BF16), v)
            m_buf[j] = m_next

    needs_mask = (ki + 1) * tk - 1 > qi * tq

    @pl.when((ki <= last_k) & needs_mask)
    def _():
        step(True)

    @pl.when((ki <= last_k) & jnp.logical_not(needs_mask))
    def _():
        step(False)

    @pl.when(ki == last_k)
    def _():
        lane = lax.broadcasted_iota(jnp.int32, (tq, LANES), 1)
        o0 = acc_buf[0] / l_buf[0]
        o1 = acc_buf[1] / l_buf[1]
        o_ref[...] = jnp.where(lane < head_dim, o0, o1).astype(o_ref.dtype)


def _prompt_attention(q2, k2, vb, *, batch, seq, n_heads, tq, tk):
    rows = batch * seq
    head_dim = vb.shape[1] // n_heads
    n_pairs = n_heads // 2
    nq, nk = seq // tq, seq // tk

    def kv_map(b, hp, qi, ki):
        return (b * nk + jnp.minimum(ki, ((qi + 1) * tq - 1) // tk), hp)

    kern = functools.partial(_prompt_attn_kernel, tq=tq, tk=tk, head_dim=head_dim)
    return pl.pallas_call(
        kern,
        grid=(batch, n_pairs, nq, nk),
        in_specs=[
            pl.BlockSpec((tq, 2 * LANES), lambda b, hp, qi, ki: (b * nq + qi, hp)),
            pl.BlockSpec((tk, 2 * LANES), kv_map),
            pl.BlockSpec((tk, LANES), kv_map),
        ],
        out_specs=pl.BlockSpec((tq, LANES), lambda b, hp, qi, ki: (b * nq + qi, hp)),
        out_shape=jax.ShapeDtypeStruct((rows, n_heads * head_dim), BF16),
        scratch_shapes=[
            pltpu.VMEM((2, tq, 2 * LANES), BF16),
            pltpu.VMEM((2, tq, LANES), F32),
            pltpu.VMEM((2, tq, LANES), F32),
            pltpu.VMEM((2, tq, LANES), F32),
        ],
        compiler_params=pltpu.CompilerParams(
            dimension_semantics=("arbitrary", "arbitrary", "arbitrary", "arbitrary"),
            vmem_limit_bytes=VMEM_LIMIT_BYTES),
        name="prompt_attention",
    )(q2, k2, vb)


def _sample_attn_kernel(pt_ref, *refs, n_heads, head_dim, dec_seq, pages_per_step, page_size):
    del pt_ref
    n_in = 3 * pages_per_step
    k_refs = refs[0:pages_per_step]
    v_refs = refs[pages_per_step:2 * pages_per_step]
    lf_refs = refs[2 * pages_per_step:n_in]
    q_ref, kn_ref, vn_ref, lfn_ref, sl_ref = refs[n_in:n_in + 5]
    o_ref = refs[n_in + 5]
    qrows_buf, cn_col, carry, m_buf, l_buf, acc_buf, new_buf = refs[n_in + 6:]
    g = pl.program_id(1)
    n_rows = dec_seq * n_heads
    hd = n_heads * head_dim

    def head_mask(shape):
        r = lax.broadcasted_iota(jnp.int32, shape, 0) % n_heads
        lane = lax.broadcasted_iota(jnp.int32, shape, 1)
        return (lane >= r * head_dim) & (lane < (r + 1) * head_dim)

    def column_of(rowvec):
        wide = jnp.broadcast_to(rowvec, (n_rows, rowvec.shape[1]))
        r = lax.broadcasted_iota(jnp.int32, wide.shape, 0) % n_heads
        lane = lax.broadcasted_iota(jnp.int32, wide.shape, 1)
        return jnp.sum(jnp.where(lane == r, wide, 0.0), axis=1, keepdims=True)

    def online_update(s, v_bf16):
        m_prev = m_buf[...]
        m_next = jnp.maximum(m_prev, jnp.max(s, axis=1, keepdims=True))
        alpha = jnp.exp(m_prev - m_next)
        p = jnp.exp(s - m_next[:, 0:1])
        l_buf[...] = alpha * l_buf[...] + jnp.sum(p, axis=1, keepdims=True)
        acc_buf[...] = acc_buf[...] * alpha[:, 0:1] + _dot(p.astype(BF16), v_bf16)
        m_buf[...] = m_next

    @pl.when(g == 0)
    def _():
        q = q_ref[0]
        mask = head_mask((n_rows, hd))
        qrep = jnp.concatenate(
            [jnp.broadcast_to(q[t:t + 1, :], (n_heads, hd)) for t in range(dec_seq)], axis=0)
        qrows = jnp.where(mask, qrep, 0.0).astype(BF16)
        qrows_buf[...] = qrows
        m_buf[...] = jnp.full_like(m_buf, MASK_VALUE)
        l_buf[...] = jnp.zeros_like(l_buf)
        acc_buf[...] = jnp.zeros_like(acc_buf)
        carry[...] = jnp.zeros_like(carry)

        lfn = lfn_ref[0]
        cums = []
        run = None
        for t in range(dec_seq):
            run = lfn[t:t + 1, :] if run is None else run + lfn[t:t + 1, :]
            cums.append(run)
        r_t = lax.broadcasted_iota(jnp.int32, (n_rows, 1), 0) // n_heads
        cn = jnp.zeros((n_rows, 1), F32)
        for t in range(dec_seq):
            cn = jnp.where(r_t == t, column_of(cums[t]), cn)
        cn_col[...] = cn
        lane = lax.broadcasted_iota(jnp.int32, (n_rows, page_size), 1)
        bias = jnp.full((n_rows, page_size), MASK_VALUE, F32)
        for s in range(dec_seq):
            bias = jnp.where((lane == s) & (r_t >= s), cn - column_of(cums[s]), bias)
        new_buf[...] = jnp.zeros_like(new_buf)
        new_buf[0, 0:dec_seq, :] = kn_ref[0]
        new_buf[1, 0:dec_seq, :] = vn_ref[0]
        s_new = _dot_nt(qrows, new_buf[0].astype(BF16)) + bias
        online_update(s_new, new_buf[1].astype(BF16))

    sl = sl_ref[...]
    biases = [None] * pages_per_step
    run = carry[...]
    for j in reversed(range(pages_per_step)):
        lft = lf_refs[j][0]
        hi, mid, lo = _split3_bf16(lft)
        suffix = _dot(hi, sl) + _dot(mid, sl) + _dot(lo, sl) + run
        biases[j] = jnp.concatenate([suffix] * dec_seq, axis=0)
        run = run + jnp.sum(lft, axis=1, keepdims=True)
    carry[...] = run
    bias = jnp.concatenate(biases, axis=1) + cn_col[...]
    k_all = jnp.concatenate([r[0].astype(BF16) for r in k_refs], axis=0)
    v_all = jnp.concatenate([r[0].astype(BF16) for r in v_refs], axis=0)
    s = _dot_nt(qrows_buf[...], k_all) + bias
    online_update(s, v_all)

    @pl.when(g == pl.num_programs(1) - 1)
    def _():
        out = jnp.where(head_mask((n_rows, hd)), acc_buf[...] / l_buf[:, 0:1], 0.0)
        o_ref[0] = jnp.sum(out.reshape(dec_seq, n_heads, hd), axis=1).astype(o_ref.dtype)


def _sample_attention(page_table, cache_k2, cache_v2, cache_lft, q, k_new, v_new, lf_new,
                      *, n_heads, pages_per_step):
    dec_batch, dec_seq, hd = q.shape
    n_pages = page_table.shape[1]
    page_size = cache_k2.shape[1]
    head_dim = hd // n_heads
    n_groups = n_pages // pages_per_step
    n_rows = dec_seq * n_heads
    sl = jnp.asarray(np.tril(np.ones((page_size, page_size), np.float32), -1), BF16)

    def page_map(j):
        def index_map(b, g, pt):
            return (pt[b, (n_groups - 1 - g) * pages_per_step + j], 0, 0)
        return index_map

    seq_map = lambda b, g, pt: (b, 0, 0)
    in_specs = (
        [pl.BlockSpec((1, page_size, hd), page_map(j)) for j in range(pages_per_step)]
        + [pl.BlockSpec((1, page_size, hd), page_map(j)) for j in range(pages_per_step)]
        + [pl.BlockSpec((1, n_heads, page_size), page_map(j)) for j in range(pages_per_step)]
        + [pl.BlockSpec((1, dec_seq, hd), seq_map)] * 3
        + [pl.BlockSpec((1, dec_seq, n_heads), seq_map),
           pl.BlockSpec((page_size, page_size), lambda b, g, pt: (0, 0))])
    kern = functools.partial(_sample_attn_kernel, n_heads=n_heads, head_dim=head_dim,
                             dec_seq=dec_seq, pages_per_step=pages_per_step, page_size=page_size)
    grid_spec = pltpu.PrefetchScalarGridSpec(
        num_scalar_prefetch=1,
        grid=(dec_batch, n_groups),
        in_specs=in_specs,
        out_specs=pl.BlockSpec((1, dec_seq, hd), seq_map),
        scratch_shapes=[
            pltpu.VMEM((n_rows, hd), BF16),
            pltpu.VMEM((n_rows, 1), F32),
            pltpu.VMEM((n_heads, LANES), F32),
            pltpu.VMEM((n_rows, LANES), F32),
            pltpu.VMEM((n_rows, LANES), F32),
            pltpu.VMEM((n_rows, hd), F32),
            pltpu.VMEM((2, page_size, hd), F32),
        ])
    operands = ([cache_k2] * pages_per_step + [cache_v2] * pages_per_step
                + [cache_lft] * pages_per_step + [q, k_new, v_new, lf_new, sl])
    return pl.pallas_call(
        kern,
        grid_spec=grid_spec,
        out_shape=jax.ShapeDtypeStruct((dec_batch, dec_seq, hd), BF16),
        compiler_params=pltpu.CompilerParams(
            dimension_semantics=("arbitrary", "arbitrary"),
            vmem_limit_bytes=VMEM_LIMIT_BYTES),
        name="sample_attention",
    )(page_table, *operands)


def _pad_history(state, hist):
    keep = state.shape[1]
    if keep == hist:
        return state
    return jnp.pad(state, ((0, 0), (hist - keep, 0), (0, 0)))


def _to_time_major(a):
    b, t = a.shape[:2]
    return jnp.swapaxes(a, 0, 1).reshape((1, t * b) + a.shape[2:])


def _from_time_major(a, b):
    t = a.shape[1] // b
    return jnp.swapaxes(a.reshape((t, b) + a.shape[2:]), 0, 1)


def kernel(x_prompt, x_sample, cache_k, cache_v, cache_logf, state_lru_h, state_lru_conv,
           state_ffn_conv, page_table, norm_a_g, w_a_in, conv_a_w, conv_a_b, w_a_gate, b_a_gate,
           lru_lambda, w_a_out, norm_f_g, w_f_in, conv_f_w, conv_f_b, w_f_out, norm_kv_g, w_kv,
           b_forget, norm_b_g, w_q, w_o, norm_out_g):
    batch, seq, d_model = x_prompt.shape
    dec_batch, dec_seq, _ = x_sample.shape
    n_phys, page_size, n_heads, head_dim = cache_k.shape
    hd = n_heads * head_dim
    n_a = w_a_in.shape[0]
    n_b = w_q.shape[0]
    d_rnn = w_a_out.shape[1]
    d_ff = w_f_out.shape[1]
    assert 2 * head_dim == LANES and n_heads % 2 == 0 and n_heads <= LANES
    assert dec_batch % SUBLANES == 0 and dec_seq >= CONV_A_WIDTH - 1

    row = lambda a: a.reshape(1, -1).astype(F32)
    w_a_in_b = w_a_in.astype(BF16)
    w_a_gate_b = w_a_gate.astype(BF16)
    w_a_out_b = w_a_out.astype(BF16)
    w_f_in_b = w_f_in.astype(BF16)
    w_f_out_b = w_f_out.astype(BF16)
    w_q_b = w_q.astype(BF16)
    w_o_b = w_o.astype(BF16)
    wk = w_kv[:, :hd].astype(BF16)
    wv = w_kv[:, hd:2 * hd].astype(BF16)
    wf = jnp.pad(w_kv[:, 2 * hd:], ((0, 0), (0, LANES - n_heads))).astype(BF16)
    bf = jnp.pad(b_forget.astype(F32), (0, LANES - n_heads)).reshape(1, LANES)

    def trunk(x2d, lru_h0, lru_conv0, ffn_conv0, *, groups, stride, tiles, with_aug,
              attention_fn):
        lru_h_new, lru_conv_new, ffn_conv_new = [], [], []
        layer = 0
        hist_a = _round_up((CONV_A_WIDTH - 1) * stride, SUBLANES)
        hist_f = _round_up((CONV_F_WIDTH - 1) * stride, SUBLANES)

        def ffn(x2d, layer, **kw):
            return _ffn_layer(x2d, _pad_history(ffn_conv0[layer], hist_f), row(norm_f_g[layer]),
                              w_f_in_b[layer], conv_f_w[layer], row(conv_f_b[layer]),
                              w_f_out_b[layer], groups=groups, stride=stride, tile=tiles["ffn"],
                              **kw)

        for i in range(n_a):
            x2d, c_new, h_new = _rglru_layer(
                x2d, _pad_history(lru_conv0[i], hist_a), lru_h0[i], row(norm_a_g[i]), w_a_in_b[i],
                conv_a_w[i], row(conv_a_b[i]), w_a_gate_b[i], b_a_gate[i][:, None, :],
                row(lru_lambda[i]), w_a_out_b[i], groups=groups, stride=stride,
                tile=tiles["rglru"])
            x2d, f_new = ffn(x2d, layer)
            lru_h_new.append(h_new)
            lru_conv_new.append(c_new)
            ffn_conv_new.append(f_new)
            layer += 1
        kvq = _kvq_layer(x2d, row(norm_kv_g), row(norm_b_g[0]), wk, wv, wf, bf, w_q_b[0],
                         n_heads=n_heads, groups=groups, tile=tiles["kvq"],
                         with_aug=with_aug)
        k_new, v_new, lf_new = kvq[:3]
        for j in range(n_b):
            assert j == 0, "one attention layer per shared K/V projection is supported"
            o2d = attention_fn(kvq)
            x2d, f_new = ffn(x2d, layer, pre=(o2d, w_o_b[j]),
                             final_gain=row(norm_out_g) if j == n_b - 1 else None)
            ffn_conv_new.append(f_new)
            layer += 1
        return x2d, k_new, v_new, lf_new, lru_h_new, lru_conv_new, ffn_conv_new

    def prompt_attention_fn(kvq):
        q2, k2, vb = kvq[3:]
        t_attn = _pick_tile(seq, 512)
        return _prompt_attention(q2, k2, vb, batch=batch, seq=seq, n_heads=n_heads, tq=t_attn,
                                 tk=t_attn)

    zeros = lambda *s: jnp.zeros(s, F32)
    tiles = {"rglru": _pick_tile(seq, 256), "ffn": _pick_tile(seq, 512),
             "kvq": _pick_tile(seq, 512)}
    (y, k_p, v_p, lf_p, h_p, c_p, f_p) = trunk(
        x_prompt.reshape(batch * seq, d_model),
        zeros(n_a, batch, 1, d_rnn), zeros(n_a, batch, CONV_A_WIDTH - 1, d_rnn),
        zeros(n_a + n_b, batch, CONV_F_WIDTH - 1, d_ff),
        groups=batch, stride=1, tiles=tiles, with_aug=True, attention_fn=prompt_attention_fn)
    y_prompt = y.reshape(batch, seq, d_model)
    k_prompt = k_p.reshape(batch, seq, n_heads, head_dim)
    v_prompt = v_p.reshape(batch, seq, n_heads, head_dim)
    logf_prompt = lf_p.reshape(batch, seq, n_heads)
    lru_h_prompt = jnp.stack([h.reshape(batch, d_rnn) for h in h_p], axis=0)
    lru_conv_prompt = jnp.stack(c_p, axis=0)
    ffn_conv_prompt = jnp.stack(f_p, axis=0)

    cache_k2 = cache_k.reshape(n_phys, page_size, hd)
    cache_v2 = cache_v.reshape(n_phys, page_size, hd)
    cache_lft = jnp.swapaxes(cache_logf, 1, 2)
    n_pages = page_table.shape[1]

    def sample_attention_fn(kvq):
        k_t, v_t, lf_t, q_t = kvq
        bm = lambda a: _from_time_major(a[None], dec_batch)
        o = _sample_attention(page_table, cache_k2, cache_v2, cache_lft, bm(q_t), bm(k_t),
                              bm(v_t), bm(lf_t), n_heads=n_heads,
                              pages_per_step=_pick_tile(n_pages, 8))
        return _to_time_major(o)[0]

    n_rows = dec_batch * dec_seq
    tiles = {"rglru": n_rows, "ffn": n_rows, "kvq": n_rows}
    (y, k_s, v_s, lf_s, h_s, c_s, f_s) = trunk(
        _to_time_major(x_sample)[0],
        state_lru_h[:, None], jnp.stack([_to_time_major(s) for s in state_lru_conv]),
        jnp.stack([_to_time_major(s) for s in state_ffn_conv]),
        groups=1, stride=dec_batch, tiles=tiles, with_aug=False,
        attention_fn=sample_attention_fn)
    bm = lambda a: _from_time_major(a[None], dec_batch)
    y_sample = bm(y)
    k_sample = bm(k_s).reshape(dec_batch, dec_seq, n_heads, head_dim)
    v_sample = bm(v_s).reshape(dec_batch, dec_seq, n_heads, head_dim)
    logf_sample = bm(lf_s)
    lru_h_sample = jnp.stack([h[0] for h in h_s], axis=0)
    lru_conv_sample = jnp.stack([_from_time_major(c, dec_batch) for c in c_s], axis=0)
    ffn_conv_sample = jnp.stack([_from_time_major(f, dec_batch) for f in f_s], axis=0)

    return (y_prompt, y_sample, k_prompt, v_prompt, logf_prompt, lru_h_prompt, lru_conv_prompt,
            ffn_conv_prompt, k_sample, v_sample, logf_sample, lru_h_sample, lru_conv_sample,
            ffn_conv_sample)
```

```python
import functools

import numpy as np
import jax
import jax.numpy as jnp
from jax import lax
from jax.experimental import pallas as pl
from jax.experimental.pallas import tpu as pltpu

F32 = jnp.float32
BF16 = jnp.bfloat16

LANES = 128
SUBLANES = 8
VMEM_LIMIT_BYTES = 56 * 1024 * 1024

RMS_EPS = 1e-6
LRU_C = 8.0
CONV_A_WIDTH = 4
CONV_F_WIDTH = 3
MASK_VALUE = -1e30
LOG2_E = 1.4426950408889634

AUG_LANES_PER_HEAD = 8


def _round_up(x, m):
    return (x + m - 1) // m * m


def _pick_tile(n, target):
    t = min(n, target)
    while n % t:
        t //= 2
    return t


def _rms_normalize(x):
    return x * lax.rsqrt(jnp.mean(x * x, axis=-1, keepdims=True) + RMS_EPS)


def _split3_bf16(x):
    hi = x.astype(BF16)
    r1 = x - hi.astype(F32)
    mid = r1.astype(BF16)
    lo = (r1 - mid.astype(F32)).astype(BF16)
    return hi, mid, lo


def _dot(a, b):
    return jnp.dot(a, b, preferred_element_type=F32)


def _dot_nt(a, b):
    return lax.dot_general(a, b, (((1,), (1,)), ((), ())), preferred_element_type=F32)


def _shifted_conv(buf_ref, w_ref, b_ref, *, width, hist, tile, stride):
    acc = None
    for j in range(width):
        back = (width - 1 - j) * stride
        term = buf_ref[hist - back:hist - back + tile, :] * w_ref[j:j + 1, :]
        acc = term if acc is None else acc + term
    return acc + b_ref[...]


def _rglru_kernel(x_ref, conv0_ref, h0_ref, g_ref, win_ref, cw_ref, cb_ref, wg_ref, bg_ref,
                  lam_ref, wout_ref, y_ref, convn_ref, hn_ref, rec_buf, a_buf, b_buf, h_carry,
                  *, stride, hist, tile):
    t = pl.program_id(1)
    d_rnn = rec_buf.shape[1]
    n_blocks, lru_block, _ = wg_ref.shape
    keep = (CONV_A_WIDTH - 1) * stride

    @pl.when(t == 0)
    def _():
        rec_buf[0:hist, :] = conv0_ref[0]
        h_carry[...] = h0_ref[0]

    @pl.when(t != 0)
    def _():
        rec_buf[0:hist, :] = rec_buf[tile:tile + hist, :]

    x = x_ref[...]
    hn = (_rms_normalize(x) * g_ref[...]).astype(BF16)
    proj = _dot(hn, win_ref[...])
    rec_buf[hist:hist + tile, :] = proj[:, d_rnn:]
    xc = _shifted_conv(rec_buf, cw_ref, cb_ref, width=CONV_A_WIDTH, hist=hist, tile=tile,
                       stride=stride)
    xcb = xc.astype(BF16)
    log_sig_lam = jax.nn.log_sigmoid(lam_ref[...])
    for n in range(n_blocks):
        cols = slice(n * lru_block, (n + 1) * lru_block)
        gates = jax.nn.sigmoid(_dot(xcb[:, cols], wg_ref[n]) + bg_ref[n])
        r = gates[:, :lru_block]
        ig = gates[:, lru_block:]
        log_a = LRU_C * r * log_sig_lam[:, cols]
        a = jnp.exp(log_a)
        a_buf[:, cols] = a
        one_minus_a2 = -jnp.tanh(log_a) * (a * a + 1.0)
        b_buf[:, cols] = jnp.sqrt(one_minus_a2) * ig * xc[:, cols]

    if stride == 1:
        row = lax.broadcasted_iota(jnp.int32, (SUBLANES, d_rnn), 0)

        def chunk(c, h_prev):
            rows = pl.ds(pl.multiple_of(c * SUBLANES, SUBLANES), SUBLANES)
            a8 = a_buf[rows, :]
            b8 = b_buf[rows, :]
            for s in (1, 2, 4):
                a_sh = pltpu.roll(a8, s, 0)
                b_sh = pltpu.roll(b8, s, 0)
                valid = row >= s
                b8 = jnp.where(valid, a8 * b_sh + b8, b8)
                a8 = jnp.where(valid, a8 * a_sh, a8)
            h8 = b8 + a8 * h_prev
            b_buf[rows, :] = h8
            return h8[SUBLANES - 1:SUBLANES, :]

        h_last = lax.fori_loop(0, tile // SUBLANES, chunk, h_carry[...])
    else:
        h_last = h_carry[...]
        for k in range(tile // stride):
            rows = slice(k * stride, (k + 1) * stride)
            h_last = a_buf[rows, :] * h_last + b_buf[rows, :]
            b_buf[rows, :] = h_last
    h_carry[...] = h_last
    hn_ref[0] = h_last
    convn_ref[0] = rec_buf[hist + tile - keep:hist + tile, :]

    gate_branch = proj[:, :d_rnn]
    yv = (b_buf[...] * jax.nn.gelu(gate_branch, approximate=True)).astype(BF16)
    y_ref[...] = x + _dot(yv, wout_ref[...])


def _rglru_layer(x2d, conv0, h0, g, w_in, conv_w, conv_b, w_gate, b_gate, lam, w_out,
                 *, groups, stride, tile):
    rows, d_model = x2d.shape
    d_rnn = w_out.shape[0]
    n_tiles = rows // groups // tile
    hist = conv0.shape[1]
    keep = (CONV_A_WIDTH - 1) * stride
    const2 = lambda b, t: (0, 0)
    const3 = lambda b, t: (0, 0, 0)
    kern = functools.partial(_rglru_kernel, stride=stride, hist=hist, tile=tile)
    return pl.pallas_call(
        kern,
        grid=(groups, n_tiles),
        in_specs=[
            pl.BlockSpec((tile, d_model), lambda b, t: (b * n_tiles + t, 0)),
            pl.BlockSpec((1, hist, d_rnn), lambda b, t: (b, 0, 0)),
            pl.BlockSpec((1, stride, d_rnn), lambda b, t: (b, 0, 0)),
            pl.BlockSpec((1, d_model), const2),
            pl.BlockSpec(w_in.shape, const2),
            pl.BlockSpec(conv_w.shape, const2),
            pl.BlockSpec((1, d_rnn), const2),
            pl.BlockSpec(w_gate.shape, const3),
            pl.BlockSpec(b_gate.shape, const3),
            pl.BlockSpec((1, d_rnn), const2),
            pl.BlockSpec(w_out.shape, const2),
        ],
        out_specs=[
            pl.BlockSpec((tile, d_model), lambda b, t: (b * n_tiles + t, 0)),
            pl.BlockSpec((1, keep, d_rnn), lambda b, t: (b, 0, 0)),
            pl.BlockSpec((1, stride, d_rnn), lambda b, t: (b, 0, 0)),
        ],
        out_shape=[
            jax.ShapeDtypeStruct((rows, d_model), F32),
            jax.ShapeDtypeStruct((groups, keep, d_rnn), F32),
            jax.ShapeDtypeStruct((groups, stride, d_rnn), F32),
        ],
        scratch_shapes=[
            pltpu.VMEM((hist + tile, d_rnn), F32),
            pltpu.VMEM((tile, d_rnn), F32),
            pltpu.VMEM((tile, d_rnn), F32),
            pltpu.VMEM((stride, d_rnn), F32),
        ],
        compiler_params=pltpu.CompilerParams(
            dimension_semantics=("arbitrary", "arbitrary"),
            vmem_limit_bytes=VMEM_LIMIT_BYTES),
        name="rglru_layer",
    )(x2d, conv0, h0, g, w_in, conv_w, conv_b, w_gate, b_gate, lam, w_out)


def _ffn_kernel(*refs, has_pre, has_final_norm, stride, hist, tile, n_chunks):
    refs = list(refs)
    x_ref = refs.pop(0)
    if has_pre:
        o_ref = refs.pop(0)
        wo_ref = refs.pop(0)
    gf_ref, wgate_ref, wup_ref, cw_ref, cb_ref, wout_ref, g0_ref = refs[:7]
    refs = refs[7:]
    if has_final_norm:
        gout_ref = refs.pop(0)
    y_ref, convn_ref, h_buf, gate_buf, hist_buf = refs
    t = pl.program_id(1)
    c = pl.program_id(2)
    keep = (CONV_F_WIDTH - 1) * stride

    @pl.when(c == 0)
    def _():
        x1 = x_ref[...]
        if has_pre:
            x1 = x1 + _dot(o_ref[...], wo_ref[...])
        y_ref[...] = x1
        h_buf[...] = (_rms_normalize(x1) * gf_ref[...]).astype(BF16)

    @pl.when(t == 0)
    def _():
        gate_buf[0:hist, :] = g0_ref[0]

    @pl.when(t != 0)
    def _():
        gate_buf[0:hist, :] = hist_buf[c]

    h = h_buf[...]
    gate_buf[hist:hist + tile, :] = _dot(h, wgate_ref[...])
    up = _dot(h, wup_ref[...])
    gate = _shifted_conv(gate_buf, cw_ref, cb_ref, width=CONV_F_WIDTH, hist=hist, tile=tile,
                         stride=stride)
    hist_buf[c] = gate_buf[tile:tile + hist, :]
    convn_ref[0, c] = gate_buf[hist + tile - keep:hist + tile, :]
    yv = (jax.nn.gelu(gate, approximate=True) * up).astype(BF16)
    y_ref[...] += _dot(yv, wout_ref[...])

    if has_final_norm:
        @pl.when(c == n_chunks - 1)
        def _():
            y_ref[...] = _rms_normalize(y_ref[...]) * gout_ref[...]


def _ffn_layer(x2d, g0, gf, w_in, conv_w, conv_b, w_out, *, groups, stride, tile,
               pre=None, final_gain=None):
    rows, d_model = x2d.shape
    d_ff = w_out.shape[0]
    ff_chunk = _pick_tile(d_ff, 1024)
    n_chunks = d_ff // ff_chunk
    n_tiles = rows // groups // tile
    hist = g0.shape[1]
    keep = (CONV_F_WIDTH - 1) * stride
    const2 = lambda b, t, c: (0, 0)
    row_map = lambda b, t, c: (b * n_tiles + t, 0)
    operands = [x2d]
    in_specs = [pl.BlockSpec((tile, d_model), row_map)]
    if pre is not None:
        o2d, w_o = pre
        operands += [o2d, w_o]
        in_specs += [pl.BlockSpec((tile, o2d.shape[1]), row_map), pl.BlockSpec(w_o.shape, const2)]
    operands += [gf, w_in, w_in, conv_w, conv_b, w_out, g0]
    in_specs += [
        pl.BlockSpec((1, d_model), const2),
        pl.BlockSpec((d_model, ff_chunk), lambda b, t, c: (0, c)),
        pl.BlockSpec((d_model, ff_chunk), lambda b, t, c: (0, n_chunks + c)),
        pl.BlockSpec((CONV_F_WIDTH, ff_chunk), lambda b, t, c: (0, c)),
        pl.BlockSpec((1, ff_chunk), lambda b, t, c: (0, c)),
        pl.BlockSpec((ff_chunk, d_model), lambda b, t, c: (c, 0)),
        pl.BlockSpec((1, hist, ff_chunk), lambda b, t, c: (b, 0, c)),
    ]
    if final_gain is not None:
        operands.append(final_gain)
        in_specs.append(pl.BlockSpec((1, d_model), const2))
    kern = functools.partial(_ffn_kernel, has_pre=pre is not None,
                             has_final_norm=final_gain is not None, stride=stride, hist=hist,
                             tile=tile, n_chunks=n_chunks)
    y, conv_new = pl.pallas_call(
        kern,
        grid=(groups, n_tiles, n_chunks),
        in_specs=in_specs,
        out_specs=[
            pl.BlockSpec((tile, d_model), row_map),
            pl.BlockSpec((1, n_chunks, keep, ff_chunk), lambda b, t, c: (b, 0, 0, 0)),
        ],
        out_shape=[
            jax.ShapeDtypeStruct((rows, d_model), F32),
            jax.ShapeDtypeStruct((groups, n_chunks, keep, ff_chunk), F32),
        ],
        scratch_shapes=[
            pltpu.VMEM((tile, d_model), BF16),
            pltpu.VMEM((hist + tile, ff_chunk), F32),
            pltpu.VMEM((n_chunks, hist, ff_chunk), F32),
        ],
        compiler_params=pltpu.CompilerParams(
            dimension_semantics=("arbitrary", "arbitrary", "arbitrary"),
            vmem_limit_bytes=VMEM_LIMIT_BYTES),
        name="conv_ffn",
    )(*operands)
    return y, jnp.swapaxes(conv_new, 1, 2).reshape(groups, keep, d_ff)


def _kvq_kernel(*refs, n_heads, scale, with_aug, tile):
    (x_ref, gkv_ref, gb_ref, wk_ref, wv_ref, wf_ref, bf_ref, wq_ref) = refs[:8]
    if with_aug:
        tri_ref, pq_ref, pk_ref, oneq_ref, onek_ref = refs[8:13]
        kt_ref, vt_ref, lft_ref, q2_ref, k2_ref, vb_ref, f_carry = refs[13:]
    else:
        k_ref, v_ref, lf_ref, q_ref = refs[8:]
    n = _rms_normalize(x_ref[...])
    hk = (n * gkv_ref[...]).astype(BF16)
    hq = (n * gb_ref[...]).astype(BF16)
    k = _dot(hk, wk_ref[...])
    v = _dot(hk, wv_ref[...])
    z = _dot(hk, wf_ref[...]) + bf_ref[...]
    lane = lax.broadcasted_iota(jnp.int32, z.shape, 1)
    lf = jnp.where(lane < n_heads, jax.nn.log_sigmoid(z), 0.0)
    q = _dot(hq, wq_ref[...]) * scale
    if not with_aug:
        k_ref[...] = k
        v_ref[...] = v
        lf_ref[...] = lf[:, :n_heads]
        q_ref[...] = q
        return

    kt_ref[0] = k.T
    vt_ref[0] = v.T
    lft_ref[0] = lf.T[:n_heads, :]

    @pl.when(pl.program_id(1) == 0)
    def _():
        f_carry[...] = jnp.zeros_like(f_carry)

    tri = tri_ref[...]
    hi, mid, lo = _split3_bf16(lf)
    f_cum = _dot(tri, hi) + _dot(tri, mid) + _dot(tri, lo) + f_carry[...]
    f_carry[...] = f_cum[tile - 1:tile, :]

    f_cat = jnp.concatenate(_split3_bf16(f_cum * LOG2_E), axis=1)
    aug_q = (_dot(f_cat, pq_ref[...]) + oneq_ref[...]).astype(BF16)
    aug_k = (_dot(f_cat, pk_ref[...]) + onek_ref[...]).astype(BF16)
    qb = q.astype(BF16)
    kb = k.astype(BF16)
    vb_ref[...] = v.astype(BF16)
    for hp in range(n_heads // 2):
        src = slice(hp * LANES, (hp + 1) * LANES)
        q2_ref[:, 2 * hp * LANES:(2 * hp + 1) * LANES] = qb[:, src]
        q2_ref[:, (2 * hp + 1) * LANES:(2 * hp + 2) * LANES] = aug_q[:, src]
        k2_ref[:, 2 * hp * LANES:(2 * hp + 1) * LANES] = kb[:, src]
        k2_ref[:, (2 * hp + 1) * LANES:(2 * hp + 2) * LANES] = aug_k[:, src]


def _aug_constants(n_heads, tile):
    n_pairs = n_heads // 2
    pq = np.zeros((3 * LANES, n_pairs * LANES), np.float32)
    pk = np.zeros((3 * LANES, n_pairs * LANES), np.float32)
    oneq = np.zeros((1, n_pairs * LANES), np.float32)
    onek = np.zeros((1, n_pairs * LANES), np.float32)
    for h in range(n_heads):
        base = (h // 2) * LANES + (h % 2) * AUG_LANES_PER_HEAD
        for piece in range(3):
            pq[piece * LANES + h, base + piece] = 1.0
            onek[0, base + piece] = 1.0
            pk[piece * LANES + h, base + 3 + piece] = -1.0
            oneq[0, base + 3 + piece] = 1.0
    tri = np.tril(np.ones((tile, tile), np.float32))
    return (jnp.asarray(tri, BF16), jnp.asarray(pq, BF16), jnp.asarray(pk, BF16),
            jnp.asarray(oneq), jnp.asarray(onek))


def _kvq_layer(x2d, gkv, gb, wk, wv, wf, bf, wq, *, n_heads, groups, tile, with_aug):
    rows, d_model = x2d.shape
    hd = wk.shape[1]
    n_tiles = rows // groups // tile
    scale = float(hd // n_heads) ** -0.5
    const2 = lambda b, t: (0, 0)
    row_map = lambda b, t: (b * n_tiles + t, 0)
    operands = [x2d, gkv, gb, wk, wv, wf, bf, wq]
    in_specs = [pl.BlockSpec((tile, d_model), row_map)]
    in_specs += [pl.BlockSpec(a.shape, const2) for a in operands[1:]]
    scratch = []
    if with_aug:
        scale *= LOG2_E
        seq = rows // groups
        col_map = lambda b, t: (b, 0, t)
        consts = _aug_constants(n_heads, tile)
        operands += list(consts)
        in_specs += [pl.BlockSpec(a.shape, const2) for a in consts]
        out_shape = [jax.ShapeDtypeStruct((groups, hd, seq), F32),
                     jax.ShapeDtypeStruct((groups, hd, seq), F32),
                     jax.ShapeDtypeStruct((groups, n_heads, seq), F32),
                     jax.ShapeDtypeStruct((rows, 2 * hd), BF16),
                     jax.ShapeDtypeStruct((rows, 2 * hd), BF16),
                     jax.ShapeDtypeStruct((rows, hd), BF16)]
        out_specs = [pl.BlockSpec((1, hd, tile), col_map), pl.BlockSpec((1, hd, tile), col_map),
                     pl.BlockSpec((1, n_heads, tile), col_map),
                     pl.BlockSpec((tile, 2 * hd), row_map), pl.BlockSpec((tile, 2 * hd), row_map),
                     pl.BlockSpec((tile, hd), row_map)]
        scratch = [pltpu.VMEM((1, LANES), F32)]
    else:
        out_shape = [jax.ShapeDtypeStruct((rows, hd), F32), jax.ShapeDtypeStruct((rows, hd), F32),
                     jax.ShapeDtypeStruct((rows, n_heads), F32),
                     jax.ShapeDtypeStruct((rows, hd), F32)]
        out_specs = [pl.BlockSpec((tile, hd), row_map), pl.BlockSpec((tile, hd), row_map),
                     pl.BlockSpec((tile, n_heads), row_map), pl.BlockSpec((tile, hd), row_map)]
    kern = functools.partial(_kvq_kernel, n_heads=n_heads, scale=scale, with_aug=with_aug,
                             tile=tile)
    return pl.pallas_call(
        kern,
        grid=(groups, n_tiles),
        in_specs=in_specs,
        out_specs=out_specs,
        out_shape=out_shape,
        scratch_shapes=scratch,
        compiler_params=pltpu.CompilerParams(
            dimension_semantics=("arbitrary", "arbitrary"),
            vmem_limit_bytes=VMEM_LIMIT_BYTES),
        name="kv_q_proj",
    )(*operands)


def _prompt_attn_kernel(q2_ref, k2_ref, v_ref, o_ref, qh_buf, m_buf, l_buf, acc_buf,
                        *, tq, tk, head_dim):
    qi = pl.program_id(2)

    q2 = q2_ref[...]
    lane = lax.broadcasted_iota(jnp.int32, q2.shape, 1)
    aug = lane - LANES
    for j in range(2):
        own = ((lane >= j * head_dim) & (lane < (j + 1) * head_dim)) | (
            (aug >= j * AUG_LANES_PER_HEAD) & (aug < (j + 1) * AUG_LANES_PER_HEAD))
        qh_buf[j] = jnp.where(own, q2, jnp.zeros_like(q2))
    m_buf[...] = jnp.full_like(m_buf, MASK_VALUE)
    l_buf[...] = jnp.zeros_like(l_buf)
    acc_buf[...] = jnp.zeros_like(acc_buf)

    def step(ki, masked):
        rows = pl.ds(pl.multiple_of(ki * tk, tk), tk)
        k2 = k2_ref[0, rows, :]
        v = v_ref[0, rows, :]
        for j in range(2):
            s = _dot_nt(qh_buf[j], k2)
            if masked:
                qpos = qi * tq + lax.broadcasted_iota(jnp.int32, s.shape, 0)
                kpos = ki * tk + lax.broadcasted_iota(jnp.int32, s.shape, 1)
                s = jnp.where(kpos <= qpos, s, MASK_VALUE)
            m_prev = m_buf[j]
            m_next = jnp.maximum(m_prev, jnp.max(s, axis=1, keepdims=True))
            alpha = jnp.exp2(m_prev - m_next)
            p = jnp.exp2(s - pltpu.repeat(m_next, tk // LANES, axis=1))
            l_buf[j] = alpha * l_buf[j] + jnp.sum(p, axis=1, keepdims=True)
            acc_buf[j] = alpha * acc_buf[j] + _dot(p.astype(BF16), v)
            m_buf[j] = m_next

    n_full = (qi * tq) // tk

    def full_step(ki, carry):
        step(ki, masked=False)
        return carry

    lax.fori_loop(0, n_full, full_step, 0)
    for r in range(tq // tk):
        step(n_full + r, masked=True)

    lane = lax.broadcasted_iota(jnp.int32, (tq, LANES), 1)
    o0 = acc_buf[0] / l_buf[0]
    o1 = acc_buf[1] / l_buf[1]
    o_ref[...] = jnp.where(lane < head_dim, o0, o1).astype(o_ref.dtype)


def _prompt_attention(q2, k2, vb, *, batch, seq, n_heads, tq, tk):
    rows = batch * seq
    hd = vb.shape[1]
    head_dim = hd // n_heads
    n_pairs = n_heads // 2
    nq = seq // tq
    kern = functools.partial(_prompt_attn_kernel, tq=tq, tk=tk, head_dim=head_dim)
    return pl.pallas_call(
        kern,
        grid=(batch, n_pairs, nq),
        in_specs=[
            pl.BlockSpec((tq, 2 * LANES), lambda b, hp, qi: (b * nq + qi, hp)),
            pl.BlockSpec((1, seq, 2 * LANES), lambda b, hp, qi: (b, 0, hp)),
            pl.BlockSpec((1, seq, LANES), lambda b, hp, qi: (b, 0, hp)),
        ],
        out_specs=pl.BlockSpec((tq, LANES), lambda b, hp, qi: (b * nq + qi, hp)),
        out_shape=jax.ShapeDtypeStruct((rows, hd), BF16),
        scratch_shapes=[
            pltpu.VMEM((2, tq, 2 * LANES), BF16),
            pltpu.VMEM((2, tq, LANES), F32),
            pltpu.VMEM((2, tq, LANES), F32),
            pltpu.VMEM((2, tq, LANES), F32),
        ],
        compiler_params=pltpu.CompilerParams(
            dimension_semantics=("arbitrary", "arbitrary", "arbitrary"),
            vmem_limit_bytes=VMEM_LIMIT_BYTES),
        name="prompt_attention",
    )(q2, k2.reshape(batch, seq, 2 * hd), vb.reshape(batch, seq, hd))


def _sample_attn_kernel(pt_ref, *refs, n_heads, head_dim, dec_seq, pages_per_step, page_size):
    del pt_ref
    n_in = 3 * pages_per_step
    k_refs = refs[0:pages_per_step]
    v_refs = refs[pages_per_step:2 * pages_per_step]
    lf_refs = refs[2 * pages_per_step:n_in]
    q_ref, kn_ref, vn_ref, lfn_ref, sl_ref = refs[n_in:n_in + 5]
    o_ref = refs[n_in + 5]
    qrows_buf, cn_col, carry, m_buf, l_buf, acc_buf = refs[n_in + 6:]
    g = pl.program_id(1)
    n_rows = dec_seq * n_heads
    hd = n_heads * head_dim

    def head_mask(shape):
        r = lax.broadcasted_iota(jnp.int32, shape, 0) % n_heads
        lane = lax.broadcasted_iota(jnp.int32, shape, 1)
        return (lane >= r * head_dim) & (lane < (r + 1) * head_dim)

    def column_of(rowvec):
        wide = jnp.broadcast_to(rowvec, (n_rows, rowvec.shape[1]))
        r = lax.broadcasted_iota(jnp.int32, wide.shape, 0) % n_heads
        lane = lax.broadcasted_iota(jnp.int32, wide.shape, 1)
        return jnp.sum(jnp.where(lane == r, wide, 0.0), axis=1, keepdims=True)

    def online_update(s, weighted_values):
        m_prev = m_buf[...]
        m_next = jnp.maximum(m_prev, jnp.max(s, axis=1, keepdims=True))
        alpha = jnp.exp(m_prev - m_next)
        p = jnp.exp(s - m_next[:, 0:1])
        l_buf[...] = alpha * l_buf[...] + jnp.sum(p, axis=1, keepdims=True)
        acc_buf[...] = acc_buf[...] * alpha[:, 0:1] + weighted_values(p.astype(BF16))
        m_buf[...] = m_next

    @pl.when(g == 0)
    def _():
        q = q_ref[0]
        mask = head_mask((n_rows, hd))
        qrep = jnp.concatenate(
            [jnp.broadcast_to(q[t:t + 1, :], (n_heads, hd)) for t in range(dec_seq)], axis=0)
        qrows = jnp.where(mask, qrep, 0.0).astype(BF16)
        qrows_buf[...] = qrows
        m_buf[...] = jnp.full_like(m_buf, MASK_VALUE)
        l_buf[...] = jnp.zeros_like(l_buf)
        acc_buf[...] = jnp.zeros_like(acc_buf)
        carry[...] = jnp.zeros_like(carry)

        lfn = lfn_ref[0]
        cums = []
        run = None
        for t in range(dec_seq):
            run = lfn[t:t + 1, :] if run is None else run + lfn[t:t + 1, :]
            cums.append(run)
        r_t = lax.broadcasted_iota(jnp.int32, (n_rows, 1), 0) // n_heads
        cn = jnp.zeros((n_rows, 1), F32)
        for t in range(dec_seq):
            cn = jnp.where(r_t == t, column_of(cums[t]), cn)
        cn_col[...] = cn
        qrows_f = qrows.astype(F32)
        kn = kn_ref[0].astype(BF16).astype(F32)
        vn = vn_ref[0].astype(BF16).astype(F32)
        lane = lax.broadcasted_iota(jnp.int32, (n_rows, LANES), 1)
        s_new = jnp.full((n_rows, LANES), MASK_VALUE, F32)
        for s in range(dec_seq):
            qk = jnp.sum(qrows_f * kn[s:s + 1, :], axis=1, keepdims=True)
            s_new = jnp.where((lane == s) & (r_t >= s), qk + cn - column_of(cums[s]), s_new)

        def new_values(p):
            p = p.astype(F32)
            return sum(p[:, s:s + 1] * vn[s:s + 1, :] for s in range(dec_seq))

        online_update(s_new, new_values)

    sl = sl_ref[...]
    biases = [None] * pages_per_step
    run = carry[...]
    for j in reversed(range(pages_per_step)):
        lft = lf_refs[j][0]
        hi, mid, lo = _split3_bf16(lft)
        suffix = _dot(hi, sl) + _dot(mid, sl) + _dot(lo, sl) + run
        biases[j] = jnp.concatenate([suffix] * dec_seq, axis=0)
        run = run + jnp.sum(lft, axis=1, keepdims=True)
    carry[...] = run
    bias = jnp.concatenate(biases, axis=1) + cn_col[...]
    kt_all = jnp.concatenate([r[0].astype(BF16) for r in k_refs], axis=1)
    vt_all = jnp.concatenate([r[0].astype(BF16) for r in v_refs], axis=1)
    s = _dot(qrows_buf[...], kt_all) + bias
    online_update(s, lambda p: _dot_nt(p, vt_all))

    @pl.when(g == pl.num_programs(1) - 1)
    def _():
        out = jnp.where(head_mask((n_rows, hd)), acc_buf[...] / l_buf[:, 0:1], 0.0)
        o_ref[0] = jnp.sum(out.reshape(dec_seq, n_heads, hd), axis=1).astype(o_ref.dtype)


def _sample_attention(page_table, cache_kt, cache_vt, cache_lft, q, k_new, v_new, lf_new,
                      *, n_heads, pages_per_step):
    dec_batch, dec_seq, hd = q.shape
    n_pages = page_table.shape[1]
    page_size = cache_kt.shape[2]
    head_dim = hd // n_heads
    n_groups = n_pages // pages_per_step
    n_rows = dec_seq * n_heads
    sl = jnp.asarray(np.tril(np.ones((page_size, page_size), np.float32), -1), BF16)

    def page_map(j):
        def index_map(b, g, pt):
            return (pt[b, (n_groups - 1 - g) * pages_per_step + j], 0, 0)
        return index_map

    seq_map = lambda b, g, pt: (b, 0, 0)
    in_specs = (
        [pl.BlockSpec((1, hd, page_size), page_map(j)) for j in range(pages_per_step)]
        + [pl.BlockSpec((1, hd, page_size), page_map(j)) for j in range(pages_per_step)]
        + [pl.BlockSpec((1, n_heads, page_size), page_map(j)) for j in range(pages_per_step)]
        + [pl.BlockSpec((1, dec_seq, hd), seq_map)] * 3
        + [pl.BlockSpec((1, dec_seq, n_heads), seq_map),
           pl.BlockSpec((page_size, page_size), lambda b, g, pt: (0, 0))])
    kern = functools.partial(_sample_attn_kernel, n_heads=n_heads, head_dim=head_dim,
                             dec_seq=dec_seq, pages_per_step=pages_per_step, page_size=page_size)
    grid_spec = pltpu.PrefetchScalarGridSpec(
        num_scalar_prefetch=1,
        grid=(dec_batch, n_groups),
        in_specs=in_specs,
        out_specs=pl.BlockSpec((1, dec_seq, hd), seq_map),
        scratch_shapes=[
            pltpu.VMEM((n_rows, hd), BF16),
            pltpu.VMEM((n_rows, 1), F32),
            pltpu.VMEM((n_heads, LANES), F32),
            pltpu.VMEM((n_rows, LANES), F32),
            pltpu.VMEM((n_rows, LANES), F32),
            pltpu.VMEM((n_rows, hd), F32),
        ])
    operands = ([cache_kt] * pages_per_step + [cache_vt] * pages_per_step
                + [cache_lft] * pages_per_step + [q, k_new, v_new, lf_new, sl])
    return pl.pallas_call(
        kern,
        grid_spec=grid_spec,
        out_shape=jax.ShapeDtypeStruct((dec_batch, dec_seq, hd), BF16),
        compiler_params=pltpu.CompilerParams(
            dimension_semantics=("arbitrary", "arbitrary"),
            vmem_limit_bytes=VMEM_LIMIT_BYTES),
        name="sample_attention",
    )(page_table, *operands)


def _pad_history(state, hist):
    keep = state.shape[1]
    if keep == hist:
        return state
    return jnp.pad(state, ((0, 0), (hist - keep, 0), (0, 0)))


def _to_time_major(a):
    b, t = a.shape[:2]
    return jnp.swapaxes(a, 0, 1).reshape((1, t * b) + a.shape[2:])


def _from_time_major(a, b):
    t = a.shape[1] // b
    return jnp.swapaxes(a.reshape((t, b) + a.shape[2:]), 0, 1)


def kernel(x_prompt, x_sample, cache_k, cache_v, cache_logf, state_lru_h, state_lru_conv,
           state_ffn_conv, page_table, norm_a_g, w_a_in, conv_a_w, conv_a_b, w_a_gate, b_a_gate,
           lru_lambda, w_a_out, norm_f_g, w_f_in, conv_f_w, conv_f_b, w_f_out, norm_kv_g, w_kv,
           b_forget, norm_b_g, w_q, w_o, norm_out_g):
    batch, seq, d_model = x_prompt.shape
    dec_batch, dec_seq, _ = x_sample.shape
    n_phys, page_size, n_heads, head_dim = cache_k.shape
    hd = n_heads * head_dim
    n_a = w_a_in.shape[0]
    n_b = w_q.shape[0]
    d_rnn = w_a_out.shape[1]
    d_ff = w_f_out.shape[1]
    assert 2 * head_dim == LANES and n_heads % 2 == 0 and n_heads <= LANES
    assert dec_batch % SUBLANES == 0 and dec_seq >= CONV_A_WIDTH - 1

    row = lambda a: a.reshape(1, -1).astype(F32)
    w_a_in_b = w_a_in.astype(BF16)
    w_a_gate_b = w_a_gate.astype(BF16)
    w_a_out_b = w_a_out.astype(BF16)
    w_f_in_b = w_f_in.astype(BF16)
    w_f_out_b = w_f_out.astype(BF16)
    w_q_b = w_q.astype(BF16)
    w_o_b = w_o.astype(BF16)
    wk = w_kv[:, :hd].astype(BF16)
    wv = w_kv[:, hd:2 * hd].astype(BF16)
    wf = jnp.pad(w_kv[:, 2 * hd:], ((0, 0), (0, LANES - n_heads))).astype(BF16)
    bf = jnp.pad(b_forget.astype(F32), (0, LANES - n_heads)).reshape(1, LANES)

    def trunk(x2d, lru_h0, lru_conv0, ffn_conv0, *, groups, stride, tiles, with_aug,
              attention_fn):
        lru_h_new, lru_conv_new, ffn_conv_new = [], [], []
        layer = 0
        hist_a = _round_up((CONV_A_WIDTH - 1) * stride, SUBLANES)
        hist_f = _round_up((CONV_F_WIDTH - 1) * stride, SUBLANES)

        def ffn(x2d, layer, **kw):
            return _ffn_layer(x2d, _pad_history(ffn_conv0[layer], hist_f), row(norm_f_g[layer]),
                              w_f_in_b[layer], conv_f_w[layer], row(conv_f_b[layer]),
                              w_f_out_b[layer], groups=groups, stride=stride, tile=tiles["ffn"],
                              **kw)

        for i in range(n_a):
            x2d, c_new, h_new = _rglru_layer(
                x2d, _pad_history(lru_conv0[i], hist_a), lru_h0[i], row(norm_a_g[i]), w_a_in_b[i],
                conv_a_w[i], row(conv_a_b[i]), w_a_gate_b[i], b_a_gate[i][:, None, :],
                row(lru_lambda[i]), w_a_out_b[i], groups=groups, stride=stride,
                tile=tiles["rglru"])
            x2d, f_new = ffn(x2d, layer)
            lru_h_new.append(h_new)
            lru_conv_new.append(c_new)
            ffn_conv_new.append(f_new)
            layer += 1
        kvq = _kvq_layer(x2d, row(norm_kv_g), row(norm_b_g[0]), wk, wv, wf, bf, w_q_b[0],
                         n_heads=n_heads, groups=groups, tile=tiles["kvq"],
                         with_aug=with_aug)
        k_new, v_new, lf_new = kvq[:3]
        for j in range(n_b):
            assert j == 0, "one attention layer per shared K/V projection is supported"
            o2d = attention_fn(kvq)
            x2d, f_new = ffn(x2d, layer, pre=(o2d, w_o_b[j]),
                             final_gain=row(norm_out_g) if j == n_b - 1 else None)
            ffn_conv_new.append(f_new)
            layer += 1
        return x2d, k_new, v_new, lf_new, lru_h_new, lru_conv_new, ffn_conv_new

    def prompt_attention_fn(kvq):
        q2, k2, vb = kvq[3:]
        t_attn = _pick_tile(seq, 512)
        return _prompt_attention(q2, k2, vb, batch=batch, seq=seq, n_heads=n_heads, tq=t_attn,
                                 tk=t_attn)

    zeros = lambda *s: jnp.zeros(s, F32)
    tiles = {"rglru": _pick_tile(seq, 256), "ffn": _pick_tile(seq, 512),
             "kvq": _pick_tile(seq, 512)}
    (y, k_p, v_p, lf_p, h_p, c_p, f_p) = trunk(
        x_prompt.reshape(batch * seq, d_model),
        zeros(n_a, batch, 1, d_rnn), zeros(n_a, batch, CONV_A_WIDTH - 1, d_rnn),
        zeros(n_a + n_b, batch, CONV_F_WIDTH - 1, d_ff),
        groups=batch, stride=1, tiles=tiles, with_aug=True, attention_fn=prompt_attention_fn)
    y_prompt = y.reshape(batch, seq, d_model)
    k_prompt = k_p.reshape(batch, n_heads, head_dim, seq).transpose(0, 3, 1, 2)
    v_prompt = v_p.reshape(batch, n_heads, head_dim, seq).transpose(0, 3, 1, 2)
    logf_prompt = lf_p.transpose(0, 2, 1)
    lru_h_prompt = jnp.stack([h.reshape(batch, d_rnn) for h in h_p], axis=0)
    lru_conv_prompt = jnp.stack(c_p, axis=0)
    ffn_conv_prompt = jnp.stack(f_p, axis=0)

    cache_kt = cache_k.transpose(0, 2, 3, 1).reshape(n_phys, hd, page_size)
    cache_vt = cache_v.transpose(0, 2, 3, 1).reshape(n_phys, hd, page_size)
    cache_lft = jnp.swapaxes(cache_logf, 1, 2)
    n_pages = page_table.shape[1]

    def sample_attention_fn(kvq):
        k_t, v_t, lf_t, q_t = kvq
        bm = lambda a: _from_time_major(a[None], dec_batch)
        o = _sample_attention(page_table, cache_kt, cache_vt, cache_lft, bm(q_t), bm(k_t),
                              bm(v_t), bm(lf_t), n_heads=n_heads,
                              pages_per_step=_pick_tile(n_pages, 8))
        return _to_time_major(o)[0]

    n_rows = dec_batch * dec_seq
    tiles = {"rglru": n_rows, "ffn": n_rows, "kvq": n_rows}
    (y, k_s, v_s, lf_s, h_s, c_s, f_s) = trunk(
        _to_time_major(x_sample)[0],
        state_lru_h[:, None], jnp.stack([_to_time_major(s) for s in state_lru_conv]),
        jnp.stack([_to_time_major(s) for s in state_ffn_conv]),
        groups=1, stride=dec_batch, tiles=tiles, with_aug=False,
        attention_fn=sample_attention_fn)
    bm = lambda a: _from_time_major(a[None], dec_batch)
    y_sample = bm(y)
    k_sample = bm(k_s).reshape(dec_batch, dec_seq, n_heads, head_dim)
    v_sample = bm(v_s).reshape(dec_batch, dec_seq, n_heads, head_dim)
    logf_sample = bm(lf_s)
    lru_h_sample = jnp.stack([h[0] for h in h_s], axis=0)
    lru_conv_sample = jnp.stack([_from_time_major(c, dec_batch) for c in c_s], axis=0)
    ffn_conv_sample = jnp.stack([_from_time_major(f, dec_batch) for f in f_s], axis=0)

    return (y_prompt, y_sample, k_prompt, v_prompt, logf_prompt, lru_h_prompt, lru_conv_prompt,
            ffn_conv_prompt, k_sample, v_sample, logf_sample, lru_h_sample, lru_conv_sample,
            ffn_conv_sample)
```

```python
import functools

import numpy as np
import jax
import jax.numpy as jnp
from jax import lax
from jax.experimental import pallas as pl
from jax.experimental.pallas import tpu as pltpu

F32 = jnp.float32
BF16 = jnp.bfloat16

LANES = 128
SUBLANES = 8
VMEM_LIMIT_BYTES = 56 * 1024 * 1024

RMS_EPS = 1e-6
LRU_C = 8.0
CONV_A_WIDTH = 4
CONV_F_WIDTH = 3
MASK_VALUE = -1e30
LOG2_E = 1.4426950408889634

AUG_LANES_PER_HEAD = 8


def _round_up(x, m):
    return (x + m - 1) // m * m


def _pick_tile(n, target):
    t = min(n, target)
    while n % t:
        t //= 2
    return t


def _rms_normalize(x):
    return x * lax.rsqrt(jnp.mean(x * x, axis=-1, keepdims=True) + RMS_EPS)


def _split3_bf16(x):
    hi = x.astype(BF16)
    r1 = x - hi.astype(F32)
    mid = r1.astype(BF16)
    lo = (r1 - mid.astype(F32)).astype(BF16)
    return hi, mid, lo


def _dot(a, b):
    return jnp.dot(a, b, preferred_element_type=F32)


def _dot_nt(a, b):
    return lax.dot_general(a, b, (((1,), (1,)), ((), ())), preferred_element_type=F32)


def _shifted_conv(buf_ref, w, b, *, width, hist, tile, stride):
    acc = None
    for j in range(width):
        back = (width - 1 - j) * stride
        term = buf_ref[hist - back:hist - back + tile, :] * w[j:j + 1, :]
        acc = term if acc is None else acc + term
    return acc + b


def _rglru_kernel(x_ref, conv0_ref, h0_ref, g_ref, win_ref, cw_ref, cb_ref, wg_ref, bg_ref,
                  lam_ref, wout_ref, y_ref, convn_ref, hn_ref, rec_buf, a_buf, b_buf, h_carry,
                  *, stride, hist, tile):
    t = pl.program_id(1)
    d_rnn = rec_buf.shape[1]
    n_blocks, lru_block, _ = wg_ref.shape
    keep = (CONV_A_WIDTH - 1) * stride

    @pl.when(t == 0)
    def _():
        rec_buf[0:hist, :] = conv0_ref[0]
        h_carry[...] = h0_ref[0]

    @pl.when(t != 0)
    def _():
        rec_buf[0:hist, :] = rec_buf[tile:tile + hist, :]

    x = x_ref[...]
    hn = (_rms_normalize(x) * g_ref[...]).astype(BF16)
    proj = _dot(hn, win_ref[...])
    rec_buf[hist:hist + tile, :] = proj[:, d_rnn:]
    xc = _shifted_conv(rec_buf, cw_ref[...], cb_ref[...], width=CONV_A_WIDTH, hist=hist,
                       tile=tile, stride=stride)
    xcb = xc.astype(BF16)
    log_sig_lam = jax.nn.log_sigmoid(lam_ref[...])
    for n in range(n_blocks):
        cols = slice(n * lru_block, (n + 1) * lru_block)
        gates = jax.nn.sigmoid(_dot(xcb[:, cols], wg_ref[n]) + bg_ref[n])
        r = gates[:, :lru_block]
        ig = gates[:, lru_block:]
        log_a = LRU_C * r * log_sig_lam[:, cols]
        a = jnp.exp(log_a)
        a_buf[:, cols] = a
        one_minus_a2 = -jnp.tanh(log_a) * (a * a + 1.0)
        b_buf[:, cols] = jnp.sqrt(one_minus_a2) * ig * xc[:, cols]

    if stride == 1:
        row = lax.broadcasted_iota(jnp.int32, (SUBLANES, d_rnn), 0)

        def chunk(c, h_prev):
            rows = pl.ds(pl.multiple_of(c * SUBLANES, SUBLANES), SUBLANES)
            a8 = a_buf[rows, :]
            b8 = b_buf[rows, :]
            for s in (1, 2, 4):
                a_sh = pltpu.roll(a8, s, 0)
                b_sh = pltpu.roll(b8, s, 0)
                valid = row >= s
                b8 = jnp.where(valid, a8 * b_sh + b8, b8)
                a8 = jnp.where(valid, a8 * a_sh, a8)
            h8 = b8 + a8 * h_prev
            b_buf[rows, :] = h8
            return h8[SUBLANES - 1:SUBLANES, :]

        h_last = lax.fori_loop(0, tile // SUBLANES, chunk, h_carry[...])
    else:
        h_last = h_carry[...]
        for k in range(tile // stride):
            rows = slice(k * stride, (k + 1) * stride)
            h_last = a_buf[rows, :] * h_last + b_buf[rows, :]
            b_buf[rows, :] = h_last
    h_carry[...] = h_last
    hn_ref[0] = h_last
    convn_ref[0] = rec_buf[hist + tile - keep:hist + tile, :]

    gate_branch = proj[:, :d_rnn]
    yv = (b_buf[...] * jax.nn.gelu(gate_branch, approximate=True)).astype(BF16)
    y_ref[...] = x + _dot(yv, wout_ref[...])


def _rglru_layer(x2d, conv0, h0, g, w_in, conv_w, conv_b, w_gate, b_gate, lam, w_out,
                 *, groups, stride, tile):
    rows, d_model = x2d.shape
    d_rnn = w_out.shape[0]
    n_tiles = rows // groups // tile
    hist = conv0.shape[1]
    keep = (CONV_A_WIDTH - 1) * stride
    const2 = lambda b, t: (0, 0)
    const3 = lambda b, t: (0, 0, 0)
    kern = functools.partial(_rglru_kernel, stride=stride, hist=hist, tile=tile)
    return pl.pallas_call(
        kern,
        grid=(groups, n_tiles),
        in_specs=[
            pl.BlockSpec((tile, d_model), lambda b, t: (b * n_tiles + t, 0)),
            pl.BlockSpec((1, hist, d_rnn), lambda b, t: (b, 0, 0)),
            pl.BlockSpec((1, stride, d_rnn), lambda b, t: (b, 0, 0)),
            pl.BlockSpec((1, d_model), const2),
            pl.BlockSpec(w_in.shape, const2),
            pl.BlockSpec(conv_w.shape, const2),
            pl.BlockSpec((1, d_rnn), const2),
            pl.BlockSpec(w_gate.shape, const3),
            pl.BlockSpec(b_gate.shape, const3),
            pl.BlockSpec((1, d_rnn), const2),
            pl.BlockSpec(w_out.shape, const2),
        ],
        out_specs=[
            pl.BlockSpec((tile, d_model), lambda b, t: (b * n_tiles + t, 0)),
            pl.BlockSpec((1, keep, d_rnn), lambda b, t: (b, 0, 0)),
            pl.BlockSpec((1, stride, d_rnn), lambda b, t: (b, 0, 0)),
        ],
        out_shape=[
            jax.ShapeDtypeStruct((rows, d_model), F32),
            jax.ShapeDtypeStruct((groups, keep, d_rnn), F32),
            jax.ShapeDtypeStruct((groups, stride, d_rnn), F32),
        ],
        scratch_shapes=[
            pltpu.VMEM((hist + tile, d_rnn), F32),
            pltpu.VMEM((tile, d_rnn), F32),
            pltpu.VMEM((tile, d_rnn), F32),
            pltpu.VMEM((stride, d_rnn), F32),
        ],
        compiler_params=pltpu.CompilerParams(
            dimension_semantics=("arbitrary", "arbitrary"),
            vmem_limit_bytes=VMEM_LIMIT_BYTES),
        name="rglru_layer",
    )(x2d, conv0, h0, g, w_in, conv_w, conv_b, w_gate, b_gate, lam, w_out)


def _ffn_kernel(*refs, has_pre, has_final_norm, stride, hist, tile, ff_chunk):
    refs = list(refs)
    x_ref = refs.pop(0)
    if has_pre:
        o_ref = refs.pop(0)
        wo_ref = refs.pop(0)
    gf_ref, win_ref, cw_ref, cb_ref, wout_ref, g0_ref = refs[:6]
    refs = refs[6:]
    if has_final_norm:
        gout_ref = refs.pop(0)
    y_ref, convn_ref, gate_buf, hist_buf = refs
    t = pl.program_id(1)
    d_ff = wout_ref.shape[0]
    keep = (CONV_F_WIDTH - 1) * stride

    x1 = x_ref[...]
    if has_pre:
        x1 = x1 + _dot(o_ref[...], wo_ref[...])
    y_ref[...] = x1
    h = (_rms_normalize(x1) * gf_ref[...]).astype(BF16)

    for c in range(d_ff // ff_chunk):
        cols = slice(c * ff_chunk, (c + 1) * ff_chunk)
        up_cols = slice(d_ff + c * ff_chunk, d_ff + (c + 1) * ff_chunk)

        @pl.when(t == 0)
        def _():
            gate_buf[0:hist, :] = g0_ref[0, :, cols]

        @pl.when(t != 0)
        def _():
            gate_buf[0:hist, :] = hist_buf[c]

        gate_buf[hist:hist + tile, :] = _dot(h, win_ref[:, cols])
        up = _dot(h, win_ref[:, up_cols])
        gate = _shifted_conv(gate_buf, cw_ref[:, cols], cb_ref[:, cols], width=CONV_F_WIDTH,
                             hist=hist, tile=tile, stride=stride)
        hist_buf[c] = gate_buf[tile:tile + hist, :]
        convn_ref[0, :, cols] = gate_buf[hist + tile - keep:hist + tile, :]
        yv = (jax.nn.gelu(gate, approximate=True) * up).astype(BF16)
        y_ref[...] += _dot(yv, wout_ref[cols, :])

    if has_final_norm:
        y_ref[...] = _rms_normalize(y_ref[...]) * gout_ref[...]


def _ffn_layer(x2d, g0, gf, w_in, conv_w, conv_b, w_out, *, groups, stride, tile,
               pre=None, final_gain=None):
    rows, d_model = x2d.shape
    d_ff = w_out.shape[0]
    ff_chunk = _pick_tile(d_ff, 1024)
    n_chunks = d_ff // ff_chunk
    n_tiles = rows // groups // tile
    hist = g0.shape[1]
    keep = (CONV_F_WIDTH - 1) * stride
    const2 = lambda b, t: (0, 0)
    row_map = lambda b, t: (b * n_tiles + t, 0)
    resident = functools.partial(pl.BlockSpec, index_map=const2, pipeline_mode=pl.Buffered(1))
    operands = [x2d]
    in_specs = [pl.BlockSpec((tile, d_model), row_map)]
    if pre is not None:
        o2d, w_o = pre
        operands += [o2d, w_o]
        in_specs += [pl.BlockSpec((tile, o2d.shape[1]), row_map), resident(w_o.shape)]
    operands += [gf, w_in, conv_w, conv_b, w_out, g0]
    in_specs += [
        resident((1, d_model)),
        resident(w_in.shape),
        resident(conv_w.shape),
        resident(conv_b.shape),
        resident(w_out.shape),
        pl.BlockSpec((1, hist, d_ff), lambda b, t: (b, 0, 0)),
    ]
    if final_gain is not None:
        operands.append(final_gain)
        in_specs.append(resident((1, d_model)))
    kern = functools.partial(_ffn_kernel, has_pre=pre is not None,
                             has_final_norm=final_gain is not None, stride=stride, hist=hist,
                             tile=tile, ff_chunk=ff_chunk)
    return pl.pallas_call(
        kern,
        grid=(groups, n_tiles),
        in_specs=in_specs,
        out_specs=[
            pl.BlockSpec((tile, d_model), row_map),
            pl.BlockSpec((1, keep, d_ff), lambda b, t: (b, 0, 0)),
        ],
        out_shape=[
            jax.ShapeDtypeStruct((rows, d_model), F32),
            jax.ShapeDtypeStruct((groups, keep, d_ff), F32),
        ],
        scratch_shapes=[
            pltpu.VMEM((hist + tile, ff_chunk), F32),
            pltpu.VMEM((n_chunks, hist, ff_chunk), F32),
        ],
        compiler_params=pltpu.CompilerParams(
            dimension_semantics=("arbitrary", "arbitrary"),
            vmem_limit_bytes=VMEM_LIMIT_BYTES),
        name="conv_ffn",
    )(*operands)


def _kvq_kernel(*refs, n_heads, scale, with_aug, tile):
    (x_ref, gkv_ref, gb_ref, wk_ref, wv_ref, wf_ref, bf_ref, wq_ref) = refs[:8]
    if with_aug:
        tri_ref, pq_ref, pk_ref, oneq_ref, onek_ref = refs[8:13]
        kt_ref, vt_ref, lft_ref, q2t_ref, k2_ref, vtb_ref, f_carry = refs[13:]
    else:
        k_ref, v_ref, lf_ref, q_ref = refs[8:]
    n = _rms_normalize(x_ref[...])
    hk = (n * gkv_ref[...]).astype(BF16)
    hq = (n * gb_ref[...]).astype(BF16)
    k = _dot(hk, wk_ref[...])
    v = _dot(hk, wv_ref[...])
    z = _dot(hk, wf_ref[...]) + bf_ref[...]
    lane = lax.broadcasted_iota(jnp.int32, z.shape, 1)
    lf = jnp.where(lane < n_heads, jax.nn.log_sigmoid(z), 0.0)
    q = _dot(hq, wq_ref[...]) * scale
    if not with_aug:
        k_ref[...] = k
        v_ref[...] = v
        lf_ref[...] = lf[:, :n_heads]
        q_ref[...] = q
        return

    kt_ref[0] = k.T
    vt = v.T
    vt_ref[0] = vt
    vtb_ref[0, 0] = vt.astype(BF16)
    lft_ref[0] = lf.T[:n_heads, :]

    @pl.when(pl.program_id(1) == 0)
    def _():
        f_carry[...] = jnp.zeros_like(f_carry)

    tri = tri_ref[...]
    hi, mid, lo = _split3_bf16(lf)
    f_cum = _dot(tri, hi) + _dot(tri, mid) + _dot(tri, lo) + f_carry[...]
    f_carry[...] = f_cum[tile - 1:tile, :]

    f_cat = jnp.concatenate(_split3_bf16(f_cum * LOG2_E), axis=1)
    aug_qt = (_dot(f_cat, pq_ref[...]) + oneq_ref[...]).T.astype(BF16)
    aug_k = (_dot(f_cat, pk_ref[...]) + onek_ref[...]).astype(BF16)
    qt = q.T.astype(BF16)
    kb = k.astype(BF16)
    for hp in range(n_heads // 2):
        src = slice(hp * LANES, (hp + 1) * LANES)
        q2t_ref[0, 0, 2 * hp * LANES:(2 * hp + 1) * LANES, :] = qt[src, :]
        q2t_ref[0, 0, (2 * hp + 1) * LANES:(2 * hp + 2) * LANES, :] = aug_qt[src, :]
        k2_ref[:, 2 * hp * LANES:(2 * hp + 1) * LANES] = kb[:, src]
        k2_ref[:, (2 * hp + 1) * LANES:(2 * hp + 2) * LANES] = aug_k[:, src]


def _aug_constants(n_heads, tile):
    n_pairs = n_heads // 2
    pq = np.zeros((3 * LANES, n_pairs * LANES), np.float32)
    pk = np.zeros((3 * LANES, n_pairs * LANES), np.float32)
    oneq = np.zeros((1, n_pairs * LANES), np.float32)
    onek = np.zeros((1, n_pairs * LANES), np.float32)
    for h in range(n_heads):
        base = (h // 2) * LANES + (h % 2) * AUG_LANES_PER_HEAD
        for piece in range(3):
            pq[piece * LANES + h, base + piece] = 1.0
            onek[0, base + piece] = 1.0
            pk[piece * LANES + h, base + 3 + piece] = -1.0
            oneq[0, base + 3 + piece] = 1.0
    tri = np.tril(np.ones((tile, tile), np.float32))
    return (jnp.asarray(tri, BF16), jnp.asarray(pq, BF16), jnp.asarray(pk, BF16),
            jnp.asarray(oneq), jnp.asarray(onek))


def _kvq_layer(x2d, gkv, gb, wk, wv, wf, bf, wq, *, n_heads, groups, tile, with_aug):
    rows, d_model = x2d.shape
    hd = wk.shape[1]
    n_tiles = rows // groups // tile
    scale = float(hd // n_heads) ** -0.5
    const2 = lambda b, t: (0, 0)
    row_map = lambda b, t: (b * n_tiles + t, 0)
    operands = [x2d, gkv, gb, wk, wv, wf, bf, wq]
    in_specs = [pl.BlockSpec((tile, d_model), row_map)]
    in_specs += [pl.BlockSpec(a.shape, const2) for a in operands[1:]]
    scratch = []
    if with_aug:
        scale *= LOG2_E
        seq = rows // groups
        col_map = lambda b, t: (b, 0, t)
        consts = _aug_constants(n_heads, tile)
        operands += list(consts)
        in_specs += [pl.BlockSpec(a.shape, const2) for a in consts]
        tile_map = lambda b, t: (b, t, 0, 0)
        out_shape = [jax.ShapeDtypeStruct((groups, hd, seq), F32),
                     jax.ShapeDtypeStruct((groups, hd, seq), F32),
                     jax.ShapeDtypeStruct((groups, n_heads, seq), F32),
                     jax.ShapeDtypeStruct((groups, n_tiles, 2 * hd, tile), BF16),
                     jax.ShapeDtypeStruct((rows, 2 * hd), BF16),
                     jax.ShapeDtypeStruct((groups, n_tiles, hd, tile), BF16)]
        out_specs = [pl.BlockSpec((1, hd, tile), col_map), pl.BlockSpec((1, hd, tile), col_map),
                     pl.BlockSpec((1, n_heads, tile), col_map),
                     pl.BlockSpec((1, 1, 2 * hd, tile), tile_map),
                     pl.BlockSpec((tile, 2 * hd), row_map),
                     pl.BlockSpec((1, 1, hd, tile), tile_map)]
        scratch = [pltpu.VMEM((1, LANES), F32)]
    else:
        out_shape = [jax.ShapeDtypeStruct((rows, hd), F32), jax.ShapeDtypeStruct((rows, hd), F32),
                     jax.ShapeDtypeStruct((rows, n_heads), F32),
                     jax.ShapeDtypeStruct((rows, hd), F32)]
        out_specs = [pl.BlockSpec((tile, hd), row_map), pl.BlockSpec((tile, hd), row_map),
                     pl.BlockSpec((tile, n_heads), row_map), pl.BlockSpec((tile, hd), row_map)]
    kern = functools.partial(_kvq_kernel, n_heads=n_heads, scale=scale, with_aug=with_aug,
                             tile=tile)
    return pl.pallas_call(
        kern,
        grid=(groups, n_tiles),
        in_specs=in_specs,
        out_specs=out_specs,
        out_shape=out_shape,
        scratch_shapes=scratch,
        compiler_params=pltpu.CompilerParams(
            dimension_semantics=("arbitrary", "arbitrary"),
            vmem_limit_bytes=VMEM_LIMIT_BYTES),
        name="kv_q_proj",
    )(*operands)


def _prompt_attn_kernel(q2t_ref, k2_ref, vt_ref, o_ref, qh_buf, s_buf, m_buf, l_buf, acc_buf,
                        *, tq, tk, head_dim):
    qi = pl.program_id(2)

    q2t = q2t_ref[0, 0]
    chan = lax.broadcasted_iota(jnp.int32, q2t.shape, 0)
    aug = chan - LANES
    for j in range(2):
        own = ((chan >= j * head_dim) & (chan < (j + 1) * head_dim)) | (
            (aug >= j * AUG_LANES_PER_HEAD) & (aug < (j + 1) * AUG_LANES_PER_HEAD))
        qh_buf[j] = jnp.where(own, q2t, jnp.zeros_like(q2t))
    m_buf[...] = jnp.full_like(m_buf, MASK_VALUE)
    l_buf[...] = jnp.zeros_like(l_buf)
    acc_buf[...] = jnp.zeros_like(acc_buf)

    def score(j, ki):
        k2 = k2_ref[0, pl.ds(pl.multiple_of(ki * tk, tk), tk), :]
        s_buf[j] = _dot(k2, qh_buf[j])

    def absorb(j, ki, masked):
        s = s_buf[j]
        if masked:
            kpos = ki * tk + lax.broadcasted_iota(jnp.int32, s.shape, 0)
            qpos = qi * tq + lax.broadcasted_iota(jnp.int32, s.shape, 1)
            s = jnp.where(kpos <= qpos, s, MASK_VALUE)
        m_prev = m_buf[j]
        m_next = jnp.maximum(m_prev, jnp.max(s, axis=0, keepdims=True))
        alpha = jnp.exp2(m_prev - m_next)
        p = jnp.exp2(s - m_next)
        l_buf[j] = alpha * l_buf[j] + jnp.sum(p, axis=0, keepdims=True)
        m_buf[j] = m_next
        vt = vt_ref[0, ki, j * head_dim:(j + 1) * head_dim, :]
        acc_buf[j] = alpha * acc_buf[j] + _dot(vt, p.astype(BF16))

    score(0, 0)

    def full_step(ki, carry):
        score(1, ki)
        absorb(0, ki, masked=False)
        score(0, ki + 1)
        absorb(1, ki, masked=False)
        return carry

    lax.fori_loop(0, qi, full_step, 0)
    score(1, qi)
    absorb(0, qi, masked=True)
    absorb(1, qi, masked=True)

    ot = jnp.concatenate([acc_buf[j] / l_buf[j] for j in range(2)], axis=0)
    o_ref[...] = ot.T.astype(o_ref.dtype)


def _prompt_attention(q2t, k2, vtb, *, n_heads, tq, tk):
    batch, n_tiles, hd, tile = vtb.shape
    assert tile == tq == tk, "attention tiles follow the projection kernel's row tile"
    seq = n_tiles * tile
    rows = batch * seq
    head_dim = hd // n_heads
    n_pairs = n_heads // 2
    kern = functools.partial(_prompt_attn_kernel, tq=tq, tk=tk, head_dim=head_dim)
    return pl.pallas_call(
        kern,
        grid=(batch, n_pairs, n_tiles),
        in_specs=[
            pl.BlockSpec((1, 1, 2 * LANES, tq), lambda b, hp, qi: (b, qi, hp, 0)),
            pl.BlockSpec((1, seq, 2 * LANES), lambda b, hp, qi: (b, 0, hp)),
            pl.BlockSpec((1, n_tiles, LANES, tk), lambda b, hp, qi: (b, 0, hp, 0)),
        ],
        out_specs=pl.BlockSpec((tq, LANES), lambda b, hp, qi: (b * n_tiles + qi, hp)),
        out_shape=jax.ShapeDtypeStruct((rows, hd), BF16),
        scratch_shapes=[
            pltpu.VMEM((2, 2 * LANES, tq), BF16),
            pltpu.VMEM((2, tk, tq), F32),
            pltpu.VMEM((2, 1, tq), F32),
            pltpu.VMEM((2, 1, tq), F32),
            pltpu.VMEM((2, head_dim, tq), F32),
        ],
        compiler_params=pltpu.CompilerParams(
            dimension_semantics=("arbitrary", "arbitrary", "arbitrary"),
            vmem_limit_bytes=VMEM_LIMIT_BYTES),
        name="prompt_attention",
    )(q2t, k2.reshape(batch, seq, 2 * hd), vtb)


def _sample_attn_kernel(pt_ref, *refs, n_heads, head_dim, dec_seq, pages_per_step, page_size):
    del pt_ref
    n_in = 3 * pages_per_step
    k_refs = refs[0:pages_per_step]
    v_refs = refs[pages_per_step:2 * pages_per_step]
    lf_refs = refs[2 * pages_per_step:n_in]
    q_ref, kn_ref, vn_ref, lfn_ref, sl_ref = refs[n_in:n_in + 5]
    o_ref = refs[n_in + 5]
    qrows_buf, cn_col, carry, m_buf, l_buf, acc_buf = refs[n_in + 6:]
    g = pl.program_id(1)
    n_rows = dec_seq * n_heads
    hd = n_heads * head_dim

    def head_mask(shape):
        r = lax.broadcasted_iota(jnp.int32, shape, 0) % n_heads
        lane = lax.broadcasted_iota(jnp.int32, shape, 1)
        return (lane >= r * head_dim) & (lane < (r + 1) * head_dim)

    def column_of(rowvec):
        wide = jnp.broadcast_to(rowvec, (n_rows, rowvec.shape[1]))
        r = lax.broadcasted_iota(jnp.int32, wide.shape, 0) % n_heads
        lane = lax.broadcasted_iota(jnp.int32, wide.shape, 1)
        return jnp.sum(jnp.where(lane == r, wide, 0.0), axis=1, keepdims=True)

    def online_update(s, weighted_values):
        m_prev = m_buf[...]
        m_next = jnp.maximum(m_prev, jnp.max(s, axis=1, keepdims=True))
        alpha = jnp.exp(m_prev - m_next)
        p = jnp.exp(s - m_next[:, 0:1])
        l_buf[...] = alpha * l_buf[...] + jnp.sum(p, axis=1, keepdims=True)
        acc_buf[...] = acc_buf[...] * alpha[:, 0:1] + weighted_values(p.astype(BF16))
        m_buf[...] = m_next

    @pl.when(g == 0)
    def _():
        q = q_ref[0]
        mask = head_mask((n_rows, hd))
        qrep = jnp.concatenate(
            [jnp.broadcast_to(q[t:t + 1, :], (n_heads, hd)) for t in range(dec_seq)], axis=0)
        qrows = jnp.where(mask, qrep, 0.0).astype(BF16)
        qrows_buf[...] = qrows
        m_buf[...] = jnp.full_like(m_buf, MASK_VALUE)
        l_buf[...] = jnp.zeros_like(l_buf)
        acc_buf[...] = jnp.zeros_like(acc_buf)
        carry[...] = jnp.zeros_like(carry)

        lfn = lfn_ref[0]
        cums = []
        run = None
        for t in range(dec_seq):
            run = lfn[t:t + 1, :] if run is None else run + lfn[t:t + 1, :]
            cums.append(run)
        r_t = lax.broadcasted_iota(jnp.int32, (n_rows, 1), 0) // n_heads
        cn = jnp.zeros((n_rows, 1), F32)
        for t in range(dec_seq):
            cn = jnp.where(r_t == t, column_of(cums[t]), cn)
        cn_col[...] = cn
        qrows_f = qrows.astype(F32)
        kn = kn_ref[0].astype(BF16).astype(F32)
        vn = vn_ref[0].astype(BF16).astype(F32)
        lane = lax.broadcasted_iota(jnp.int32, (n_rows, LANES), 1)
        s_new = jnp.full((n_rows, LANES), MASK_VALUE, F32)
        for s in range(dec_seq):
            qk = jnp.sum(qrows_f * kn[s:s + 1, :], axis=1, keepdims=True)
            s_new = jnp.where((lane == s) & (r_t >= s), qk + cn - column_of(cums[s]), s_new)

        def new_values(p):
            p = p.astype(F32)
            return sum(p[:, s:s + 1] * vn[s:s + 1, :] for s in range(dec_seq))

        online_update(s_new, new_values)

    sl = sl_ref[...]
    biases = [None] * pages_per_step
    run = carry[...]
    for j in reversed(range(pages_per_step)):
        lft = lf_refs[j][0]
        hi, mid, lo = _split3_bf16(lft)
        suffix = _dot(hi, sl) + _dot(mid, sl) + _dot(lo, sl) + run
        biases[j] = jnp.concatenate([suffix] * dec_seq, axis=0)
        run = run + jnp.sum(lft, axis=1, keepdims=True)
    carry[...] = run
    bias = jnp.concatenate(biases, axis=1) + cn_col[...]
    kt_all = jnp.concatenate([r[0].astype(BF16) for r in k_refs], axis=1)
    vt_all = jnp.concatenate([r[0].astype(BF16) for r in v_refs], axis=1)
    s = _dot(qrows_buf[...], kt_all) + bias
    online_update(s, lambda p: _dot_nt(p, vt_all))

    @pl.when(g == pl.num_programs(1) - 1)
    def _():
        out = jnp.where(head_mask((n_rows, hd)), acc_buf[...] / l_buf[:, 0:1], 0.0)
        o_ref[0] = jnp.sum(out.reshape(dec_seq, n_heads, hd), axis=1).astype(o_ref.dtype)


def _sample_attention(page_table, cache_kt, cache_vt, cache_lft, q, k_new, v_new, lf_new,
                      *, n_heads, pages_per_step):
    dec_batch, dec_seq, hd = q.shape
    n_pages = page_table.shape[1]
    page_size = cache_kt.shape[2]
    head_dim = hd // n_heads
    n_groups = n_pages // pages_per_step
    n_rows = dec_seq * n_heads
    sl = jnp.asarray(np.tril(np.ones((page_size, page_size), np.float32), -1), BF16)

    def page_map(j):
        def index_map(b, g, pt):
            return (pt[b, (n_groups - 1 - g) * pages_per_step + j], 0, 0)
        return index_map

    seq_map = lambda b, g, pt: (b, 0, 0)
    in_specs = (
        [pl.BlockSpec((1, hd, page_size), page_map(j)) for j in range(pages_per_step)]
        + [pl.BlockSpec((1, hd, page_size), page_map(j)) for j in range(pages_per_step)]
        + [pl.BlockSpec((1, n_heads, page_size), page_map(j)) for j in range(pages_per_step)]
        + [pl.BlockSpec((1, dec_seq, hd), seq_map)] * 3
        + [pl.BlockSpec((1, dec_seq, n_heads), seq_map),
           pl.BlockSpec((page_size, page_size), lambda b, g, pt: (0, 0))])
    kern = functools.partial(_sample_attn_kernel, n_heads=n_heads, head_dim=head_dim,
                             dec_seq=dec_seq, pages_per_step=pages_per_step, page_size=page_size)
    grid_spec = pltpu.PrefetchScalarGridSpec(
        num_scalar_prefetch=1,
        grid=(dec_batch, n_groups),
        in_specs=in_specs,
        out_specs=pl.BlockSpec((1, dec_seq, hd), seq_map),
        scratch_shapes=[
            pltpu.VMEM((n_rows, hd), BF16),
            pltpu.VMEM((n_rows, 1), F32),
            pltpu.VMEM((n_heads, LANES), F32),
            pltpu.VMEM((n_rows, LANES), F32),
            pltpu.VMEM((n_rows, LANES), F32),
            pltpu.VMEM((n_rows, hd), F32),
        ])
    operands = ([cache_kt] * pages_per_step + [cache_vt] * pages_per_step
                + [cache_lft] * pages_per_step + [q, k_new, v_new, lf_new, sl])
    return pl.pallas_call(
        kern,
        grid_spec=grid_spec,
        out_shape=jax.ShapeDtypeStruct((dec_batch, dec_seq, hd), BF16),
        compiler_params=pltpu.CompilerParams(
            dimension_semantics=("arbitrary", "arbitrary"),
            vmem_limit_bytes=VMEM_LIMIT_BYTES),
        name="sample_attention",
    )(page_table, *operands)


def _pad_history(state, hist):
    keep = state.shape[1]
    if keep == hist:
        return state
    return jnp.pad(state, ((0, 0), (hist - keep, 0), (0, 0)))


def _to_time_major(a):
    b, t = a.shape[:2]
    return jnp.swapaxes(a, 0, 1).reshape((1, t * b) + a.shape[2:])


def _from_time_major(a, b):
    t = a.shape[1] // b
    return jnp.swapaxes(a.reshape((t, b) + a.shape[2:]), 0, 1)


def kernel(x_prompt, x_sample, cache_k, cache_v, cache_logf, state_lru_h, state_lru_conv,
           state_ffn_conv, page_table, norm_a_g, w_a_in, conv_a_w, conv_a_b, w_a_gate, b_a_gate,
           lru_lambda, w_a_out, norm_f_g, w_f_in, conv_f_w, conv_f_b, w_f_out, norm_kv_g, w_kv,
           b_forget, norm_b_g, w_q, w_o, norm_out_g):
    batch, seq, d_model = x_prompt.shape
    dec_batch, dec_seq, _ = x_sample.shape
    n_phys, page_size, n_heads, head_dim = cache_k.shape
    hd = n_heads * head_dim
    n_a = w_a_in.shape[0]
    n_b = w_q.shape[0]
    d_rnn = w_a_out.shape[1]
    d_ff = w_f_out.shape[1]
    assert 2 * head_dim == LANES and n_heads % 2 == 0 and n_heads <= LANES
    assert dec_batch % SUBLANES == 0 and dec_seq >= CONV_A_WIDTH - 1

    row = lambda a: a.reshape(1, -1).astype(F32)
    w_a_in_b = w_a_in.astype(BF16)
    w_a_gate_b = w_a_gate.astype(BF16)
    w_a_out_b = w_a_out.astype(BF16)
    w_f_in_b = w_f_in.astype(BF16)
    w_f_out_b = w_f_out.astype(BF16)
    w_q_b = w_q.astype(BF16)
    w_o_b = w_o.astype(BF16)
    wk = w_kv[:, :hd].astype(BF16)
    wv = w_kv[:, hd:2 * hd].astype(BF16)
    wf = jnp.pad(w_kv[:, 2 * hd:], ((0, 0), (0, LANES - n_heads))).astype(BF16)
    bf = jnp.pad(b_forget.astype(F32), (0, LANES - n_heads)).reshape(1, LANES)

    def trunk(x2d, lru_h0, lru_conv0, ffn_conv0, *, groups, stride, tiles, with_aug,
              attention_fn):
        lru_h_new, lru_conv_new, ffn_conv_new = [], [], []
        layer = 0
        hist_a = _round_up((CONV_A_WIDTH - 1) * stride, SUBLANES)
        hist_f = _round_up((CONV_F_WIDTH - 1) * stride, SUBLANES)

        def ffn(x2d, layer, **kw):
            return _ffn_layer(x2d, _pad_history(ffn_conv0[layer], hist_f), row(norm_f_g[layer]),
                              w_f_in_b[layer], conv_f_w[layer], row(conv_f_b[layer]),
                              w_f_out_b[layer], groups=groups, stride=stride, tile=tiles["ffn"],
                              **kw)

        for i in range(n_a):
            x2d, c_new, h_new = _rglru_layer(
                x2d, _pad_history(lru_conv0[i], hist_a), lru_h0[i], row(norm_a_g[i]), w_a_in_b[i],
                conv_a_w[i], row(conv_a_b[i]), w_a_gate_b[i], b_a_gate[i][:, None, :],
                row(lru_lambda[i]), w_a_out_b[i], groups=groups, stride=stride,
                tile=tiles["rglru"])
            x2d, f_new = ffn(x2d, layer)
            lru_h_new.append(h_new)
            lru_conv_new.append(c_new)
            ffn_conv_new.append(f_new)
            layer += 1
        kvq = _kvq_layer(x2d, row(norm_kv_g), row(norm_b_g[0]), wk, wv, wf, bf, w_q_b[0],
                         n_heads=n_heads, groups=groups, tile=tiles["kvq"],
                         with_aug=with_aug)
        k_new, v_new, lf_new = kvq[:3]
        for j in range(n_b):
            assert j == 0, "one attention layer per shared K/V projection is supported"
            o2d = attention_fn(kvq)
            x2d, f_new = ffn(x2d, layer, pre=(o2d, w_o_b[j]),
                             final_gain=row(norm_out_g) if j == n_b - 1 else None)
            ffn_conv_new.append(f_new)
            layer += 1
        return x2d, k_new, v_new, lf_new, lru_h_new, lru_conv_new, ffn_conv_new

    zeros = lambda *s: jnp.zeros(s, F32)
    tiles = {"rglru": _pick_tile(seq, 256), "ffn": _pick_tile(seq, 512),
             "kvq": _pick_tile(seq, 512)}

    def prompt_attention_fn(kvq):
        q2t, k2, vtb = kvq[3:]
        return _prompt_attention(q2t, k2, vtb, n_heads=n_heads, tq=tiles["kvq"], tk=tiles["kvq"])

    (y, k_p, v_p, lf_p, h_p, c_p, f_p) = trunk(
        x_prompt.reshape(batch * seq, d_model),
        zeros(n_a, batch, 1, d_rnn), zeros(n_a, batch, CONV_A_WIDTH - 1, d_rnn),
        zeros(n_a + n_b, batch, CONV_F_WIDTH - 1, d_ff),
        groups=batch, stride=1, tiles=tiles, with_aug=True, attention_fn=prompt_attention_fn)
    y_prompt = y.reshape(batch, seq, d_model)
    k_prompt = k_p.reshape(batch, n_heads, head_dim, seq).transpose(0, 3, 1, 2)
    v_prompt = v_p.reshape(batch, n_heads, head_dim, seq).transpose(0, 3, 1, 2)
    logf_prompt = lf_p.transpose(0, 2, 1)
    lru_h_prompt = jnp.stack([h.reshape(batch, d_rnn) for h in h_p], axis=0)
    lru_conv_prompt = jnp.stack(c_p, axis=0)
    ffn_conv_prompt = jnp.stack(f_p, axis=0)

    cache_kt = cache_k.transpose(0, 2, 3, 1).reshape(n_phys, hd, page_size)
    cache_vt = cache_v.transpose(0, 2, 3, 1).reshape(n_phys, hd, page_size)
    cache_lft = jnp.swapaxes(cache_logf, 1, 2)
    n_pages = page_table.shape[1]

    def sample_attention_fn(kvq):
        k_t, v_t, lf_t, q_t = kvq
        bm = lambda a: _from_time_major(a[None], dec_batch)
        o = _sample_attention(page_table, cache_kt, cache_vt, cache_lft, bm(q_t), bm(k_t),
                              bm(v_t), bm(lf_t), n_heads=n_heads,
                              pages_per_step=_pick_tile(n_pages, 8))
        return _to_time_major(o)[0]

    n_rows = dec_batch * dec_seq
    tiles = {"rglru": n_rows, "ffn": n_rows, "kvq": n_rows}
    (y, k_s, v_s, lf_s, h_s, c_s, f_s) = trunk(
        _to_time_major(x_sample)[0],
        state_lru_h[:, None], jnp.stack([_to_time_major(s) for s in state_lru_conv]),
        jnp.stack([_to_time_major(s) for s in state_ffn_conv]),
        groups=1, stride=dec_batch, tiles=tiles, with_aug=False,
        attention_fn=sample_attention_fn)
    bm = lambda a: _from_time_major(a[None], dec_batch)
    y_sample = bm(y)
    k_sample = bm(k_s).reshape(dec_batch, dec_seq, n_heads, head_dim)
    v_sample = bm(v_s).reshape(dec_batch, dec_seq, n_heads, head_dim)
    logf_sample = bm(lf_s)
    lru_h_sample = jnp.stack([h[0] for h in h_s], axis=0)
    lru_conv_sample = jnp.stack([_from_time_major(c, dec_batch) for c in c_s], axis=0)
    ffn_conv_sample = jnp.stack([_from_time_major(f, dec_batch) for f in f_s], axis=0)

    return (y_prompt, y_sample, k_prompt, v_prompt, logf_prompt, lru_h_prompt, lru_conv_prompt,
            ffn_conv_prompt, k_sample, v_sample, logf_sample, lru_h_sample, lru_conv_sample,
            ffn_conv_sample)
```

```python
import functools
import math

import numpy as np
import jax
import jax.numpy as jnp
from jax import lax
from jax.experimental import pallas as pl
from jax.experimental.pallas import tpu as pltpu

F32 = jnp.float32
BF16 = jnp.bfloat16

LANES = 128
SUBLANES = 8
VMEM_LIMIT_BYTES = 56 * 1024 * 1024

RMS_EPS = 1e-6
LRU_C = 8.0
CONV_A_WIDTH = 4
CONV_F_WIDTH = 3
MASK_VALUE = -1e30
LOG2_E = 1.4426950408889634

AUG_LANES_PER_HEAD = 8
SUM_ROWS = 16
SAMPLE_PAGES_PER_STEP = 16
SAMPLE_SOFTMAX_CHAINS = 4


def _round_up(x, m):
    return (x + m - 1) // m * m


def _pick_tile(n, target):
    t = min(n, target)
    while n % t:
        t //= 2
    return t


def _rms_normalize(x):
    return x * lax.rsqrt(jnp.mean(x * x, axis=-1, keepdims=True) + RMS_EPS)


def _split3_bf16(x):
    hi = x.astype(BF16)
    r1 = x - hi.astype(F32)
    mid = r1.astype(BF16)
    lo = (r1 - mid.astype(F32)).astype(BF16)
    return hi, mid, lo


def _dot(a, b):
    return jnp.dot(a, b, preferred_element_type=F32)


def _dot_nt(a, b):
    return lax.dot_general(a, b, (((1,), (1,)), ((), ())), preferred_element_type=F32)


def _shifted_conv(buf_ref, w, b, *, width, hist, tile, stride):
    acc = None
    for j in range(width):
        back = (width - 1 - j) * stride
        term = buf_ref[hist - back:hist - back + tile, :] * w[j:j + 1, :]
        acc = term if acc is None else acc + term
    return acc + b


def _rglru_kernel(x_ref, conv0_ref, h0_ref, g_ref, win_ref, cw_ref, cb_ref, wg_ref, bg_ref,
                  lam_ref, wout_ref, y_ref, convn_ref, hn_ref, rec_buf, a_buf, b_buf, h_carry,
                  *, stride, hist, tile):
    t = pl.program_id(1)
    d_rnn = rec_buf.shape[1]
    n_blocks, lru_block, _ = wg_ref.shape
    keep = (CONV_A_WIDTH - 1) * stride

    @pl.when(t == 0)
    def _():
        rec_buf[0:hist, :] = conv0_ref[0]
        h_carry[...] = h0_ref[0]

    @pl.when(t != 0)
    def _():
        rec_buf[0:hist, :] = rec_buf[tile:tile + hist, :]

    x = x_ref[...]
    hn = (_rms_normalize(x) * g_ref[...]).astype(BF16)
    proj = _dot(hn, win_ref[...])
    rec_buf[hist:hist + tile, :] = proj[:, d_rnn:]
    xc = _shifted_conv(rec_buf, cw_ref[...], cb_ref[...], width=CONV_A_WIDTH, hist=hist,
                       tile=tile, stride=stride)
    xcb = xc.astype(BF16)
    log_sig_lam = jax.nn.log_sigmoid(lam_ref[...])
    for n in range(n_blocks):
        cols = slice(n * lru_block, (n + 1) * lru_block)
        gates = jax.nn.sigmoid(_dot(xcb[:, cols], wg_ref[n]) + bg_ref[n])
        r = gates[:, :lru_block]
        ig = gates[:, lru_block:]
        log_a = LRU_C * r * log_sig_lam[:, cols]
        a = jnp.exp(log_a)
        a_buf[:, cols] = a
        one_minus_a2 = -jnp.tanh(log_a) * (a * a + 1.0)
        b_buf[:, cols] = jnp.sqrt(one_minus_a2) * ig * xc[:, cols]

    if stride == 1:
        row = lax.broadcasted_iota(jnp.int32, (SUBLANES, d_rnn), 0)

        def chunk(c, h_prev):
            rows = pl.ds(pl.multiple_of(c * SUBLANES, SUBLANES), SUBLANES)
            a8 = a_buf[rows, :]
            b8 = b_buf[rows, :]
            for s in (1, 2, 4):
                a_sh = pltpu.roll(a8, s, 0)
                b_sh = pltpu.roll(b8, s, 0)
                valid = row >= s
                b8 = jnp.where(valid, a8 * b_sh + b8, b8)
                a8 = jnp.where(valid, a8 * a_sh, a8)
            h8 = b8 + a8 * h_prev
            b_buf[rows, :] = h8
            return h8[SUBLANES - 1:SUBLANES, :]

        h_last = lax.fori_loop(0, tile // SUBLANES, chunk, h_carry[...])
    else:
        h_last = h_carry[...]
        for k in range(tile // stride):
            rows = slice(k * stride, (k + 1) * stride)
            h_last = a_buf[rows, :] * h_last + b_buf[rows, :]
            b_buf[rows, :] = h_last
    h_carry[...] = h_last
    hn_ref[0] = h_last
    convn_ref[0] = rec_buf[hist + tile - keep:hist + tile, :]

    gate_branch = proj[:, :d_rnn]
    yv = (b_buf[...] * jax.nn.gelu(gate_branch, approximate=True)).astype(BF16)
    y_ref[...] = x + _dot(yv, wout_ref[...])


def _rglru_layer(x2d, conv0, h0, g, w_in, conv_w, conv_b, w_gate, b_gate, lam, w_out,
                 *, groups, stride, tile):
    rows, d_model = x2d.shape
    d_rnn = w_out.shape[0]
    n_tiles = rows // groups // tile
    hist = conv0.shape[1]
    keep = (CONV_A_WIDTH - 1) * stride
    const2 = lambda b, t: (0, 0)
    const3 = lambda b, t: (0, 0, 0)
    kern = functools.partial(_rglru_kernel, stride=stride, hist=hist, tile=tile)
    return pl.pallas_call(
        kern,
        grid=(groups, n_tiles),
        in_specs=[
            pl.BlockSpec((tile, d_model), lambda b, t: (b * n_tiles + t, 0)),
            pl.BlockSpec((1, hist, d_rnn), lambda b, t: (b, 0, 0)),
            pl.BlockSpec((1, stride, d_rnn), lambda b, t: (b, 0, 0)),
            pl.BlockSpec((1, d_model), const2),
            pl.BlockSpec(w_in.shape, const2),
            pl.BlockSpec(conv_w.shape, const2),
            pl.BlockSpec((1, d_rnn), const2),
            pl.BlockSpec(w_gate.shape, const3),
            pl.BlockSpec(b_gate.shape, const3),
            pl.BlockSpec((1, d_rnn), const2),
            pl.BlockSpec(w_out.shape, const2),
        ],
        out_specs=[
            pl.BlockSpec((tile, d_model), lambda b, t: (b * n_tiles + t, 0)),
            pl.BlockSpec((1, keep, d_rnn), lambda b, t: (b, 0, 0)),
            pl.BlockSpec((1, stride, d_rnn), lambda b, t: (b, 0, 0)),
        ],
        out_shape=[
            jax.ShapeDtypeStruct((rows, d_model), F32),
            jax.ShapeDtypeStruct((groups, keep, d_rnn), F32),
            jax.ShapeDtypeStruct((groups, stride, d_rnn), F32),
        ],
        scratch_shapes=[
            pltpu.VMEM((hist + tile, d_rnn), F32),
            pltpu.VMEM((tile, d_rnn), F32),
            pltpu.VMEM((tile, d_rnn), F32),
            pltpu.VMEM((stride, d_rnn), F32),
        ],
        compiler_params=pltpu.CompilerParams(
            dimension_semantics=("arbitrary", "arbitrary"),
            vmem_limit_bytes=VMEM_LIMIT_BYTES),
        name="rglru_layer",
    )(x2d, conv0, h0, g, w_in, conv_w, conv_b, w_gate, b_gate, lam, w_out)


def _ffn_kernel(*refs, has_pre, has_final_norm, stride, hist, tile, ff_chunk):
    refs = list(refs)
    x_ref = refs.pop(0)
    if has_pre:
        o_ref = refs.pop(0)
        wo_ref = refs.pop(0)
    gf_ref, win_ref, cw_ref, cb_ref, wout_ref, g0_ref = refs[:6]
    refs = refs[6:]
    if has_final_norm:
        gout_ref = refs.pop(0)
    y_ref, convn_ref, gate_buf, hist_buf = refs
    t = pl.program_id(1)
    d_ff = wout_ref.shape[0]
    keep = (CONV_F_WIDTH - 1) * stride

    x1 = x_ref[...]
    if has_pre:
        x1 = x1 + _dot(o_ref[...], wo_ref[...])
    y_ref[...] = x1
    h = (_rms_normalize(x1) * gf_ref[...]).astype(BF16)

    for c in range(d_ff // ff_chunk):
        cols = slice(c * ff_chunk, (c + 1) * ff_chunk)
        up_cols = slice(d_ff + c * ff_chunk, d_ff + (c + 1) * ff_chunk)

        @pl.when(t == 0)
        def _():
            gate_buf[0:hist, :] = g0_ref[0, :, cols]

        @pl.when(t != 0)
        def _():
            gate_buf[0:hist, :] = hist_buf[c]

        gate_buf[hist:hist + tile, :] = _dot(h, win_ref[:, cols])
        up = _dot(h, win_ref[:, up_cols])
        gate = _shifted_conv(gate_buf, cw_ref[:, cols], cb_ref[:, cols], width=CONV_F_WIDTH,
                             hist=hist, tile=tile, stride=stride)
        hist_buf[c] = gate_buf[tile:tile + hist, :]
        convn_ref[0, :, cols] = gate_buf[hist + tile - keep:hist + tile, :]
        yv = (jax.nn.gelu(gate, approximate=True) * up).astype(BF16)
        y_ref[...] += _dot(yv, wout_ref[cols, :])

    if has_final_norm:
        y_ref[...] = _rms_normalize(y_ref[...]) * gout_ref[...]


def _ffn_layer(x2d, g0, gf, w_in, conv_w, conv_b, w_out, *, groups, stride, tile,
               pre=None, final_gain=None):
    rows, d_model = x2d.shape
    d_ff = w_out.shape[0]
    ff_chunk = _pick_tile(d_ff, 1024)
    n_chunks = d_ff // ff_chunk
    n_tiles = rows // groups // tile
    hist = g0.shape[1]
    keep = (CONV_F_WIDTH - 1) * stride
    const2 = lambda b, t: (0, 0)
    row_map = lambda b, t: (b * n_tiles + t, 0)
    resident = functools.partial(pl.BlockSpec, index_map=const2, pipeline_mode=pl.Buffered(1))
    operands = [x2d]
    in_specs = [pl.BlockSpec((tile, d_model), row_map)]
    if pre is not None:
        o2d, w_o = pre
        operands += [o2d, w_o]
        in_specs += [pl.BlockSpec((tile, o2d.shape[1]), row_map), resident(w_o.shape)]
    operands += [gf, w_in, conv_w, conv_b, w_out, g0]
    in_specs += [
        resident((1, d_model)),
        resident(w_in.shape),
        resident(conv_w.shape),
        resident(conv_b.shape),
        resident(w_out.shape),
        pl.BlockSpec((1, hist, d_ff), lambda b, t: (b, 0, 0)),
    ]
    if final_gain is not None:
        operands.append(final_gain)
        in_specs.append(resident((1, d_model)))
    kern = functools.partial(_ffn_kernel, has_pre=pre is not None,
                             has_final_norm=final_gain is not None, stride=stride, hist=hist,
                             tile=tile, ff_chunk=ff_chunk)
    return pl.pallas_call(
        kern,
        grid=(groups, n_tiles),
        in_specs=in_specs,
        out_specs=[
            pl.BlockSpec((tile, d_model), row_map),
            pl.BlockSpec((1, keep, d_ff), lambda b, t: (b, 0, 0)),
        ],
        out_shape=[
            jax.ShapeDtypeStruct((rows, d_model), F32),
            jax.ShapeDtypeStruct((groups, keep, d_ff), F32),
        ],
        scratch_shapes=[
            pltpu.VMEM((hist + tile, ff_chunk), F32),
            pltpu.VMEM((n_chunks, hist, ff_chunk), F32),
        ],
        compiler_params=pltpu.CompilerParams(
            dimension_semantics=("arbitrary", "arbitrary"),
            vmem_limit_bytes=VMEM_LIMIT_BYTES),
        name="conv_ffn",
    )(*operands)


def _kvq_kernel(*refs, n_heads, scale, with_aug, tile):
    (x_ref, gkv_ref, gb_ref, wk_ref, wv_ref, wf_ref, bf_ref, wq_ref) = refs[:8]
    if with_aug:
        tri_ref, pq_ref, pk_ref, oneq_ref, onek_ref = refs[8:13]
        kt_ref, vt_ref, lft_ref, q2t_ref, k2_ref, vtb_ref, f_carry = refs[13:]
    else:
        k_ref, v_ref, lf_ref, q_ref = refs[8:]
    n = _rms_normalize(x_ref[...])
    hk = (n * gkv_ref[...]).astype(BF16)
    hq = (n * gb_ref[...]).astype(BF16)
    k = _dot(hk, wk_ref[...])
    v = _dot(hk, wv_ref[...])
    z = _dot(hk, wf_ref[...]) + bf_ref[...]
    lane = lax.broadcasted_iota(jnp.int32, z.shape, 1)
    lf = jnp.where(lane < n_heads, jax.nn.log_sigmoid(z), 0.0)
    q = _dot(hq, wq_ref[...]) * scale
    if not with_aug:
        k_ref[...] = k
        v_ref[...] = v
        lf_ref[...] = lf[:, :n_heads]
        q_ref[...] = q
        return

    kt_ref[0] = k.T
    vt = v.T
    vt_ref[0] = vt
    vtb_ref[0, 0] = vt.astype(BF16)
    lft_ref[0] = lf.T[:n_heads, :]

    @pl.when(pl.program_id(1) == 0)
    def _():
        f_carry[...] = jnp.zeros_like(f_carry)

    tri = tri_ref[...]
    hi, mid, lo = _split3_bf16(lf)
    f_cum = _dot(tri, hi) + _dot(tri, mid) + _dot(tri, lo) + f_carry[...]
    f_carry[...] = f_cum[tile - 1:tile, :]

    f_cat = jnp.concatenate(_split3_bf16(f_cum * LOG2_E), axis=1)
    aug_qt = (_dot(f_cat, pq_ref[...]) + oneq_ref[...]).T.astype(BF16)
    aug_k = (_dot(f_cat, pk_ref[...]) + onek_ref[...]).astype(BF16)
    qt = q.T.astype(BF16)
    kb = k.astype(BF16)
    for hp in range(n_heads // 2):
        src = slice(hp * LANES, (hp + 1) * LANES)
        q2t_ref[0, 0, 2 * hp * LANES:(2 * hp + 1) * LANES, :] = qt[src, :]
        q2t_ref[0, 0, (2 * hp + 1) * LANES:(2 * hp + 2) * LANES, :] = aug_qt[src, :]
        k2_ref[:, 2 * hp * LANES:(2 * hp + 1) * LANES] = kb[:, src]
        k2_ref[:, (2 * hp + 1) * LANES:(2 * hp + 2) * LANES] = aug_k[:, src]


def _aug_constants(n_heads, tile):
    n_pairs = n_heads // 2
    pq = np.zeros((3 * LANES, n_pairs * LANES), np.float32)
    pk = np.zeros((3 * LANES, n_pairs * LANES), np.float32)
    oneq = np.zeros((1, n_pairs * LANES), np.float32)
    onek = np.zeros((1, n_pairs * LANES), np.float32)
    for h in range(n_heads):
        base = (h // 2) * LANES + (h % 2) * AUG_LANES_PER_HEAD
        for piece in range(3):
            pq[piece * LANES + h, base + piece] = 1.0
            onek[0, base + piece] = 1.0
            pk[piece * LANES + h, base + 3 + piece] = -1.0
            oneq[0, base + 3 + piece] = 1.0
    tri = np.tril(np.ones((tile, tile), np.float32))
    return (jnp.asarray(tri, BF16), jnp.asarray(pq, BF16), jnp.asarray(pk, BF16),
            jnp.asarray(oneq), jnp.asarray(onek))


def _kvq_layer(x2d, gkv, gb, wk, wv, wf, bf, wq, *, n_heads, groups, tile, with_aug):
    rows, d_model = x2d.shape
    hd = wk.shape[1]
    n_tiles = rows // groups // tile
    scale = float(hd // n_heads) ** -0.5
    const2 = lambda b, t: (0, 0)
    row_map = lambda b, t: (b * n_tiles + t, 0)
    operands = [x2d, gkv, gb, wk, wv, wf, bf, wq]
    in_specs = [pl.BlockSpec((tile, d_model), row_map)]
    in_specs += [pl.BlockSpec(a.shape, const2) for a in operands[1:]]
    scratch = []
    if with_aug:
        scale *= LOG2_E
        seq = rows // groups
        col_map = lambda b, t: (b, 0, t)
        consts = _aug_constants(n_heads, tile)
        operands += list(consts)
        in_specs += [pl.BlockSpec(a.shape, const2) for a in consts]
        tile_map = lambda b, t: (b, t, 0, 0)
        out_shape = [jax.ShapeDtypeStruct((groups, hd, seq), F32),
                     jax.ShapeDtypeStruct((groups, hd, seq), F32),
                     jax.ShapeDtypeStruct((groups, n_heads, seq), F32),
                     jax.ShapeDtypeStruct((groups, n_tiles, 2 * hd, tile), BF16),
                     jax.ShapeDtypeStruct((rows, 2 * hd), BF16),
                     jax.ShapeDtypeStruct((groups, n_tiles, hd, tile), BF16)]
        out_specs = [pl.BlockSpec((1, hd, tile), col_map), pl.BlockSpec((1, hd, tile), col_map),
                     pl.BlockSpec((1, n_heads, tile), col_map),
                     pl.BlockSpec((1, 1, 2 * hd, tile), tile_map),
                     pl.BlockSpec((tile, 2 * hd), row_map),
                     pl.BlockSpec((1, 1, hd, tile), tile_map)]
        scratch = [pltpu.VMEM((1, LANES), F32)]
    else:
        out_shape = [jax.ShapeDtypeStruct((rows, hd), F32), jax.ShapeDtypeStruct((rows, hd), F32),
                     jax.ShapeDtypeStruct((rows, n_heads), F32),
                     jax.ShapeDtypeStruct((rows, hd), F32)]
        out_specs = [pl.BlockSpec((tile, hd), row_map), pl.BlockSpec((tile, hd), row_map),
                     pl.BlockSpec((tile, n_heads), row_map), pl.BlockSpec((tile, hd), row_map)]
    kern = functools.partial(_kvq_kernel, n_heads=n_heads, scale=scale, with_aug=with_aug,
                             tile=tile)
    return pl.pallas_call(
        kern,
        grid=(groups, n_tiles),
        in_specs=in_specs,
        out_specs=out_specs,
        out_shape=out_shape,
        scratch_shapes=scratch,
        compiler_params=pltpu.CompilerParams(
            dimension_semantics=("arbitrary", "arbitrary"),
            vmem_limit_bytes=VMEM_LIMIT_BYTES),
        name="kv_q_proj",
    )(*operands)


def _prompt_attn_kernel(q2t_ref, k2_ref, vt_ref, o_ref, qh_buf, s_buf, m_buf, acc_buf,
                        *, tq, tk, head_dim):
    qi = pl.program_id(2)

    q2t = q2t_ref[0, 0]
    chan = lax.broadcasted_iota(jnp.int32, q2t.shape, 0)
    aug = chan - LANES
    for j in range(2):
        own = ((chan >= j * head_dim) & (chan < (j + 1) * head_dim)) | (
            (aug >= j * AUG_LANES_PER_HEAD) & (aug < (j + 1) * AUG_LANES_PER_HEAD))
        qh_buf[j] = jnp.where(own, q2t, jnp.zeros_like(q2t))
    m_buf[...] = jnp.full_like(m_buf, MASK_VALUE)
    acc_buf[...] = jnp.zeros_like(acc_buf)

    def score(j, ki):
        k2 = k2_ref[0, pl.ds(pl.multiple_of(ki * tk, tk), tk), :]
        s_buf[j] = _dot(k2, qh_buf[j])

    def absorb(j, ki, masked):
        s = s_buf[j]
        if masked:
            causal = (lax.broadcasted_iota(jnp.int32, s.shape, 0)
                      <= lax.broadcasted_iota(jnp.int32, s.shape, 1))
            s = jnp.where(causal, s, MASK_VALUE)
        m_prev = m_buf[j]
        m_next = jnp.maximum(m_prev, jnp.max(s, axis=0, keepdims=True))
        alpha = jnp.exp2(m_prev - m_next)
        p = jnp.exp2(s - m_next)
        m_buf[j] = m_next
        vt = vt_ref[0, ki, j * head_dim:(j + 1) * head_dim, :]
        vt1 = jnp.concatenate([vt, jnp.ones((SUM_ROWS, tk), BF16)], axis=0)
        acc_buf[j] = alpha * acc_buf[j] + _dot(vt1, p.astype(BF16))

    score(0, 0)

    def full_step(ki):
        score(1, ki)
        absorb(0, ki, masked=False)
        score(0, ki + 1)
        absorb(1, ki, masked=False)

    def two_steps(kp, carry):
        full_step(2 * kp)
        full_step(2 * kp + 1)
        return carry

    lax.fori_loop(0, qi // 2, two_steps, 0)

    @pl.when(qi % 2 == 1)
    def _():
        full_step(qi - 1)

    score(1, qi)
    absorb(0, qi, masked=True)
    absorb(1, qi, masked=True)

    ot = jnp.concatenate(
        [acc_buf[j, 0:head_dim, :] / acc_buf[j, head_dim:head_dim + 1, :] for j in range(2)],
        axis=0)
    o_ref[...] = ot.T.astype(o_ref.dtype)


def _prompt_attention(q2t, k2, vtb, *, n_heads, tq, tk):
    batch, n_tiles, hd, tile = vtb.shape
    assert tile == tq == tk, "attention tiles follow the projection kernel's row tile"
    seq = n_tiles * tile
    rows = batch * seq
    head_dim = hd // n_heads
    n_pairs = n_heads // 2
    kern = functools.partial(_prompt_attn_kernel, tq=tq, tk=tk, head_dim=head_dim)
    return pl.pallas_call(
        kern,
        grid=(batch, n_pairs, n_tiles),
        in_specs=[
            pl.BlockSpec((1, 1, 2 * LANES, tq), lambda b, hp, qi: (b, qi, hp, 0)),
            pl.BlockSpec((1, seq, 2 * LANES), lambda b, hp, qi: (b, 0, hp)),
            pl.BlockSpec((1, n_tiles, LANES, tk), lambda b, hp, qi: (b, 0, hp, 0)),
        ],
        out_specs=pl.BlockSpec((tq, LANES), lambda b, hp, qi: (b * n_tiles + qi, hp)),
        out_shape=jax.ShapeDtypeStruct((rows, hd), BF16),
        scratch_shapes=[
            pltpu.VMEM((2, 2 * LANES, tq), BF16),
            pltpu.VMEM((2, tk, tq), F32),
            pltpu.VMEM((2, 1, tq), F32),
            pltpu.VMEM((2, head_dim + SUM_ROWS, tq), F32),
        ],
        compiler_params=pltpu.CompilerParams(
            dimension_semantics=("arbitrary", "arbitrary", "arbitrary"),
            vmem_limit_bytes=VMEM_LIMIT_BYTES),
        name="prompt_attention",
    )(q2t, k2.reshape(batch, seq, 2 * hd), vtb)


def _sample_attn_kernel(pt_ref, *refs, n_heads, head_dim, dec_seq, pages_per_step, page_size,
                        n_chains):
    del pt_ref
    n_in = 3 * pages_per_step
    k_refs = refs[0:pages_per_step]
    v_refs = refs[pages_per_step:2 * pages_per_step]
    lf_refs = refs[2 * pages_per_step:n_in]
    q_ref, kn_ref, vn_ref, lfn_ref, sl_ref = refs[n_in:n_in + 5]
    o_ref = refs[n_in + 5]
    qrows_buf, cn_col, carry, m_buf, l_buf, acc_buf = refs[n_in + 6:]
    g = pl.program_id(1)
    n_rows = dec_seq * n_heads
    hd = n_heads * head_dim

    def head_mask(shape):
        r = lax.broadcasted_iota(jnp.int32, shape, 0) % n_heads
        lane = lax.broadcasted_iota(jnp.int32, shape, 1)
        return (lane >= r * head_dim) & (lane < (r + 1) * head_dim)

    def column_of(rowvec):
        wide = jnp.broadcast_to(rowvec, (n_rows, rowvec.shape[1]))
        r = lax.broadcasted_iota(jnp.int32, wide.shape, 0) % n_heads
        lane = lax.broadcasted_iota(jnp.int32, wide.shape, 1)
        return jnp.sum(jnp.where(lane == r, wide, 0.0), axis=1, keepdims=True)

    def online_update(c, s, weighted_values):
        m_prev = m_buf[c]
        m_next = jnp.maximum(m_prev, jnp.max(s, axis=1, keepdims=True))
        alpha = jnp.exp(m_prev - m_next)
        p = jnp.exp(s - m_next[:, 0:1])
        l_buf[c] = alpha * l_buf[c] + jnp.sum(p, axis=1, keepdims=True)
        acc_buf[c] = acc_buf[c] * alpha[:, 0:1] + weighted_values(p.astype(BF16))
        m_buf[c] = m_next

    @pl.when(g == 0)
    def _():
        q = q_ref[0]
        mask = head_mask((n_rows, hd))
        qrep = jnp.concatenate(
            [jnp.broadcast_to(q[t:t + 1, :], (n_heads, hd)) for t in range(dec_seq)], axis=0)
        qrows = jnp.where(mask, qrep, 0.0).astype(BF16)
        qrows_buf[...] = qrows
        m_buf[...] = jnp.full_like(m_buf, MASK_VALUE)
        l_buf[...] = jnp.zeros_like(l_buf)
        acc_buf[...] = jnp.zeros_like(acc_buf)
        carry[...] = jnp.zeros_like(carry)

        lfn = lfn_ref[0]
        cums = []
        run = None
        for t in range(dec_seq):
            run = lfn[t:t + 1, :] if run is None else run + lfn[t:t + 1, :]
            cums.append(run)
        r_t = lax.broadcasted_iota(jnp.int32, (n_rows, 1), 0) // n_heads
        cn = jnp.zeros((n_rows, 1), F32)
        for t in range(dec_seq):
            cn = jnp.where(r_t == t, column_of(cums[t]), cn)
        cn_col[...] = cn
        qrows_f = qrows.astype(F32)
        kn = kn_ref[0].astype(BF16).astype(F32)
        vn = vn_ref[0].astype(BF16).astype(F32)
        lane = lax.broadcasted_iota(jnp.int32, (n_rows, LANES), 1)
        s_new = jnp.full((n_rows, LANES), MASK_VALUE, F32)
        for s in range(dec_seq):
            qk = jnp.sum(qrows_f * kn[s:s + 1, :], axis=1, keepdims=True)
            s_new = jnp.where((lane == s) & (r_t >= s), qk + cn - column_of(cums[s]), s_new)

        def new_values(p):
            p = p.astype(F32)
            return sum(p[:, s:s + 1] * vn[s:s + 1, :] for s in range(dec_seq))

        online_update(0, s_new, new_values)

    sl = sl_ref[...]
    biases = [None] * pages_per_step
    run = carry[...]
    for j in reversed(range(pages_per_step)):
        lft = lf_refs[j][0]
        hi, mid, lo = _split3_bf16(lft)
        suffix = _dot(hi, sl) + _dot(mid, sl) + _dot(lo, sl) + run
        biases[j] = jnp.concatenate([suffix] * dec_seq, axis=0)
        run = run + jnp.sum(lft, axis=1, keepdims=True)
    carry[...] = run
    cn = cn_col[...]
    qrows = qrows_buf[...]
    per_chain = pages_per_step // n_chains
    for c in range(n_chains):
        pages = range(c * per_chain, (c + 1) * per_chain)
        bias = jnp.concatenate([biases[j] for j in pages], axis=1) + cn
        kt = jnp.concatenate([k_refs[j][0].astype(BF16) for j in pages], axis=1)
        vt = jnp.concatenate([v_refs[j][0].astype(BF16) for j in pages], axis=1)
        online_update(c, _dot(qrows, kt) + bias, lambda p, vt=vt: _dot_nt(p, vt))

    @pl.when(g == pl.num_programs(1) - 1)
    def _():
        m_all = m_buf[0]
        for c in range(1, n_chains):
            m_all = jnp.maximum(m_all, m_buf[c])
        l_all = jnp.zeros_like(m_all)
        acc_all = jnp.zeros((n_rows, hd), F32)
        for c in range(n_chains):
            w = jnp.exp(m_buf[c] - m_all)
            l_all = l_all + w * l_buf[c]
            acc_all = acc_all + w[:, 0:1] * acc_buf[c]
        out = jnp.where(head_mask((n_rows, hd)), acc_all / l_all[:, 0:1], 0.0)
        o_ref[0] = jnp.sum(out.reshape(dec_seq, n_heads, hd), axis=1).astype(o_ref.dtype)


def _sample_attention(page_table, cache_kt, cache_vt, cache_lft, q, k_new, v_new, lf_new,
                      *, n_heads, pages_per_step):
    dec_batch, dec_seq, hd = q.shape
    n_pages = page_table.shape[1]
    page_size = cache_kt.shape[2]
    head_dim = hd // n_heads
    n_groups = n_pages // pages_per_step
    n_rows = dec_seq * n_heads
    sl = jnp.asarray(np.tril(np.ones((page_size, page_size), np.float32), -1), BF16)

    def page_map(j):
        def index_map(b, g, pt):
            return (pt[b, (n_groups - 1 - g) * pages_per_step + j], 0, 0)
        return index_map

    seq_map = lambda b, g, pt: (b, 0, 0)
    in_specs = (
        [pl.BlockSpec((1, hd, page_size), page_map(j)) for j in range(pages_per_step)]
        + [pl.BlockSpec((1, hd, page_size), page_map(j)) for j in range(pages_per_step)]
        + [pl.BlockSpec((1, n_heads, page_size), page_map(j)) for j in range(pages_per_step)]
        + [pl.BlockSpec((1, dec_seq, hd), seq_map)] * 3
        + [pl.BlockSpec((1, dec_seq, n_heads), seq_map),
           pl.BlockSpec((page_size, page_size), lambda b, g, pt: (0, 0))])
    n_chains = math.gcd(pages_per_step, SAMPLE_SOFTMAX_CHAINS)
    kern = functools.partial(_sample_attn_kernel, n_heads=n_heads, head_dim=head_dim,
                             dec_seq=dec_seq, pages_per_step=pages_per_step, page_size=page_size,
                             n_chains=n_chains)
    grid_spec = pltpu.PrefetchScalarGridSpec(
        num_scalar_prefetch=1,
        grid=(dec_batch, n_groups),
        in_specs=in_specs,
        out_specs=pl.BlockSpec((1, dec_seq, hd), seq_map),
        scratch_shapes=[
            pltpu.VMEM((n_rows, hd), BF16),
            pltpu.VMEM((n_rows, 1), F32),
            pltpu.VMEM((n_heads, LANES), F32),
            pltpu.VMEM((n_chains, n_rows, LANES), F32),
            pltpu.VMEM((n_chains, n_rows, LANES), F32),
            pltpu.VMEM((n_chains, n_rows, hd), F32),
        ])
    operands = ([cache_kt] * pages_per_step + [cache_vt] * pages_per_step
                + [cache_lft] * pages_per_step + [q, k_new, v_new, lf_new, sl])
    return pl.pallas_call(
        kern,
        grid_spec=grid_spec,
        out_shape=jax.ShapeDtypeStruct((dec_batch, dec_seq, hd), BF16),
        compiler_params=pltpu.CompilerParams(
            dimension_semantics=("arbitrary", "arbitrary"),
            vmem_limit_bytes=VMEM_LIMIT_BYTES),
        name="sample_attention",
    )(page_table, *operands)


def _pad_history(state, hist):
    keep = state.shape[1]
    if keep == hist:
        return state
    return jnp.pad(state, ((0, 0), (hist - keep, 0), (0, 0)))


def _to_time_major(a):
    b, t = a.shape[:2]
    return jnp.swapaxes(a, 0, 1).reshape((1, t * b) + a.shape[2:])


def _from_time_major(a, b):
    t = a.shape[1] // b
    return jnp.swapaxes(a.reshape((t, b) + a.shape[2:]), 0, 1)


def kernel(x_prompt, x_sample, cache_k, cache_v, cache_logf, state_lru_h, state_lru_conv,
           state_ffn_conv, page_table, norm_a_g, w_a_in, conv_a_w, conv_a_b, w_a_gate, b_a_gate,
           lru_lambda, w_a_out, norm_f_g, w_f_in, conv_f_w, conv_f_b, w_f_out, norm_kv_g, w_kv,
           b_forget, norm_b_g, w_q, w_o, norm_out_g):
    batch, seq, d_model = x_prompt.shape
    dec_batch, dec_seq, _ = x_sample.shape
    n_phys, page_size, n_heads, head_dim = cache_k.shape
    hd = n_heads * head_dim
    n_a = w_a_in.shape[0]
    n_b = w_q.shape[0]
    d_rnn = w_a_out.shape[1]
    d_ff = w_f_out.shape[1]
    assert 2 * head_dim == LANES and n_heads % 2 == 0 and n_heads <= LANES
    assert dec_batch % SUBLANES == 0 and dec_seq >= CONV_A_WIDTH - 1

    row = lambda a: a.reshape(1, -1).astype(F32)
    w_a_in_b = w_a_in.astype(BF16)
    w_a_gate_b = w_a_gate.astype(BF16)
    w_a_out_b = w_a_out.astype(BF16)
    w_f_in_b = w_f_in.astype(BF16)
    w_f_out_b = w_f_out.astype(BF16)
    w_q_b = w_q.astype(BF16)
    w_o_b = w_o.astype(BF16)
    wk = w_kv[:, :hd].astype(BF16)
    wv = w_kv[:, hd:2 * hd].astype(BF16)
    wf = jnp.pad(w_kv[:, 2 * hd:], ((0, 0), (0, LANES - n_heads))).astype(BF16)
    bf = jnp.pad(b_forget.astype(F32), (0, LANES - n_heads)).reshape(1, LANES)

    def trunk(x2d, lru_h0, lru_conv0, ffn_conv0, *, groups, stride, tiles, with_aug,
              attention_fn):
        lru_h_new, lru_conv_new, ffn_conv_new = [], [], []
        layer = 0
        hist_a = _round_up((CONV_A_WIDTH - 1) * stride, SUBLANES)
        hist_f = _round_up((CONV_F_WIDTH - 1) * stride, SUBLANES)

        def ffn(x2d, layer, **kw):
            return _ffn_layer(x2d, _pad_history(ffn_conv0[layer], hist_f), row(norm_f_g[layer]),
                              w_f_in_b[layer], conv_f_w[layer], row(conv_f_b[layer]),
                              w_f_out_b[layer], groups=groups, stride=stride, tile=tiles["ffn"],
                              **kw)

        for i in range(n_a):
            x2d, c_new, h_new = _rglru_layer(
                x2d, _pad_history(lru_conv0[i], hist_a), lru_h0[i], row(norm_a_g[i]), w_a_in_b[i],
                conv_a_w[i], row(conv_a_b[i]), w_a_gate_b[i], b_a_gate[i][:, None, :],
                row(lru_lambda[i]), w_a_out_b[i], groups=groups, stride=stride,
                tile=tiles["rglru"])
            x2d, f_new = ffn(x2d, layer)
            lru_h_new.append(h_new)
            lru_conv_new.append(c_new)
            ffn_conv_new.append(f_new)
            layer += 1
        kvq = _kvq_layer(x2d, row(norm_kv_g), row(norm_b_g[0]), wk, wv, wf, bf, w_q_b[0],
                         n_heads=n_heads, groups=groups, tile=tiles["kvq"],
                         with_aug=with_aug)
        k_new, v_new, lf_new = kvq[:3]
        for j in range(n_b):
            assert j == 0, "one attention layer per shared K/V projection is supported"
            o2d = attention_fn(kvq)
            x2d, f_new = ffn(x2d, layer, pre=(o2d, w_o_b[j]),
                             final_gain=row(norm_out_g) if j == n_b - 1 else None)
            ffn_conv_new.append(f_new)
            layer += 1
        return x2d, k_new, v_new, lf_new, lru_h_new, lru_conv_new, ffn_conv_new

    zeros = lambda *s: jnp.zeros(s, F32)
    tiles = {"rglru": _pick_tile(seq, 256), "ffn": _pick_tile(seq, 512),
             "kvq": _pick_tile(seq, 512)}

    def prompt_attention_fn(kvq):
        q2t, k2, vtb = kvq[3:]
        return _prompt_attention(q2t, k2, vtb, n_heads=n_heads, tq=tiles["kvq"], tk=tiles["kvq"])

    (y, k_p, v_p, lf_p, h_p, c_p, f_p) = trunk(
        x_prompt.reshape(batch * seq, d_model),
        zeros(n_a, batch, 1, d_rnn), zeros(n_a, batch, CONV_A_WIDTH - 1, d_rnn),
        zeros(n_a + n_b, batch, CONV_F_WIDTH - 1, d_ff),
        groups=batch, stride=1, tiles=tiles, with_aug=True, attention_fn=prompt_attention_fn)
    y_prompt = y.reshape(batch, seq, d_model)
    k_prompt = k_p.reshape(batch, n_heads, head_dim, seq).transpose(0, 3, 1, 2)
    v_prompt = v_p.reshape(batch, n_heads, head_dim, seq).transpose(0, 3, 1, 2)
    logf_prompt = lf_p.transpose(0, 2, 1)
    lru_h_prompt = jnp.stack([h.reshape(batch, d_rnn) for h in h_p], axis=0)
    lru_conv_prompt = jnp.stack(c_p, axis=0)
    ffn_conv_prompt = jnp.stack(f_p, axis=0)

    cache_kt = cache_k.transpose(0, 2, 3, 1).reshape(n_phys, hd, page_size)
    cache_vt = cache_v.transpose(0, 2, 3, 1).reshape(n_phys, hd, page_size)
    cache_lft = jnp.swapaxes(cache_logf, 1, 2)
    n_pages = page_table.shape[1]

    def sample_attention_fn(kvq):
        k_t, v_t, lf_t, q_t = kvq
        bm = lambda a: _from_time_major(a[None], dec_batch)
        o = _sample_attention(page_table, cache_kt, cache_vt, cache_lft, bm(q_t), bm(k_t),
                              bm(v_t), bm(lf_t), n_heads=n_heads,
                              pages_per_step=_pick_tile(n_pages, SAMPLE_PAGES_PER_STEP))
        return _to_time_major(o)[0]

    n_rows = dec_batch * dec_seq
    tiles = {"rglru": n_rows, "ffn": n_rows, "kvq": n_rows}
    (y, k_s, v_s, lf_s, h_s, c_s, f_s) = trunk(
        _to_time_major(x_sample)[0],
        state_lru_h[:, None], jnp.stack([_to_time_major(s) for s in state_lru_conv]),
        jnp.stack([_to_time_major(s) for s in state_ffn_conv]),
        groups=1, stride=dec_batch, tiles=tiles, with_aug=False,
        attention_fn=sample_attention_fn)
    bm = lambda a: _from_time_major(a[None], dec_batch)
    y_sample = bm(y)
    k_sample = bm(k_s).reshape(dec_batch, dec_seq, n_heads, head_dim)
    v_sample = bm(v_s).reshape(dec_batch, dec_seq, n_heads, head_dim)
    logf_sample = bm(lf_s)
    lru_h_sample = jnp.stack([h[0] for h in h_s], axis=0)
    lru_conv_sample = jnp.stack([_from_time_major(c, dec_batch) for c in c_s], axis=0)
    ffn_conv_sample = jnp.stack([_from_time_major(f, dec_batch) for f in f_s], axis=0)

    return (y_prompt, y_sample, k_prompt, v_prompt, logf_prompt, lru_h_prompt, lru_conv_prompt,
            ffn_conv_prompt, k_sample, v_sample, logf_sample, lru_h_sample, lru_conv_sample,
            ffn_conv_sample)
```

```python
import functools
import math

import numpy as np
import jax
import jax.numpy as jnp
from jax import lax
from jax.experimental import pallas as pl
from jax.experimental.pallas import tpu as pltpu

F32 = jnp.float32
BF16 = jnp.bfloat16

LANES = 128
SUBLANES = 8
VMEM_LIMIT_BYTES = 56 * 1024 * 1024

RMS_EPS = 1e-6
LRU_C = 8.0
CONV_A_WIDTH = 4
CONV_F_WIDTH = 3
MASK_VALUE = -1e30
LOG2_E = 1.4426950408889634

AUG_LANES_PER_HEAD = 8
SUM_ROWS = 16
SAMPLE_PAGES_PER_STEP = 16
SAMPLE_SOFTMAX_CHAINS = 4


def _round_up(x, m):
    return (x + m - 1) // m * m


def _pick_tile(n, target):
    t = min(n, target)
    while n % t:
        t //= 2
    return t


def _rms_normalize(x):
    return x * lax.rsqrt(jnp.mean(x * x, axis=-1, keepdims=True) + RMS_EPS)


def _split3_bf16(x):
    hi = x.astype(BF16)
    r1 = x - hi.astype(F32)
    mid = r1.astype(BF16)
    lo = (r1 - mid.astype(F32)).astype(BF16)
    return hi, mid, lo


def _dot(a, b):
    return jnp.dot(a, b, preferred_element_type=F32)


def _dot_nt(a, b):
    return lax.dot_general(a, b, (((1,), (1,)), ((), ())), preferred_element_type=F32)


def _shifted_conv(buf_ref, w, b, *, width, hist, tile, stride):
    acc = None
    for j in range(width):
        back = (width - 1 - j) * stride
        term = buf_ref[hist - back:hist - back + tile, :] * w[j:j + 1, :]
        acc = term if acc is None else acc + term
    return acc + b


def _rglru_kernel(x_ref, conv0_ref, h0_ref, g_ref, win_ref, cw_ref, cb_ref, wg_ref, bg_ref,
                  lam_ref, wout_ref, perm_ref, y_ref, convn_ref, hn_ref, rec_buf, a_buf, b_buf,
                  h_carry, tail_buf, *, stride, tile):
    t = pl.program_id(1)
    d_rnn = rec_buf.shape[1]
    n_blocks, lru_block, _ = wg_ref.shape
    interleaved = stride == 1
    step_rows = SUBLANES if interleaved else stride
    hist = (CONV_A_WIDTH - 1) * step_rows
    seg = tile // SUBLANES
    sub = lax.broadcasted_iota(jnp.int32, (SUBLANES, d_rnn), 0)

    @pl.when(t == 0)
    def _():
        h_carry[...] = h0_ref[0]
        if interleaved:
            tail_buf[...] = conv0_ref[0]
        else:
            rec_buf[0:hist, :] = conv0_ref[0]

    x = x_ref[...]
    hn = (_rms_normalize(x) * g_ref[...]).astype(BF16)
    if interleaved:
        hn = _dot(perm_ref[0], hn).astype(BF16)
    else:
        @pl.when(t != 0)
        def _():
            rec_buf[0:hist, :] = rec_buf[tile:tile + hist, :]

    proj = _dot(hn, win_ref[...])
    rec_buf[hist:hist + tile, :] = proj[:, d_rnn:]
    if interleaved:
        for m in range(1, CONV_A_WIDTH):
            src = hist + (seg - m) * SUBLANES
            block = pltpu.roll(rec_buf[src:src + SUBLANES, :], 1, 0)
            prev = tail_buf[SUBLANES - m:SUBLANES - m + 1, :]
            dst = (CONV_A_WIDTH - 1 - m) * SUBLANES
            rec_buf[dst:dst + SUBLANES, :] = jnp.where(sub == 0, prev, block)
        for m in range(1, CONV_A_WIDTH):
            last = hist + (seg - m) * SUBLANES + SUBLANES - 1
            tail_buf[SUBLANES - m:SUBLANES - m + 1, :] = rec_buf[last:last + 1, :]
    xc = _shifted_conv(rec_buf, cw_ref[...], cb_ref[...], width=CONV_A_WIDTH, hist=hist,
                       tile=tile, stride=step_rows)
    xcb = xc.astype(BF16)
    log_sig_lam = jax.nn.log_sigmoid(lam_ref[...])
    for n in range(n_blocks):
        cols = slice(n * lru_block, (n + 1) * lru_block)
        gates = jax.nn.sigmoid(_dot(xcb[:, cols], wg_ref[n]) + bg_ref[n])
        r = gates[:, :lru_block]
        ig = gates[:, lru_block:]
        log_a = LRU_C * r * log_sig_lam[:, cols]
        a = jnp.exp(log_a)
        a_buf[:, cols] = a
        one_minus_a2 = -jnp.tanh(log_a) * (a * a + 1.0)
        b_buf[:, cols] = jnp.sqrt(one_minus_a2) * ig * xc[:, cols]

    if interleaved:
        h_loc = jnp.zeros((SUBLANES, d_rnn), F32)
        decay = jnp.ones((SUBLANES, d_rnn), F32)
        for j in range(seg):
            rows = slice(j * SUBLANES, (j + 1) * SUBLANES)
            a_j = a_buf[rows, :]
            h_loc = a_j * h_loc + b_buf[rows, :]
            decay = a_j * decay
            b_buf[rows, :] = h_loc
            a_buf[rows, :] = decay
        carry = h_carry[...]
        carry_in = jnp.zeros((SUBLANES, d_rnn), F32)
        for s in range(SUBLANES):
            carry_in = jnp.where(sub == s, carry, carry_in)
            carry = h_loc[s:s + 1, :] + decay[s:s + 1, :] * carry
        h_last = carry
        hs = (b_buf[...].reshape(seg, SUBLANES, d_rnn)
              + a_buf[...].reshape(seg, SUBLANES, d_rnn) * carry_in[None]).reshape(tile, d_rnn)
        convn_ref[0] = tail_buf[SUBLANES - (CONV_A_WIDTH - 1):SUBLANES, :]
    else:
        h_last = h_carry[...]
        for k in range(tile // stride):
            rows = slice(k * stride, (k + 1) * stride)
            h_last = a_buf[rows, :] * h_last + b_buf[rows, :]
            b_buf[rows, :] = h_last
        hs = b_buf[...]
        convn_ref[0] = rec_buf[tile:tile + hist, :]
    h_carry[...] = h_last
    hn_ref[0] = h_last

    gate_branch = proj[:, :d_rnn]
    yv = (hs * jax.nn.gelu(gate_branch, approximate=True)).astype(BF16)
    if interleaved:
        yv = _dot(perm_ref[1], yv).astype(BF16)
    y_ref[...] = x + _dot(yv, wout_ref[...])


def _rglru_layer(x2d, conv0, h0, g, w_in, conv_w, conv_b, w_gate, b_gate, lam, w_out,
                 *, groups, stride, tile):
    rows, d_model = x2d.shape
    d_rnn = w_out.shape[0]
    n_tiles = rows // groups // tile
    hist = conv0.shape[1]
    keep = (CONV_A_WIDTH - 1) * stride
    step_rows = SUBLANES if stride == 1 else stride
    const2 = lambda b, t: (0, 0)
    const3 = lambda b, t: (0, 0, 0)
    seg = tile // SUBLANES
    order = np.arange(tile).reshape(SUBLANES, seg).T.reshape(-1)
    gather = np.zeros((tile, tile), np.float32)
    gather[np.arange(tile), order] = 1.0
    perm = jnp.asarray(np.stack([gather, gather.T]), BF16)
    kern = functools.partial(_rglru_kernel, stride=stride, tile=tile)
    return pl.pallas_call(
        kern,
        grid=(groups, n_tiles),
        in_specs=[
            pl.BlockSpec((tile, d_model), lambda b, t: (b * n_tiles + t, 0)),
            pl.BlockSpec((1, hist, d_rnn), lambda b, t: (b, 0, 0)),
            pl.BlockSpec((1, stride, d_rnn), lambda b, t: (b, 0, 0)),
            pl.BlockSpec((1, d_model), const2),
            pl.BlockSpec(w_in.shape, const2),
            pl.BlockSpec(conv_w.shape, const2),
            pl.BlockSpec((1, d_rnn), const2),
            pl.BlockSpec(w_gate.shape, const3),
            pl.BlockSpec(b_gate.shape, const3),
            pl.BlockSpec((1, d_rnn), const2),
            pl.BlockSpec(w_out.shape, const2),
            pl.BlockSpec(perm.shape, const3),
        ],
        out_specs=[
            pl.BlockSpec((tile, d_model), lambda b, t: (b * n_tiles + t, 0)),
            pl.BlockSpec((1, keep, d_rnn), lambda b, t: (b, 0, 0)),
            pl.BlockSpec((1, stride, d_rnn), lambda b, t: (b, 0, 0)),
        ],
        out_shape=[
            jax.ShapeDtypeStruct((rows, d_model), F32),
            jax.ShapeDtypeStruct((groups, keep, d_rnn), F32),
            jax.ShapeDtypeStruct((groups, stride, d_rnn), F32),
        ],
        scratch_shapes=[
            pltpu.VMEM(((CONV_A_WIDTH - 1) * step_rows + tile, d_rnn), F32),
            pltpu.VMEM((tile, d_rnn), F32),
            pltpu.VMEM((tile, d_rnn), F32),
            pltpu.VMEM((stride, d_rnn), F32),
            pltpu.VMEM((SUBLANES, d_rnn), F32),
        ],
        compiler_params=pltpu.CompilerParams(
            dimension_semantics=("arbitrary", "arbitrary"),
            vmem_limit_bytes=VMEM_LIMIT_BYTES),
        name="rglru_layer",
    )(x2d, conv0, h0, g, w_in, conv_w, conv_b, w_gate, b_gate, lam, w_out, perm)


def _ffn_kernel(*refs, has_pre, has_final_norm, stride, hist, tile, ff_chunk):
    refs = list(refs)
    x_ref = refs.pop(0)
    if has_pre:
        o_ref = refs.pop(0)
        wo_ref = refs.pop(0)
    gf_ref, win_ref, cw_ref, cb_ref, wout_ref, g0_ref = refs[:6]
    refs = refs[6:]
    if has_final_norm:
        gout_ref = refs.pop(0)
    y_ref, convn_ref, gate_buf, hist_buf = refs
    t = pl.program_id(1)
    d_ff = wout_ref.shape[0]
    keep = (CONV_F_WIDTH - 1) * stride

    x1 = x_ref[...]
    if has_pre:
        x1 = x1 + _dot(o_ref[...], wo_ref[...])
    y_ref[...] = x1
    h = (_rms_normalize(x1) * gf_ref[...]).astype(BF16)

    for c in range(d_ff // ff_chunk):
        cols = slice(c * ff_chunk, (c + 1) * ff_chunk)
        up_cols = slice(d_ff + c * ff_chunk, d_ff + (c + 1) * ff_chunk)

        @pl.when(t == 0)
        def _():
            gate_buf[0:hist, :] = g0_ref[0, :, cols]

        @pl.when(t != 0)
        def _():
            gate_buf[0:hist, :] = hist_buf[c]

        gate_buf[hist:hist + tile, :] = _dot(h, win_ref[:, cols])
        up = _dot(h, win_ref[:, up_cols])
        gate = _shifted_conv(gate_buf, cw_ref[:, cols], cb_ref[:, cols], width=CONV_F_WIDTH,
                             hist=hist, tile=tile, stride=stride)
        hist_buf[c] = gate_buf[tile:tile + hist, :]
        convn_ref[0, :, cols] = gate_buf[hist + tile - keep:hist + tile, :]
        yv = (jax.nn.gelu(gate, approximate=True) * up).astype(BF16)
        y_ref[...] += _dot(yv, wout_ref[cols, :])

    if has_final_norm:
        y_ref[...] = _rms_normalize(y_ref[...]) * gout_ref[...]


def _ffn_layer(x2d, g0, gf, w_in, conv_w, conv_b, w_out, *, layer, groups, stride, tile,
               pre=None, final_gain=None):
    rows, d_model = x2d.shape
    d_ff = w_out.shape[1]
    ff_chunk = _pick_tile(d_ff, 1024)
    n_chunks = d_ff // ff_chunk
    n_tiles = rows // groups // tile
    hist = g0.shape[1]
    keep = (CONV_F_WIDTH - 1) * stride
    const2 = lambda b, t: (0, 0)
    row_map = lambda b, t: (b * n_tiles + t, 0)
    resident = functools.partial(pl.BlockSpec, index_map=const2, pipeline_mode=pl.Buffered(1))
    of_layer = lambda a: pl.BlockSpec((None,) + a.shape[1:], lambda b, t: (layer, 0, 0),
                                      pipeline_mode=pl.Buffered(1))
    operands = [x2d]
    in_specs = [pl.BlockSpec((tile, d_model), row_map)]
    if pre is not None:
        o2d, w_o = pre
        operands += [o2d, w_o]
        in_specs += [pl.BlockSpec((tile, o2d.shape[1]), row_map), resident(w_o.shape)]
    operands += [gf, w_in, conv_w, conv_b, w_out, g0]
    in_specs += [
        resident((1, d_model)),
        of_layer(w_in),
        resident(conv_w.shape),
        resident(conv_b.shape),
        of_layer(w_out),
        pl.BlockSpec((1, hist, d_ff), lambda b, t: (b, 0, 0)),
    ]
    if final_gain is not None:
        operands.append(final_gain)
        in_specs.append(resident((1, d_model)))
    kern = functools.partial(_ffn_kernel, has_pre=pre is not None,
                             has_final_norm=final_gain is not None, stride=stride, hist=hist,
                             tile=tile, ff_chunk=ff_chunk)
    return pl.pallas_call(
        kern,
        grid=(groups, n_tiles),
        in_specs=in_specs,
        out_specs=[
            pl.BlockSpec((tile, d_model), row_map),
            pl.BlockSpec((1, keep, d_ff), lambda b, t: (b, 0, 0)),
        ],
        out_shape=[
            jax.ShapeDtypeStruct((rows, d_model), F32),
            jax.ShapeDtypeStruct((groups, keep, d_ff), F32),
        ],
        scratch_shapes=[
            pltpu.VMEM((hist + tile, ff_chunk), F32),
            pltpu.VMEM((n_chunks, hist, ff_chunk), F32),
        ],
        compiler_params=pltpu.CompilerParams(
            dimension_semantics=("arbitrary", "arbitrary"),
            vmem_limit_bytes=VMEM_LIMIT_BYTES),
        name="conv_ffn",
    )(*operands)


def _kvq_kernel(*refs, n_heads, scale, with_aug, tile):
    (x_ref, gkv_ref, gb_ref, wk_ref, wv_ref, wf_ref, bf_ref, wq_ref) = refs[:8]
    if with_aug:
        tri_ref, pq_ref, pk_ref, oneq_ref, onek_ref = refs[8:13]
        kt_ref, vt_ref, lft_ref, q2t_ref, k2_ref, vtb_ref, f_carry = refs[13:]
    else:
        k_ref, v_ref, lf_ref, q_ref = refs[8:]
    n = _rms_normalize(x_ref[...])
    hk = (n * gkv_ref[...]).astype(BF16)
    hq = (n * gb_ref[...]).astype(BF16)
    k = _dot(hk, wk_ref[...])
    v = _dot(hk, wv_ref[...])
    z = _dot(hk, wf_ref[...]) + bf_ref[...]
    lane = lax.broadcasted_iota(jnp.int32, z.shape, 1)
    lf = jnp.where(lane < n_heads, jax.nn.log_sigmoid(z), 0.0)
    q = _dot(hq, wq_ref[...]) * scale
    if not with_aug:
        k_ref[...] = k
        v_ref[...] = v
        lf_ref[...] = lf[:, :n_heads]
        q_ref[...] = q
        return

    kt_ref[0] = k.T
    vt = v.T
    vt_ref[0] = vt
    vtb_ref[0, 0] = vt.astype(BF16)
    lft_ref[0] = lf.T[:n_heads, :]

    @pl.when(pl.program_id(1) == 0)
    def _():
        f_carry[...] = jnp.zeros_like(f_carry)

    tri = tri_ref[...]
    hi, mid, lo = _split3_bf16(lf)
    f_cum = _dot(tri, hi) + _dot(tri, mid) + _dot(tri, lo) + f_carry[...]
    f_carry[...] = f_cum[tile - 1:tile, :]

    f_cat = jnp.concatenate(_split3_bf16(f_cum * LOG2_E), axis=1)
    aug_qt = (_dot(f_cat, pq_ref[...]) + oneq_ref[...]).T.astype(BF16)
    aug_k = (_dot(f_cat, pk_ref[...]) + onek_ref[...]).astype(BF16)
    qt = q.T.astype(BF16)
    kb = k.astype(BF16)
    for hp in range(n_heads // 2):
        src = slice(hp * LANES, (hp + 1) * LANES)
        q2t_ref[0, 0, 2 * hp * LANES:(2 * hp + 1) * LANES, :] = qt[src, :]
        q2t_ref[0, 0, (2 * hp + 1) * LANES:(2 * hp + 2) * LANES, :] = aug_qt[src, :]
        k2_ref[:, 2 * hp * LANES:(2 * hp + 1) * LANES] = kb[:, src]
        k2_ref[:, (2 * hp + 1) * LANES:(2 * hp + 2) * LANES] = aug_k[:, src]


def _aug_constants(n_heads, tile):
    n_pairs = n_heads // 2
    pq = np.zeros((3 * LANES, n_pairs * LANES), np.float32)
    pk = np.zeros((3 * LANES, n_pairs * LANES), np.float32)
    oneq = np.zeros((1, n_pairs * LANES), np.float32)
    onek = np.zeros((1, n_pairs * LANES), np.float32)
    for h in range(n_heads):
        base = (h // 2) * LANES + (h % 2) * AUG_LANES_PER_HEAD
        for piece in range(3):
            pq[piece * LANES + h, base + piece] = 1.0
            onek[0, base + piece] = 1.0
            pk[piece * LANES + h, base + 3 + piece] = -1.0
            oneq[0, base + 3 + piece] = 1.0
    tri = np.tril(np.ones((tile, tile), np.float32))
    return (jnp.asarray(tri, BF16), jnp.asarray(pq, BF16), jnp.asarray(pk, BF16),
            jnp.asarray(oneq), jnp.asarray(onek))


def _kvq_layer(x2d, gkv, gb, wk, wv, wf, bf, wq, *, n_heads, groups, tile, with_aug):
    rows, d_model = x2d.shape
    hd = wk.shape[1]
    n_tiles = rows // groups // tile
    scale = float(hd // n_heads) ** -0.5
    const2 = lambda b, t: (0, 0)
    row_map = lambda b, t: (b * n_tiles + t, 0)
    operands = [x2d, gkv, gb, wk, wv, wf, bf, wq]
    in_specs = [pl.BlockSpec((tile, d_model), row_map)]
    in_specs += [pl.BlockSpec(a.shape, const2) for a in operands[1:]]
    scratch = []
    if with_aug:
        scale *= LOG2_E
        seq = rows // groups
        col_map = lambda b, t: (b, 0, t)
        consts = _aug_constants(n_heads, tile)
        operands += list(consts)
        in_specs += [pl.BlockSpec(a.shape, const2) for a in consts]
        tile_map = lambda b, t: (b, t, 0, 0)
        out_shape = [jax.ShapeDtypeStruct((groups, hd, seq), F32),
                     jax.ShapeDtypeStruct((groups, hd, seq), F32),
                     jax.ShapeDtypeStruct((groups, n_heads, seq), F32),
                     jax.ShapeDtypeStruct((groups, n_tiles, 2 * hd, tile), BF16),
                     jax.ShapeDtypeStruct((rows, 2 * hd), BF16),
                     jax.ShapeDtypeStruct((groups, n_tiles, hd, tile), BF16)]
        out_specs = [pl.BlockSpec((1, hd, tile), col_map), pl.BlockSpec((1, hd, tile), col_map),
                     pl.BlockSpec((1, n_heads, tile), col_map),
                     pl.BlockSpec((1, 1, 2 * hd, tile), tile_map),
                     pl.BlockSpec((tile, 2 * hd), row_map),
                     pl.BlockSpec((1, 1, hd, tile), tile_map)]
        scratch = [pltpu.VMEM((1, LANES), F32)]
    else:
        out_shape = [jax.ShapeDtypeStruct((rows, hd), F32), jax.ShapeDtypeStruct((rows, hd), F32),
                     jax.ShapeDtypeStruct((rows, n_heads), F32),
                     jax.ShapeDtypeStruct((rows, hd), F32)]
        out_specs = [pl.BlockSpec((tile, hd), row_map), pl.BlockSpec((tile, hd), row_map),
                     pl.BlockSpec((tile, n_heads), row_map), pl.BlockSpec((tile, hd), row_map)]
    kern = functools.partial(_kvq_kernel, n_heads=n_heads, scale=scale, with_aug=with_aug,
                             tile=tile)
    return pl.pallas_call(
        kern,
        grid=(groups, n_tiles),
        in_specs=in_specs,
        out_specs=out_specs,
        out_shape=out_shape,
        scratch_shapes=scratch,
        compiler_params=pltpu.CompilerParams(
            dimension_semantics=("arbitrary", "arbitrary"),
            vmem_limit_bytes=VMEM_LIMIT_BYTES),
        name="kv_q_proj",
    )(*operands)


def _prompt_attn_kernel(q2t_ref, q2t_next_ref, k2_ref, vt_ref, o_ref, qh_buf, s_buf, m_buf,
                        acc_buf, *, tq, tk, head_dim):
    qi = pl.program_id(2)

    def head_query(q2t, j):
        chan = lax.broadcasted_iota(jnp.int32, q2t.shape, 0)
        aug = chan - LANES
        own = ((chan >= j * head_dim) & (chan < (j + 1) * head_dim)) | (
            (aug >= j * AUG_LANES_PER_HEAD) & (aug < (j + 1) * AUG_LANES_PER_HEAD))
        return jnp.where(own, q2t, jnp.zeros_like(q2t))

    for j in range(2):
        qh_buf[j] = head_query(q2t_ref[0, 0], j)
    m_buf[...] = jnp.full_like(m_buf, MASK_VALUE)
    acc_buf[...] = jnp.zeros_like(acc_buf)

    def score(j, ki):
        k2 = k2_ref[0, pl.ds(pl.multiple_of(ki * tk, tk), tk), :]
        s_buf[j] = _dot(k2, qh_buf[j])

    def absorb(j, ki, masked):
        s = s_buf[j]
        if masked:
            causal = (lax.broadcasted_iota(jnp.int32, s.shape, 0)
                      <= lax.broadcasted_iota(jnp.int32, s.shape, 1))
            s = jnp.where(causal, s, MASK_VALUE)
        m_prev = m_buf[j]
        m_next = jnp.maximum(m_prev, jnp.max(s, axis=0, keepdims=True))
        alpha = jnp.exp2(m_prev - m_next)
        p = jnp.exp2(s - m_next)
        m_buf[j] = m_next
        vt = vt_ref[0, ki, j * head_dim:(j + 1) * head_dim, :]
        vt1 = jnp.concatenate([vt, jnp.ones((SUM_ROWS, tk), BF16)], axis=0)
        acc_buf[j] = alpha * acc_buf[j] + _dot(vt1, p.astype(BF16))

    @pl.when(qi == 0)
    def _():
        score(0, 0)

    def full_step(ki):
        score(1, ki)
        absorb(0, ki, masked=False)
        score(0, ki + 1)
        absorb(1, ki, masked=False)

    def two_steps(kp, carry):
        full_step(2 * kp)
        full_step(2 * kp + 1)
        return carry

    lax.fori_loop(0, qi // 2, two_steps, 0)

    @pl.when(qi % 2 == 1)
    def _():
        full_step(qi - 1)

    score(1, qi)
    absorb(0, qi, masked=True)
    absorb(1, qi, masked=True)

    s_buf[0] = _dot(k2_ref[0, 0:tk, :], head_query(q2t_next_ref[0, 0], 0))

    ot = jnp.concatenate(
        [acc_buf[j, 0:head_dim, :] / acc_buf[j, head_dim:head_dim + 1, :] for j in range(2)],
        axis=0)
    o_ref[...] = ot.T.astype(o_ref.dtype)


def _prompt_attention(q2t, k2, vtb, *, n_heads, tq, tk):
    batch, n_tiles, hd, tile = vtb.shape
    assert tile == tq == tk, "attention tiles follow the projection kernel's row tile"
    seq = n_tiles * tile
    rows = batch * seq
    head_dim = hd // n_heads
    n_pairs = n_heads // 2
    kern = functools.partial(_prompt_attn_kernel, tq=tq, tk=tk, head_dim=head_dim)
    return pl.pallas_call(
        kern,
        grid=(batch, n_pairs, n_tiles),
        in_specs=[
            pl.BlockSpec((1, 1, 2 * LANES, tq), lambda b, hp, qi: (b, qi, hp, 0)),
            pl.BlockSpec((1, 1, 2 * LANES, tq),
                         lambda b, hp, qi: (b, jnp.minimum(qi + 1, n_tiles - 1), hp, 0)),
            pl.BlockSpec((1, seq, 2 * LANES), lambda b, hp, qi: (b, 0, hp)),
            pl.BlockSpec((1, n_tiles, LANES, tk), lambda b, hp, qi: (b, 0, hp, 0)),
        ],
        out_specs=pl.BlockSpec((tq, LANES), lambda b, hp, qi: (b * n_tiles + qi, hp)),
        out_shape=jax.ShapeDtypeStruct((rows, hd), BF16),
        scratch_shapes=[
            pltpu.VMEM((2, 2 * LANES, tq), BF16),
            pltpu.VMEM((2, tk, tq), F32),
            pltpu.VMEM((2, 1, tq), F32),
            pltpu.VMEM((2, head_dim + SUM_ROWS, tq), F32),
        ],
        compiler_params=pltpu.CompilerParams(
            dimension_semantics=("arbitrary", "arbitrary", "arbitrary"),
            vmem_limit_bytes=VMEM_LIMIT_BYTES),
        name="prompt_attention",
    )(q2t, q2t, k2.reshape(batch, seq, 2 * hd), vtb)


def _sample_attn_kernel(pt_ref, *refs, n_heads, head_dim, dec_seq, pages_per_step, page_size,
                        n_chains):
    del pt_ref
    n_in = 3 * pages_per_step
    k_refs = refs[0:pages_per_step]
    v_refs = refs[pages_per_step:2 * pages_per_step]
    lf_refs = refs[2 * pages_per_step:n_in]
    q_ref, kn_ref, vn_ref, lfn_ref, sl_ref = refs[n_in:n_in + 5]
    o_ref = refs[n_in + 5]
    qrows_buf, cn_col, carry, m_buf, l_buf, acc_buf = refs[n_in + 6:]
    g = pl.program_id(1)
    n_rows = dec_seq * n_heads
    hd = n_heads * head_dim

    def head_mask(shape):
        r = lax.broadcasted_iota(jnp.int32, shape, 0) % n_heads
        lane = lax.broadcasted_iota(jnp.int32, shape, 1)
        return (lane >= r * head_dim) & (lane < (r + 1) * head_dim)

    def column_of(rowvec):
        wide = jnp.broadcast_to(rowvec, (n_rows, rowvec.shape[1]))
        r = lax.broadcasted_iota(jnp.int32, wide.shape, 0) % n_heads
        lane = lax.broadcasted_iota(jnp.int32, wide.shape, 1)
        return jnp.sum(jnp.where(lane == r, wide, 0.0), axis=1, keepdims=True)

    def online_update(c, s, weighted_values):
        m_prev = m_buf[c]
        m_next = jnp.maximum(m_prev, jnp.max(s, axis=1, keepdims=True))
        alpha = jnp.exp(m_prev - m_next)
        p = jnp.exp(s - m_next[:, 0:1])
        l_buf[c] = alpha * l_buf[c] + jnp.sum(p, axis=1, keepdims=True)
        acc_buf[c] = acc_buf[c] * alpha[:, 0:1] + weighted_values(p.astype(BF16))
        m_buf[c] = m_next

    @pl.when(g == 0)
    def _():
        q = q_ref[0]
        mask = head_mask((n_rows, hd))
        qrep = jnp.concatenate(
            [jnp.broadcast_to(q[t:t + 1, :], (n_heads, hd)) for t in range(dec_seq)], axis=0)
        qrows = jnp.where(mask, qrep, 0.0).astype(BF16)
        qrows_buf[...] = qrows
        m_buf[...] = jnp.full_like(m_buf, MASK_VALUE)
        l_buf[...] = jnp.zeros_like(l_buf)
        acc_buf[...] = jnp.zeros_like(acc_buf)
        carry[...] = jnp.zeros_like(carry)

        lfn = lfn_ref[0]
        cums = []
        run = None
        for t in range(dec_seq):
            run = lfn[t:t + 1, :] if run is None else run + lfn[t:t + 1, :]
            cums.append(run)
        r_t = lax.broadcasted_iota(jnp.int32, (n_rows, 1), 0) // n_heads
        cn = jnp.zeros((n_rows, 1), F32)
        for t in range(dec_seq):
            cn = jnp.where(r_t == t, column_of(cums[t]), cn)
        cn_col[...] = cn
        qrows_f = qrows.astype(F32)
        kn = kn_ref[0].astype(BF16).astype(F32)
        vn = vn_ref[0].astype(BF16).astype(F32)
        lane = lax.broadcasted_iota(jnp.int32, (n_rows, LANES), 1)
        s_new = jnp.full((n_rows, LANES), MASK_VALUE, F32)
        for s in range(dec_seq):
            qk = jnp.sum(qrows_f * kn[s:s + 1, :], axis=1, keepdims=True)
            s_new = jnp.where((lane == s) & (r_t >= s), qk + cn - column_of(cums[s]), s_new)

        def new_values(p):
            p = p.astype(F32)
            return sum(p[:, s:s + 1] * vn[s:s + 1, :] for s in range(dec_seq))

        online_update(0, s_new, new_values)

    sl = sl_ref[...]
    biases = [None] * pages_per_step
    run = carry[...]
    for j in reversed(range(pages_per_step)):
        lft = lf_refs[j][0]
        hi, mid, lo = _split3_bf16(lft)
        suffix = _dot(hi, sl) + _dot(mid, sl) + _dot(lo, sl) + run
        biases[j] = jnp.concatenate([suffix] * dec_seq, axis=0)
        run = run + jnp.sum(lft, axis=1, keepdims=True)
    carry[...] = run
    cn = cn_col[...]
    qrows = qrows_buf[...]
    per_chain = pages_per_step // n_chains
    for c in range(n_chains):
        pages = range(c * per_chain, (c + 1) * per_chain)
        bias = jnp.concatenate([biases[j] for j in pages], axis=1) + cn
        kt = jnp.concatenate([k_refs[j][0].astype(BF16) for j in pages], axis=1)
        vt = jnp.concatenate([v_refs[j][0].astype(BF16) for j in pages], axis=1)
        online_update(c, _dot(qrows, kt) + bias, lambda p, vt=vt: _dot_nt(p, vt))

    @pl.when(g == pl.num_programs(1) - 1)
    def _():
        m_all = m_buf[0]
        for c in range(1, n_chains):
            m_all = jnp.maximum(m_all, m_buf[c])
        l_all = jnp.zeros_like(m_all)
        acc_all = jnp.zeros((n_rows, hd), F32)
        for c in range(n_chains):
            w = jnp.exp(m_buf[c] - m_all)
            l_all = l_all + w * l_buf[c]
            acc_all = acc_all + w[:, 0:1] * acc_buf[c]
        out = jnp.where(head_mask((n_rows, hd)), acc_all / l_all[:, 0:1], 0.0)
        o_ref[0] = jnp.sum(out.reshape(dec_seq, n_heads, hd), axis=1).astype(o_ref.dtype)


def _sample_attention(page_table, cache_kt, cache_vt, cache_lft, q, k_new, v_new, lf_new,
                      *, n_heads, pages_per_step):
    dec_batch, dec_seq, hd = q.shape
    n_pages = page_table.shape[1]
    page_size = cache_kt.shape[2]
    head_dim = hd // n_heads
    n_groups = n_pages // pages_per_step
    n_rows = dec_seq * n_heads
    sl = jnp.asarray(np.tril(np.ones((page_size, page_size), np.float32), -1), BF16)

    def page_map(j):
        def index_map(b, g, pt):
            return (pt[b, (n_groups - 1 - g) * pages_per_step + j], 0, 0)
        return index_map

    seq_map = lambda b, g, pt: (b, 0, 0)
    in_specs = (
        [pl.BlockSpec((1, hd, page_size), page_map(j)) for j in range(pages_per_step)]
        + [pl.BlockSpec((1, hd, page_size), page_map(j)) for j in range(pages_per_step)]
        + [pl.BlockSpec((1, n_heads, page_size), page_map(j)) for j in range(pages_per_step)]
        + [pl.BlockSpec((1, dec_seq, hd), seq_map)] * 3
        + [pl.BlockSpec((1, dec_seq, n_heads), seq_map),
           pl.BlockSpec((page_size, page_size), lambda b, g, pt: (0, 0))])
    n_chains = math.gcd(pages_per_step, SAMPLE_SOFTMAX_CHAINS)
    kern = functools.partial(_sample_attn_kernel, n_heads=n_heads, head_dim=head_dim,
                             dec_seq=dec_seq, pages_per_step=pages_per_step, page_size=page_size,
                             n_chains=n_chains)
    grid_spec = pltpu.PrefetchScalarGridSpec(
        num_scalar_prefetch=1,
        grid=(dec_batch, n_groups),
        in_specs=in_specs,
        out_specs=pl.BlockSpec((1, dec_seq, hd), seq_map),
        scratch_shapes=[
            pltpu.VMEM((n_rows, hd), BF16),
            pltpu.VMEM((n_rows, 1), F32),
            pltpu.VMEM((n_heads, LANES), F32),
            pltpu.VMEM((n_chains, n_rows, LANES), F32),
            pltpu.VMEM((n_chains, n_rows, LANES), F32),
            pltpu.VMEM((n_chains, n_rows, hd), F32),
        ])
    operands = ([cache_kt] * pages_per_step + [cache_vt] * pages_per_step
                + [cache_lft] * pages_per_step + [q, k_new, v_new, lf_new, sl])
    return pl.pallas_call(
        kern,
        grid_spec=grid_spec,
        out_shape=jax.ShapeDtypeStruct((dec_batch, dec_seq, hd), BF16),
        compiler_params=pltpu.CompilerParams(
            dimension_semantics=("arbitrary", "arbitrary"),
            vmem_limit_bytes=VMEM_LIMIT_BYTES),
        name="sample_attention",
    )(page_table, *operands)


def _pad_history(state, hist):
    keep = state.shape[1]
    if keep == hist:
        return state
    return jnp.pad(state, ((0, 0), (hist - keep, 0), (0, 0)))


def _to_time_major(a):
    b, t = a.shape[:2]
    return jnp.swapaxes(a, 0, 1).reshape((1, t * b) + a.shape[2:])


def _from_time_major(a, b):
    t = a.shape[1] // b
    return jnp.swapaxes(a.reshape((t, b) + a.shape[2:]), 0, 1)


def kernel(x_prompt, x_sample, cache_k, cache_v, cache_logf, state_lru_h, state_lru_conv,
           state_ffn_conv, page_table, norm_a_g, w_a_in, conv_a_w, conv_a_b, w_a_gate, b_a_gate,
           lru_lambda, w_a_out, norm_f_g, w_f_in, conv_f_w, conv_f_b, w_f_out, norm_kv_g, w_kv,
           b_forget, norm_b_g, w_q, w_o, norm_out_g):
    batch, seq, d_model = x_prompt.shape
    dec_batch, dec_seq, _ = x_sample.shape
    n_phys, page_size, n_heads, head_dim = cache_k.shape
    hd = n_heads * head_dim
    n_a = w_a_in.shape[0]
    n_b = w_q.shape[0]
    d_rnn = w_a_out.shape[1]
    d_ff = w_f_out.shape[1]
    assert 2 * head_dim == LANES and n_heads % 2 == 0 and n_heads <= LANES
    assert dec_batch % SUBLANES == 0 and dec_seq >= CONV_A_WIDTH - 1

    row = lambda a: a.reshape(1, -1).astype(F32)
    w_a_in_b = w_a_in.astype(BF16)
    w_a_gate_b = w_a_gate.astype(BF16)
    w_a_out_b = w_a_out.astype(BF16)
    w_f_in_b = w_f_in.astype(BF16)
    w_f_out_b = w_f_out.astype(BF16)
    w_q_b = w_q.astype(BF16)
    w_o_b = w_o.astype(BF16)
    wk = w_kv[:, :hd].astype(BF16)
    wv = w_kv[:, hd:2 * hd].astype(BF16)
    wf = jnp.pad(w_kv[:, 2 * hd:], ((0, 0), (0, LANES - n_heads))).astype(BF16)
    bf = jnp.pad(b_forget.astype(F32), (0, LANES - n_heads)).reshape(1, LANES)

    def trunk(x2d, lru_h0, lru_conv0, ffn_conv0, *, groups, stride, tiles, with_aug,
              attention_fn):
        lru_h_new, lru_conv_new, ffn_conv_new = [], [], []
        layer = 0
        hist_a = _round_up((CONV_A_WIDTH - 1) * stride, SUBLANES)
        hist_f = _round_up((CONV_F_WIDTH - 1) * stride, SUBLANES)

        def ffn(x2d, layer, **kw):
            return _ffn_layer(x2d, _pad_history(ffn_conv0[layer], hist_f), row(norm_f_g[layer]),
                              w_f_in_b, conv_f_w[layer], row(conv_f_b[layer]),
                              w_f_out_b, layer=layer, groups=groups, stride=stride,
                              tile=tiles["ffn"], **kw)

        for i in range(n_a):
            x2d, c_new, h_new = _rglru_layer(
                x2d, _pad_history(lru_conv0[i], hist_a), lru_h0[i], row(norm_a_g[i]), w_a_in_b[i],
                conv_a_w[i], row(conv_a_b[i]), w_a_gate_b[i], b_a_gate[i][:, None, :],
                row(lru_lambda[i]), w_a_out_b[i], groups=groups, stride=stride,
                tile=tiles["rglru"])
            x2d, f_new = ffn(x2d, layer)
            lru_h_new.append(h_new)
            lru_conv_new.append(c_new)
            ffn_conv_new.append(f_new)
            layer += 1
        kvq = _kvq_layer(x2d, row(norm_kv_g), row(norm_b_g[0]), wk, wv, wf, bf, w_q_b[0],
                         n_heads=n_heads, groups=groups, tile=tiles["kvq"],
                         with_aug=with_aug)
        k_new, v_new, lf_new = kvq[:3]
        for j in range(n_b):
            assert j == 0, "one attention layer per shared K/V projection is supported"
            o2d = attention_fn(kvq)
            x2d, f_new = ffn(x2d, layer, pre=(o2d, w_o_b[j]),
                             final_gain=row(norm_out_g) if j == n_b - 1 else None)
            ffn_conv_new.append(f_new)
            layer += 1
        return x2d, k_new, v_new, lf_new, lru_h_new, lru_conv_new, ffn_conv_new

    zeros = lambda *s: jnp.zeros(s, F32)
    tiles = {"rglru": _pick_tile(seq, 256), "ffn": _pick_tile(seq, 512),
             "kvq": _pick_tile(seq, 512)}

    def prompt_attention_fn(kvq):
        q2t, k2, vtb = kvq[3:]
        return _prompt_attention(q2t, k2, vtb, n_heads=n_heads, tq=tiles["kvq"], tk=tiles["kvq"])

    (y, k_p, v_p, lf_p, h_p, c_p, f_p) = trunk(
        x_prompt.reshape(batch * seq, d_model),
        zeros(n_a, batch, 1, d_rnn), zeros(n_a, batch, CONV_A_WIDTH - 1, d_rnn),
        zeros(n_a + n_b, batch, CONV_F_WIDTH - 1, d_ff),
        groups=batch, stride=1, tiles=tiles, with_aug=True, attention_fn=prompt_attention_fn)
    y_prompt = y.reshape(batch, seq, d_model)
    k_prompt = k_p.reshape(batch, n_heads, head_dim, seq).transpose(0, 3, 1, 2)
    v_prompt = v_p.reshape(batch, n_heads, head_dim, seq).transpose(0, 3, 1, 2)
    logf_prompt = lf_p.transpose(0, 2, 1)
    lru_h_prompt = jnp.stack([h.reshape(batch, d_rnn) for h in h_p], axis=0)
    lru_conv_prompt = jnp.stack(c_p, axis=0)
    ffn_conv_prompt = jnp.stack(f_p, axis=0)

    cache_kt = cache_k.transpose(0, 2, 3, 1).reshape(n_phys, hd, page_size)
    cache_vt = cache_v.transpose(0, 2, 3, 1).reshape(n_phys, hd, page_size)
    cache_lft = jnp.swapaxes(cache_logf, 1, 2)
    n_pages = page_table.shape[1]

    def sample_attention_fn(kvq):
        k_t, v_t, lf_t, q_t = kvq
        bm = lambda a: _from_time_major(a[None], dec_batch)
        o = _sample_attention(page_table, cache_kt, cache_vt, cache_lft, bm(q_t), bm(k_t),
                              bm(v_t), bm(lf_t), n_heads=n_heads,
                              pages_per_step=_pick_tile(n_pages, SAMPLE_PAGES_PER_STEP))
        return _to_time_major(o)[0]

    n_rows = dec_batch * dec_seq
    tiles = {"rglru": n_rows, "ffn": n_rows, "kvq": n_rows}
    (y, k_s, v_s, lf_s, h_s, c_s, f_s) = trunk(
        _to_time_major(x_sample)[0],
        state_lru_h[:, None], jnp.stack([_to_time_major(s) for s in state_lru_conv]),
        jnp.stack([_to_time_major(s) for s in state_ffn_conv]),
        groups=1, stride=dec_batch, tiles=tiles, with_aug=False,
        attention_fn=sample_attention_fn)
    bm = lambda a: _from_time_major(a[None], dec_batch)
    y_sample = bm(y)
    k_sample = bm(k_s).reshape(dec_batch, dec_seq, n_heads, head_dim)
    v_sample = bm(v_s).reshape(dec_batch, dec_seq, n_heads, head_dim)
    logf_sample = bm(lf_s)
    lru_h_sample = jnp.stack([h[0] for h in h_s], axis=0)
    lru_conv_sample = jnp.stack([_from_time_major(c, dec_batch) for c in c_s], axis=0)
    ffn_conv_sample = jnp.stack([_from_time_major(f, dec_batch) for f in f_s], axis=0)

    return (y_prompt, y_sample, k_prompt, v_prompt, logf_prompt, lru_h_prompt, lru_conv_prompt,
            ffn_conv_prompt, k_sample, v_sample, logf_sample, lru_h_sample, lru_conv_sample,
            ffn_conv_sample)
```

```python
import functools
import math

import numpy as np
import jax
import jax.numpy as jnp
from jax import lax
from jax.experimental import pallas as pl
from jax.experimental.pallas import tpu as pltpu

F32 = jnp.float32
BF16 = jnp.bfloat16

LANES = 128
SUBLANES = 8
VMEM_LIMIT_BYTES = 56 * 1024 * 1024

RMS_EPS = 1e-6
LRU_C = 8.0
CONV_A_WIDTH = 4
CONV_F_WIDTH = 3
MASK_VALUE = -1e30
LOG2_E = 1.4426950408889634

AUG_LANES_PER_HEAD = 8
SUM_ROWS = 16
SAMPLE_PAGES_PER_STEP = 16
SAMPLE_SOFTMAX_CHAINS = 4


def _round_up(x, m):
    return (x + m - 1) // m * m


def _pick_tile(n, target):
    t = min(n, target)
    while n % t:
        t //= 2
    return t


def _rms_normalize(x):
    return x * lax.rsqrt(jnp.mean(x * x, axis=-1, keepdims=True) + RMS_EPS)


def _split3_bf16(x):
    hi = x.astype(BF16)
    r1 = x - hi.astype(F32)
    mid = r1.astype(BF16)
    lo = (r1 - mid.astype(F32)).astype(BF16)
    return hi, mid, lo


def _dot(a, b):
    return jnp.dot(a, b, preferred_element_type=F32)


def _dot_nt(a, b):
    return lax.dot_general(a, b, (((1,), (1,)), ((), ())), preferred_element_type=F32)


def _shifted_conv(buf_ref, w, b, *, width, hist, tile, stride):
    acc = None
    for j in range(width):
        back = (width - 1 - j) * stride
        term = buf_ref[hist - back:hist - back + tile, :] * w[j:j + 1, :]
        acc = term if acc is None else acc + term
    return acc + b


def _rglru_kernel(x_ref, conv0_ref, h0_ref, g_ref, win_ref, cw_ref, cb_ref, wg_ref, bg_ref,
                  lam_ref, wout_ref, perm_ref, y_ref, convn_ref, hn_ref, rec_buf, a_buf, b_buf,
                  h_carry, tail_buf, *, stride, tile):
    t = pl.program_id(1)
    d_rnn = rec_buf.shape[1]
    n_blocks, lru_block, _ = wg_ref.shape
    interleaved = stride == 1
    step_rows = SUBLANES if interleaved else stride
    hist = (CONV_A_WIDTH - 1) * step_rows
    seg = tile // SUBLANES
    sub = lax.broadcasted_iota(jnp.int32, (SUBLANES, d_rnn), 0)

    @pl.when(t == 0)
    def _():
        h_carry[...] = h0_ref[0]
        if interleaved:
            tail_buf[...] = conv0_ref[0]
        else:
            rec_buf[0:hist, :] = conv0_ref[0]

    x = x_ref[...]
    hn = (_rms_normalize(x) * g_ref[...]).astype(BF16)
    if interleaved:
        hn = _dot(perm_ref[0], hn).astype(BF16)
    else:
        @pl.when(t != 0)
        def _():
            rec_buf[0:hist, :] = rec_buf[tile:tile + hist, :]

    proj = _dot(hn, win_ref[...])
    rec_buf[hist:hist + tile, :] = proj[:, d_rnn:]
    if interleaved:
        for m in range(1, CONV_A_WIDTH):
            src = hist + (seg - m) * SUBLANES
            block = pltpu.roll(rec_buf[src:src + SUBLANES, :], 1, 0)
            prev = tail_buf[SUBLANES - m:SUBLANES - m + 1, :]
            dst = (CONV_A_WIDTH - 1 - m) * SUBLANES
            rec_buf[dst:dst + SUBLANES, :] = jnp.where(sub == 0, prev, block)
        for m in range(1, CONV_A_WIDTH):
            last = hist + (seg - m) * SUBLANES + SUBLANES - 1
            tail_buf[SUBLANES - m:SUBLANES - m + 1, :] = rec_buf[last:last + 1, :]
    xc = _shifted_conv(rec_buf, cw_ref[...], cb_ref[...], width=CONV_A_WIDTH, hist=hist,
                       tile=tile, stride=step_rows)
    xcb = xc.astype(BF16)
    log_sig_lam = jax.nn.log_sigmoid(lam_ref[...])
    for n in range(n_blocks):
        cols = slice(n * lru_block, (n + 1) * lru_block)
        gates = jax.nn.sigmoid(_dot(xcb[:, cols], wg_ref[n]) + bg_ref[n])
        r = gates[:, :lru_block]
        ig = gates[:, lru_block:]
        log_a = LRU_C * r * log_sig_lam[:, cols]
        a = jnp.exp(log_a)
        a_buf[:, cols] = a
        one_minus_a2 = -jnp.tanh(log_a) * (a * a + 1.0)
        b_buf[:, cols] = jnp.sqrt(one_minus_a2) * ig * xc[:, cols]

    if interleaved:
        h_loc = jnp.zeros((SUBLANES, d_rnn), F32)
        decay = jnp.ones((SUBLANES, d_rnn), F32)
        for j in range(seg):
            rows = slice(j * SUBLANES, (j + 1) * SUBLANES)
            a_j = a_buf[rows, :]
            h_loc = a_j * h_loc + b_buf[rows, :]
            decay = a_j * decay
            b_buf[rows, :] = h_loc
            a_buf[rows, :] = decay
        carry = h_carry[...]
        carry_in = jnp.zeros((SUBLANES, d_rnn), F32)
        for s in range(SUBLANES):
            carry_in = jnp.where(sub == s, carry, carry_in)
            carry = h_loc[s:s + 1, :] + decay[s:s + 1, :] * carry
        h_last = carry
        hs = (b_buf[...].reshape(seg, SUBLANES, d_rnn)
              + a_buf[...].reshape(seg, SUBLANES, d_rnn) * carry_in[None]).reshape(tile, d_rnn)
        convn_ref[0] = tail_buf[SUBLANES - (CONV_A_WIDTH - 1):SUBLANES, :]
    else:
        h_last = h_carry[...]
        for k in range(tile // stride):
            rows = slice(k * stride, (k + 1) * stride)
            h_last = a_buf[rows, :] * h_last + b_buf[rows, :]
            b_buf[rows, :] = h_last
        hs = b_buf[...]
        convn_ref[0] = rec_buf[tile:tile + hist, :]
    h_carry[...] = h_last
    hn_ref[0] = h_last

    gate_branch = proj[:, :d_rnn]
    yv = (hs * jax.nn.gelu(gate_branch, approximate=True)).astype(BF16)
    if interleaved:
        yv = _dot(perm_ref[1], yv).astype(BF16)
    y_ref[...] = x + _dot(yv, wout_ref[...])


def _rglru_layer(x2d, conv0, h0, g, w_in, conv_w, conv_b, w_gate, b_gate, lam, w_out,
                 *, groups, stride, tile):
    rows, d_model = x2d.shape
    d_rnn = w_out.shape[0]
    n_tiles = rows // groups // tile
    hist = conv0.shape[1]
    keep = (CONV_A_WIDTH - 1) * stride
    step_rows = SUBLANES if stride == 1 else stride
    const2 = lambda b, t: (0, 0)
    const3 = lambda b, t: (0, 0, 0)
    seg = tile // SUBLANES
    order = np.arange(tile).reshape(SUBLANES, seg).T.reshape(-1)
    gather = np.zeros((tile, tile), np.float32)
    gather[np.arange(tile), order] = 1.0
    perm = jnp.asarray(np.stack([gather, gather.T]), BF16)
    kern = functools.partial(_rglru_kernel, stride=stride, tile=tile)
    return pl.pallas_call(
        kern,
        grid=(groups, n_tiles),
        in_specs=[
            pl.BlockSpec((tile, d_model), lambda b, t: (b * n_tiles + t, 0)),
            pl.BlockSpec((1, hist, d_rnn), lambda b, t: (b, 0, 0)),
            pl.BlockSpec((1, stride, d_rnn), lambda b, t: (b, 0, 0)),
            pl.BlockSpec((1, d_model), const2),
            pl.BlockSpec(w_in.shape, const2),
            pl.BlockSpec(conv_w.shape, const2),
            pl.BlockSpec((1, d_rnn), const2),
            pl.BlockSpec(w_gate.shape, const3),
            pl.BlockSpec(b_gate.shape, const3),
            pl.BlockSpec((1, d_rnn), const2),
            pl.BlockSpec(w_out.shape, const2),
            pl.BlockSpec(perm.shape, const3),
        ],
        out_specs=[
            pl.BlockSpec((tile, d_model), lambda b, t: (b * n_tiles + t, 0)),
            pl.BlockSpec((1, keep, d_rnn), lambda b, t: (b, 0, 0)),
            pl.BlockSpec((1, stride, d_rnn), lambda b, t: (b, 0, 0)),
        ],
        out_shape=[
            jax.ShapeDtypeStruct((rows, d_model), F32),
            jax.ShapeDtypeStruct((groups, keep, d_rnn), F32),
            jax.ShapeDtypeStruct((groups, stride, d_rnn), F32),
        ],
        scratch_shapes=[
            pltpu.VMEM(((CONV_A_WIDTH - 1) * step_rows + tile, d_rnn), F32),
            pltpu.VMEM((tile, d_rnn), F32),
            pltpu.VMEM((tile, d_rnn), F32),
            pltpu.VMEM((stride, d_rnn), F32),
            pltpu.VMEM((SUBLANES, d_rnn), F32),
        ],
        compiler_params=pltpu.CompilerParams(
            dimension_semantics=("arbitrary", "arbitrary"),
            vmem_limit_bytes=VMEM_LIMIT_BYTES),
        name="rglru_layer",
    )(x2d, conv0, h0, g, w_in, conv_w, conv_b, w_gate, b_gate, lam, w_out, perm)


def _ffn_kernel(*refs, has_pre, has_final_norm, stride, hist, tile, ff_chunk):
    refs = list(refs)
    x_ref = refs.pop(0)
    if has_pre:
        o_ref = refs.pop(0)
        wo_ref = refs.pop(0)
    gf_ref, win_ref, cw_ref, cb_ref, wout_ref, g0_ref = refs[:6]
    refs = refs[6:]
    if has_final_norm:
        gout_ref = refs.pop(0)
    y_ref, convn_ref, gate_buf, hist_buf = refs
    t = pl.program_id(1)
    d_ff = wout_ref.shape[0]
    keep = (CONV_F_WIDTH - 1) * stride

    x1 = x_ref[...]
    if has_pre:
        x1 = x1 + _dot(o_ref[...], wo_ref[...])
    y_ref[...] = x1
    h = (_rms_normalize(x1) * gf_ref[...]).astype(BF16)

    for c in range(d_ff // ff_chunk):
        cols = slice(c * ff_chunk, (c + 1) * ff_chunk)
        up_cols = slice(d_ff + c * ff_chunk, d_ff + (c + 1) * ff_chunk)

        @pl.when(t == 0)
        def _():
            gate_buf[0:hist, :] = g0_ref[0, :, cols]

        @pl.when(t != 0)
        def _():
            gate_buf[0:hist, :] = hist_buf[c]

        gate_buf[hist:hist + tile, :] = _dot(h, win_ref[:, cols])
        up = _dot(h, win_ref[:, up_cols])
        gate = _shifted_conv(gate_buf, cw_ref[:, cols], cb_ref[:, cols], width=CONV_F_WIDTH,
                             hist=hist, tile=tile, stride=stride)
        hist_buf[c] = gate_buf[tile:tile + hist, :]
        convn_ref[0, :, cols] = gate_buf[hist + tile - keep:hist + tile, :]
        yv = (jax.nn.gelu(gate, approximate=True) * up).astype(BF16)
        y_ref[...] += _dot(yv, wout_ref[cols, :])

    if has_final_norm:
        y_ref[...] = _rms_normalize(y_ref[...]) * gout_ref[...]


def _ffn_layer(x2d, g0, gf, w_in, conv_w, conv_b, w_out, *, layer, groups, stride, tile,
               pre=None, final_gain=None):
    rows, d_model = x2d.shape
    d_ff = w_out.shape[1]
    ff_chunk = _pick_tile(d_ff, 1024)
    n_chunks = d_ff // ff_chunk
    n_tiles = rows // groups // tile
    hist = g0.shape[1]
    keep = (CONV_F_WIDTH - 1) * stride
    const2 = lambda b, t: (0, 0)
    row_map = lambda b, t: (b * n_tiles + t, 0)
    resident = functools.partial(pl.BlockSpec, index_map=const2, pipeline_mode=pl.Buffered(1))
    of_layer = lambda a: pl.BlockSpec((None,) + a.shape[1:], lambda b, t: (layer, 0, 0),
                                      pipeline_mode=pl.Buffered(1))
    operands = [x2d]
    in_specs = [pl.BlockSpec((tile, d_model), row_map)]
    if pre is not None:
        o2d, w_o = pre
        operands += [o2d, w_o]
        in_specs += [pl.BlockSpec((tile, o2d.shape[1]), row_map), resident(w_o.shape)]
    operands += [gf, w_in, conv_w, conv_b, w_out, g0]
    in_specs += [
        resident((1, d_model)),
        of_layer(w_in),
        resident(conv_w.shape),
        resident(conv_b.shape),
        of_layer(w_out),
        pl.BlockSpec((1, hist, d_ff), lambda b, t: (b, 0, 0)),
    ]
    if final_gain is not None:
        operands.append(final_gain)
        in_specs.append(resident((1, d_model)))
    kern = functools.partial(_ffn_kernel, has_pre=pre is not None,
                             has_final_norm=final_gain is not None, stride=stride, hist=hist,
                             tile=tile, ff_chunk=ff_chunk)
    return pl.pallas_call(
        kern,
        grid=(groups, n_tiles),
        in_specs=in_specs,
        out_specs=[
            pl.BlockSpec((tile, d_model), row_map),
            pl.BlockSpec((1, keep, d_ff), lambda b, t: (b, 0, 0)),
        ],
        out_shape=[
            jax.ShapeDtypeStruct((rows, d_model), F32),
            jax.ShapeDtypeStruct((groups, keep, d_ff), F32),
        ],
        scratch_shapes=[
            pltpu.VMEM((hist + tile, ff_chunk), F32),
            pltpu.VMEM((n_chunks, hist, ff_chunk), F32),
        ],
        compiler_params=pltpu.CompilerParams(
            dimension_semantics=("arbitrary", "arbitrary"),
            vmem_limit_bytes=VMEM_LIMIT_BYTES),
        name="conv_ffn",
    )(*operands)


def _kvq_kernel(*refs, n_heads, scale, with_aug, tile):
    (x_ref, gkv_ref, gb_ref, wk_ref, wv_ref, wf_ref, bf_ref, wq_ref) = refs[:8]
    if with_aug:
        tri_ref, pq_ref, pk_ref, oneq_ref, onek_ref = refs[8:13]
        kt_ref, vt_ref, lft_ref, q2t_ref, k2_ref, vtb_ref, f_carry = refs[13:]
    else:
        k_ref, v_ref, lf_ref, q_ref = refs[8:]
    n = _rms_normalize(x_ref[...])
    hk = (n * gkv_ref[...]).astype(BF16)
    hq = (n * gb_ref[...]).astype(BF16)
    k = _dot(hk, wk_ref[...])
    v = _dot(hk, wv_ref[...])
    z = _dot(hk, wf_ref[...]) + bf_ref[...]
    lane = lax.broadcasted_iota(jnp.int32, z.shape, 1)
    lf = jnp.where(lane < n_heads, jax.nn.log_sigmoid(z), 0.0)
    q = _dot(hq, wq_ref[...]) * scale
    if not with_aug:
        k_ref[...] = k
        v_ref[...] = v
        lf_ref[...] = lf[:, :n_heads]
        q_ref[...] = q
        return

    kt_ref[0] = k.T
    vt = v.T
    vt_ref[0] = vt
    vtb_ref[0, 0] = vt.astype(BF16)
    lft_ref[0] = lf.T[:n_heads, :]

    @pl.when(pl.program_id(1) == 0)
    def _():
        f_carry[...] = jnp.zeros_like(f_carry)

    tri = tri_ref[...]
    hi, mid, lo = _split3_bf16(lf)
    f_cum = _dot(tri, hi) + _dot(tri, mid) + _dot(tri, lo) + f_carry[...]
    f_carry[...] = f_cum[tile - 1:tile, :]

    f_cat = jnp.concatenate(_split3_bf16(f_cum * LOG2_E), axis=1)
    aug_qt = (_dot(f_cat, pq_ref[...]) + oneq_ref[...]).T.astype(BF16)
    aug_k = (_dot(f_cat, pk_ref[...]) + onek_ref[...]).astype(BF16)
    qt = q.T.astype(BF16)
    kb = k.astype(BF16)
    for hp in range(n_heads // 2):
        src = slice(hp * LANES, (hp + 1) * LANES)
        q2t_ref[0, 0, 2 * hp * LANES:(2 * hp + 1) * LANES, :] = qt[src, :]
        q2t_ref[0, 0, (2 * hp + 1) * LANES:(2 * hp + 2) * LANES, :] = aug_qt[src, :]
        k2_ref[:, 2 * hp * LANES:(2 * hp + 1) * LANES] = kb[:, src]
        k2_ref[:, (2 * hp + 1) * LANES:(2 * hp + 2) * LANES] = aug_k[:, src]


def _aug_constants(n_heads, tile):
    n_pairs = n_heads // 2
    pq = np.zeros((3 * LANES, n_pairs * LANES), np.float32)
    pk = np.zeros((3 * LANES, n_pairs * LANES), np.float32)
    oneq = np.zeros((1, n_pairs * LANES), np.float32)
    onek = np.zeros((1, n_pairs * LANES), np.float32)
    for h in range(n_heads):
        base = (h // 2) * LANES + (h % 2) * AUG_LANES_PER_HEAD
        for piece in range(3):
            pq[piece * LANES + h, base + piece] = 1.0
            onek[0, base + piece] = 1.0
            pk[piece * LANES + h, base + 3 + piece] = -1.0
            oneq[0, base + 3 + piece] = 1.0
    tri = np.tril(np.ones((tile, tile), np.float32))
    return (jnp.asarray(tri, BF16), jnp.asarray(pq, BF16), jnp.asarray(pk, BF16),
            jnp.asarray(oneq), jnp.asarray(onek))


def _kvq_layer(x2d, gkv, gb, wk, wv, wf, bf, wq, *, n_heads, groups, tile, with_aug):
    rows, d_model = x2d.shape
    hd = wk.shape[1]
    n_tiles = rows // groups // tile
    scale = float(hd // n_heads) ** -0.5
    const2 = lambda b, t: (0, 0)
    row_map = lambda b, t: (b * n_tiles + t, 0)
    operands = [x2d, gkv, gb, wk, wv, wf, bf, wq]
    in_specs = [pl.BlockSpec((tile, d_model), row_map)]
    in_specs += [pl.BlockSpec(a.shape, const2) for a in operands[1:]]
    scratch = []
    if with_aug:
        scale *= LOG2_E
        seq = rows // groups
        col_map = lambda b, t: (b, 0, t)
        consts = _aug_constants(n_heads, tile)
        operands += list(consts)
        in_specs += [pl.BlockSpec(a.shape, const2) for a in consts]
        tile_map = lambda b, t: (b, t, 0, 0)
        out_shape = [jax.ShapeDtypeStruct((groups, hd, seq), F32),
                     jax.ShapeDtypeStruct((groups, hd, seq), F32),
                     jax.ShapeDtypeStruct((groups, n_heads, seq), F32),
                     jax.ShapeDtypeStruct((groups, n_tiles, 2 * hd, tile), BF16),
                     jax.ShapeDtypeStruct((rows, 2 * hd), BF16),
                     jax.ShapeDtypeStruct((groups, n_tiles, hd, tile), BF16)]
        out_specs = [pl.BlockSpec((1, hd, tile), col_map), pl.BlockSpec((1, hd, tile), col_map),
                     pl.BlockSpec((1, n_heads, tile), col_map),
                     pl.BlockSpec((1, 1, 2 * hd, tile), tile_map),
                     pl.BlockSpec((tile, 2 * hd), row_map),
                     pl.BlockSpec((1, 1, hd, tile), tile_map)]
        scratch = [pltpu.VMEM((1, LANES), F32)]
    else:
        out_shape = [jax.ShapeDtypeStruct((rows, hd), F32), jax.ShapeDtypeStruct((rows, hd), F32),
                     jax.ShapeDtypeStruct((rows, n_heads), F32),
                     jax.ShapeDtypeStruct((rows, hd), F32)]
        out_specs = [pl.BlockSpec((tile, hd), row_map), pl.BlockSpec((tile, hd), row_map),
                     pl.BlockSpec((tile, n_heads), row_map), pl.BlockSpec((tile, hd), row_map)]
    kern = functools.partial(_kvq_kernel, n_heads=n_heads, scale=scale, with_aug=with_aug,
                             tile=tile)
    return pl.pallas_call(
        kern,
        grid=(groups, n_tiles),
        in_specs=in_specs,
        out_specs=out_specs,
        out_shape=out_shape,
        scratch_shapes=scratch,
        compiler_params=pltpu.CompilerParams(
            dimension_semantics=("arbitrary", "arbitrary"),
            vmem_limit_bytes=VMEM_LIMIT_BYTES),
        name="kv_q_proj",
    )(*operands)


def _prompt_attn_kernel(q2t_ref, k2_ref, vt_ref, o_ref, qh_buf, s_buf, m_buf, acc_buf,
                        *, tq, tk, head_dim, n_tiles):
    def head_query(qi, j):
        q2t = q2t_ref[0, qi]
        chan = lax.broadcasted_iota(jnp.int32, q2t.shape, 0)
        aug = chan - LANES
        own = ((chan >= j * head_dim) & (chan < (j + 1) * head_dim)) | (
            (aug >= j * AUG_LANES_PER_HEAD) & (aug < (j + 1) * AUG_LANES_PER_HEAD))
        return jnp.where(own, q2t, jnp.zeros_like(q2t))

    def score(j, ki):
        k2 = k2_ref[0, pl.ds(pl.multiple_of(ki * tk, tk), tk), :]
        s_buf[j] = _dot(k2, qh_buf[j])

    def absorb(j, ki, masked):
        s = s_buf[j]
        if masked:
            causal = (lax.broadcasted_iota(jnp.int32, s.shape, 0)
                      <= lax.broadcasted_iota(jnp.int32, s.shape, 1))
            s = jnp.where(causal, s, MASK_VALUE)
        m_prev = m_buf[j]
        m_next = jnp.maximum(m_prev, jnp.max(s, axis=0, keepdims=True))
        alpha = jnp.exp2(m_prev - m_next)
        p = jnp.exp2(s - m_next)
        m_buf[j] = m_next
        vt = vt_ref[0, ki, j * head_dim:(j + 1) * head_dim, :]
        vt1 = jnp.concatenate([vt, jnp.ones((SUM_ROWS, tk), BF16)], axis=0)
        acc_buf[j] = alpha * acc_buf[j] + _dot(vt1, p.astype(BF16))

    def full_step(ki):
        score(1, ki)
        absorb(0, ki, masked=False)
        score(0, ki + 1)
        absorb(1, ki, masked=False)

    def two_steps(kp, carry):
        full_step(2 * kp)
        full_step(2 * kp + 1)
        return carry

    s_buf[0] = _dot(k2_ref[0, 0:tk, :], head_query(0, 0))

    def query_tile(qi, carry):
        for j in range(2):
            qh_buf[j] = head_query(qi, j)
        m_buf[...] = jnp.full_like(m_buf, MASK_VALUE)
        acc_buf[...] = jnp.zeros_like(acc_buf)

        lax.fori_loop(0, qi // 2, two_steps, 0)

        @pl.when(qi % 2 == 1)
        def _():
            full_step(qi - 1)

        score(1, qi)
        absorb(0, qi, masked=True)
        absorb(1, qi, masked=True)

        nxt = jnp.minimum(qi + 1, n_tiles - 1)
        s_buf[0] = _dot(k2_ref[0, 0:tk, :], head_query(nxt, 0))

        ot = jnp.concatenate(
            [acc_buf[j, 0:head_dim, :] / acc_buf[j, head_dim:head_dim + 1, :]
             for j in range(2)], axis=0)
        o_ref[pl.ds(pl.multiple_of(qi * tq, tq), tq), :] = ot.T.astype(o_ref.dtype)
        return carry

    lax.fori_loop(0, n_tiles, query_tile, 0)


def _prompt_attention(q2t, k2, vtb, *, n_heads, tq, tk):
    batch, n_tiles, hd, tile = vtb.shape
    assert tile == tq == tk, "attention tiles follow the projection kernel's row tile"
    seq = n_tiles * tile
    rows = batch * seq
    head_dim = hd // n_heads
    n_pairs = n_heads // 2
    kern = functools.partial(_prompt_attn_kernel, tq=tq, tk=tk, head_dim=head_dim,
                             n_tiles=n_tiles)
    return pl.pallas_call(
        kern,
        grid=(batch, n_pairs),
        in_specs=[
            pl.BlockSpec((1, n_tiles, 2 * LANES, tq), lambda b, hp: (b, 0, hp, 0)),
            pl.BlockSpec((1, seq, 2 * LANES), lambda b, hp: (b, 0, hp)),
            pl.BlockSpec((1, n_tiles, LANES, tk), lambda b, hp: (b, 0, hp, 0)),
        ],
        out_specs=pl.BlockSpec((seq, LANES), lambda b, hp: (b, hp)),
        out_shape=jax.ShapeDtypeStruct((rows, hd), BF16),
        scratch_shapes=[
            pltpu.VMEM((2, 2 * LANES, tq), BF16),
            pltpu.VMEM((2, tk, tq), F32),
            pltpu.VMEM((2, 1, tq), F32),
            pltpu.VMEM((2, head_dim + SUM_ROWS, tq), F32),
        ],
        compiler_params=pltpu.CompilerParams(
            dimension_semantics=("arbitrary", "arbitrary"),
            vmem_limit_bytes=VMEM_LIMIT_BYTES),
        name="prompt_attention",
    )(q2t, k2.reshape(batch, seq, 2 * hd), vtb)


def _sample_attn_kernel(pt_ref, *refs, n_heads, head_dim, dec_seq, pages_per_step, page_size,
                        n_chains):
    del pt_ref
    n_in = 3 * pages_per_step
    k_refs = refs[0:pages_per_step]
    v_refs = refs[pages_per_step:2 * pages_per_step]
    lf_refs = refs[2 * pages_per_step:n_in]
    q_ref, kn_ref, vn_ref, lfn_ref, sl_ref = refs[n_in:n_in + 5]
    o_ref = refs[n_in + 5]
    qrows_buf, cn_col, carry, m_buf, l_buf, acc_buf = refs[n_in + 6:]
    g = pl.program_id(1)
    n_rows = dec_seq * n_heads
    hd = n_heads * head_dim

    def head_mask(shape):
        r = lax.broadcasted_iota(jnp.int32, shape, 0) % n_heads
        lane = lax.broadcasted_iota(jnp.int32, shape, 1)
        return (lane >= r * head_dim) & (lane < (r + 1) * head_dim)

    def column_of(rowvec):
        wide = jnp.broadcast_to(rowvec, (n_rows, rowvec.shape[1]))
        r = lax.broadcasted_iota(jnp.int32, wide.shape, 0) % n_heads
        lane = lax.broadcasted_iota(jnp.int32, wide.shape, 1)
        return jnp.sum(jnp.where(lane == r, wide, 0.0), axis=1, keepdims=True)

    def online_update(c, s, weighted_values):
        m_prev = m_buf[c]
        m_next = jnp.maximum(m_prev, jnp.max(s, axis=1, keepdims=True))
        alpha = jnp.exp(m_prev - m_next)
        p = jnp.exp(s - m_next[:, 0:1])
        l_buf[c] = alpha * l_buf[c] + jnp.sum(p, axis=1, keepdims=True)
        acc_buf[c] = acc_buf[c] * alpha[:, 0:1] + weighted_values(p.astype(BF16))
        m_buf[c] = m_next

    @pl.when(g == 0)
    def _():
        q = q_ref[0]
        mask = head_mask((n_rows, hd))
        qrep = jnp.concatenate(
            [jnp.broadcast_to(q[t:t + 1, :], (n_heads, hd)) for t in range(dec_seq)], axis=0)
        qrows = jnp.where(mask, qrep, 0.0).astype(BF16)
        qrows_buf[...] = qrows
        m_buf[...] = jnp.full_like(m_buf, MASK_VALUE)
        l_buf[...] = jnp.zeros_like(l_buf)
        acc_buf[...] = jnp.zeros_like(acc_buf)
        carry[...] = jnp.zeros_like(carry)

        lfn = lfn_ref[0]
        cums = []
        run = None
        for t in range(dec_seq):
            run = lfn[t:t + 1, :] if run is None else run + lfn[t:t + 1, :]
            cums.append(run)
        r_t = lax.broadcasted_iota(jnp.int32, (n_rows, 1), 0) // n_heads
        cn = jnp.zeros((n_rows, 1), F32)
        for t in range(dec_seq):
            cn = jnp.where(r_t == t, column_of(cums[t]), cn)
        cn_col[...] = cn
        qrows_f = qrows.astype(F32)
        kn = kn_ref[0].astype(BF16).astype(F32)
        vn = vn_ref[0].astype(BF16).astype(F32)
        lane = lax.broadcasted_iota(jnp.int32, (n_rows, LANES), 1)
        s_new = jnp.full((n_rows, LANES), MASK_VALUE, F32)
        for s in range(dec_seq):
            qk = jnp.sum(qrows_f * kn[s:s + 1, :], axis=1, keepdims=True)
            s_new = jnp.where((lane == s) & (r_t >= s), qk + cn - column_of(cums[s]), s_new)

        def new_values(p):
            p = p.astype(F32)
            return sum(p[:, s:s + 1] * vn[s:s + 1, :] for s in range(dec_seq))

        online_update(0, s_new, new_values)

    sl = sl_ref[...]
    biases = [None] * pages_per_step
    run = carry[...]
    for j in reversed(range(pages_per_step)):
        lft = lf_refs[j][0]
        hi, mid, lo = _split3_bf16(lft)
        suffix = _dot(hi, sl) + _dot(mid, sl) + _dot(lo, sl) + run
        biases[j] = jnp.concatenate([suffix] * dec_seq, axis=0)
        run = run + jnp.sum(lft, axis=1, keepdims=True)
    carry[...] = run
    cn = cn_col[...]
    qrows = qrows_buf[...]
    per_chain = pages_per_step // n_chains
    for c in range(n_chains):
        pages = range(c * per_chain, (c + 1) * per_chain)
        bias = jnp.concatenate([biases[j] for j in pages], axis=1) + cn
        kt = jnp.concatenate([k_refs[j][0].astype(BF16) for j in pages], axis=1)
        vt = jnp.concatenate([v_refs[j][0].astype(BF16) for j in pages], axis=1)
        online_update(c, _dot(qrows, kt) + bias, lambda p, vt=vt: _dot_nt(p, vt))

    @pl.when(g == pl.num_programs(1) - 1)
    def _():
        m_all = m_buf[0]
        for c in range(1, n_chains):
            m_all = jnp.maximum(m_all, m_buf[c])
        l_all = jnp.zeros_like(m_all)
        acc_all = jnp.zeros((n_rows, hd), F32)
        for c in range(n_chains):
            w = jnp.exp(m_buf[c] - m_all)
            l_all = l_all + w * l_buf[c]
            acc_all = acc_all + w[:, 0:1] * acc_buf[c]
        out = jnp.where(head_mask((n_rows, hd)), acc_all / l_all[:, 0:1], 0.0)
        o_ref[0] = jnp.sum(out.reshape(dec_seq, n_heads, hd), axis=1).astype(o_ref.dtype)


def _sample_attention(page_table, cache_kt, cache_vt, cache_lft, q, k_new, v_new, lf_new,
                      *, n_heads, pages_per_step):
    dec_batch, dec_seq, hd = q.shape
    n_pages = page_table.shape[1]
    page_size = cache_kt.shape[2]
    head_dim = hd // n_heads
    n_groups = n_pages // pages_per_step
    n_rows = dec_seq * n_heads
    sl = jnp.asarray(np.tril(np.ones((page_size, page_size), np.float32), -1), BF16)

    def page_map(j):
        def index_map(b, g, pt):
            return (pt[b, (n_groups - 1 - g) * pages_per_step + j], 0, 0)
        return index_map

    seq_map = lambda b, g, pt: (b, 0, 0)
    in_specs = (
        [pl.BlockSpec((1, hd, page_size), page_map(j)) for j in range(pages_per_step)]
        + [pl.BlockSpec((1, hd, page_size), page_map(j)) for j in range(pages_per_step)]
        + [pl.BlockSpec((1, n_heads, page_size), page_map(j)) for j in range(pages_per_step)]
        + [pl.BlockSpec((1, dec_seq, hd), seq_map)] * 3
        + [pl.BlockSpec((1, dec_seq, n_heads), seq_map),
           pl.BlockSpec((page_size, page_size), lambda b, g, pt: (0, 0))])
    n_chains = math.gcd(pages_per_step, SAMPLE_SOFTMAX_CHAINS)
    kern = functools.partial(_sample_attn_kernel, n_heads=n_heads, head_dim=head_dim,
                             dec_seq=dec_seq, pages_per_step=pages_per_step, page_size=page_size,
                             n_chains=n_chains)
    grid_spec = pltpu.PrefetchScalarGridSpec(
        num_scalar_prefetch=1,
        grid=(dec_batch, n_groups),
        in_specs=in_specs,
        out_specs=pl.BlockSpec((1, dec_seq, hd), seq_map),
        scratch_shapes=[
            pltpu.VMEM((n_rows, hd), BF16),
            pltpu.VMEM((n_rows, 1), F32),
            pltpu.VMEM((n_heads, LANES), F32),
            pltpu.VMEM((n_chains, n_rows, LANES), F32),
            pltpu.VMEM((n_chains, n_rows, LANES), F32),
            pltpu.VMEM((n_chains, n_rows, hd), F32),
        ])
    operands = ([cache_kt] * pages_per_step + [cache_vt] * pages_per_step
                + [cache_lft] * pages_per_step + [q, k_new, v_new, lf_new, sl])
    return pl.pallas_call(
        kern,
        grid_spec=grid_spec,
        out_shape=jax.ShapeDtypeStruct((dec_batch, dec_seq, hd), BF16),
        compiler_params=pltpu.CompilerParams(
            dimension_semantics=("arbitrary", "arbitrary"),
            vmem_limit_bytes=VMEM_LIMIT_BYTES),
        name="sample_attention",
    )(page_table, *operands)


def _pad_history(state, hist):
    keep = state.shape[1]
    if keep == hist:
        return state
    return jnp.pad(state, ((0, 0), (hist - keep, 0), (0, 0)))


def _to_time_major(a):
    b, t = a.shape[:2]
    return jnp.swapaxes(a, 0, 1).reshape((1, t * b) + a.shape[2:])


def _from_time_major(a, b):
    t = a.shape[1] // b
    return jnp.swapaxes(a.reshape((t, b) + a.shape[2:]), 0, 1)


def kernel(x_prompt, x_sample, cache_k, cache_v, cache_logf, state_lru_h, state_lru_conv,
           state_ffn_conv, page_table, norm_a_g, w_a_in, conv_a_w, conv_a_b, w_a_gate, b_a_gate,
           lru_lambda, w_a_out, norm_f_g, w_f_in, conv_f_w, conv_f_b, w_f_out, norm_kv_g, w_kv,
           b_forget, norm_b_g, w_q, w_o, norm_out_g):
    batch, seq, d_model = x_prompt.shape
    dec_batch, dec_seq, _ = x_sample.shape
    n_phys, page_size, n_heads, head_dim = cache_k.shape
    hd = n_heads * head_dim
    n_a = w_a_in.shape[0]
    n_b = w_q.shape[0]
    d_rnn = w_a_out.shape[1]
    d_ff = w_f_out.shape[1]
    assert 2 * head_dim == LANES and n_heads % 2 == 0 and n_heads <= LANES
    assert dec_batch % SUBLANES == 0 and dec_seq >= CONV_A_WIDTH - 1

    row = lambda a: a.reshape(1, -1).astype(F32)
    w_a_in_b = w_a_in.astype(BF16)
    w_a_gate_b = w_a_gate.astype(BF16)
    w_a_out_b = w_a_out.astype(BF16)
    w_f_in_b = w_f_in.astype(BF16)
    w_f_out_b = w_f_out.astype(BF16)
    w_q_b = w_q.astype(BF16)
    w_o_b = w_o.astype(BF16)
    wk = w_kv[:, :hd].astype(BF16)
    wv = w_kv[:, hd:2 * hd].astype(BF16)
    wf = jnp.pad(w_kv[:, 2 * hd:], ((0, 0), (0, LANES - n_heads))).astype(BF16)
    bf = jnp.pad(b_forget.astype(F32), (0, LANES - n_heads)).reshape(1, LANES)

    def trunk(x2d, lru_h0, lru_conv0, ffn_conv0, *, groups, stride, tiles, with_aug,
              attention_fn):
        lru_h_new, lru_conv_new, ffn_conv_new = [], [], []
        layer = 0
        hist_a = _round_up((CONV_A_WIDTH - 1) * stride, SUBLANES)
        hist_f = _round_up((CONV_F_WIDTH - 1) * stride, SUBLANES)

        def ffn(x2d, layer, **kw):
            return _ffn_layer(x2d, _pad_history(ffn_conv0[layer], hist_f), row(norm_f_g[layer]),
                              w_f_in_b, conv_f_w[layer], row(conv_f_b[layer]),
                              w_f_out_b, layer=layer, groups=groups, stride=stride,
                              tile=tiles["ffn"], **kw)

        for i in range(n_a):
            x2d, c_new, h_new = _rglru_layer(
                x2d, _pad_history(lru_conv0[i], hist_a), lru_h0[i], row(norm_a_g[i]), w_a_in_b[i],
                conv_a_w[i], row(conv_a_b[i]), w_a_gate_b[i], b_a_gate[i][:, None, :],
                row(lru_lambda[i]), w_a_out_b[i], groups=groups, stride=stride,
                tile=tiles["rglru"])
            x2d, f_new = ffn(x2d, layer)
            lru_h_new.append(h_new)
            lru_conv_new.append(c_new)
            ffn_conv_new.append(f_new)
            layer += 1
        kvq = _kvq_layer(x2d, row(norm_kv_g), row(norm_b_g[0]), wk, wv, wf, bf, w_q_b[0],
                         n_heads=n_heads, groups=groups, tile=tiles["kvq"],
                         with_aug=with_aug)
        k_new, v_new, lf_new = kvq[:3]
        for j in range(n_b):
            assert j == 0, "one attention layer per shared K/V projection is supported"
            o2d = attention_fn(kvq)
            x2d, f_new = ffn(x2d, layer, pre=(o2d, w_o_b[j]),
                             final_gain=row(norm_out_g) if j == n_b - 1 else None)
            ffn_conv_new.append(f_new)
            layer += 1
        return x2d, k_new, v_new, lf_new, lru_h_new, lru_conv_new, ffn_conv_new

    zeros = lambda *s: jnp.zeros(s, F32)
    tiles = {"rglru": _pick_tile(seq, 256), "ffn": _pick_tile(seq, 512),
             "kvq": _pick_tile(seq, 512)}

    def prompt_attention_fn(kvq):
        q2t, k2, vtb = kvq[3:]
        return _prompt_attention(q2t, k2, vtb, n_heads=n_heads, tq=tiles["kvq"], tk=tiles["kvq"])

    (y, k_p, v_p, lf_p, h_p, c_p, f_p) = trunk(
        x_prompt.reshape(batch * seq, d_model),
        zeros(n_a, batch, 1, d_rnn), zeros(n_a, batch, CONV_A_WIDTH - 1, d_rnn),
        zeros(n_a + n_b, batch, CONV_F_WIDTH - 1, d_ff),
        groups=batch, stride=1, tiles=tiles, with_aug=True, attention_fn=prompt_attention_fn)
    y_prompt = y.reshape(batch, seq, d_model)
    k_prompt = k_p.reshape(batch, n_heads, head_dim, seq).transpose(0, 3, 1, 2)
    v_prompt = v_p.reshape(batch, n_heads, head_dim, seq).transpose(0, 3, 1, 2)
    logf_prompt = lf_p.transpose(0, 2, 1)
    lru_h_prompt = jnp.stack([h.reshape(batch, d_rnn) for h in h_p], axis=0)
    lru_conv_prompt = jnp.stack(c_p, axis=0)
    ffn_conv_prompt = jnp.stack(f_p, axis=0)

    cache_kt = cache_k.transpose(0, 2, 3, 1).reshape(n_phys, hd, page_size)
    cache_vt = cache_v.transpose(0, 2, 3, 1).reshape(n_phys, hd, page_size)
    cache_lft = jnp.swapaxes(cache_logf, 1, 2)
    n_pages = page_table.shape[1]

    def sample_attention_fn(kvq):
        k_t, v_t, lf_t, q_t = kvq
        bm = lambda a: _from_time_major(a[None], dec_batch)
        o = _sample_attention(page_table, cache_kt, cache_vt, cache_lft, bm(q_t), bm(k_t),
                              bm(v_t), bm(lf_t), n_heads=n_heads,
                              pages_per_step=_pick_tile(n_pages, SAMPLE_PAGES_PER_STEP))
        return _to_time_major(o)[0]

    n_rows = dec_batch * dec_seq
    tiles = {"rglru": n_rows, "ffn": n_rows, "kvq": n_rows}
    (y, k_s, v_s, lf_s, h_s, c_s, f_s) = trunk(
        _to_time_major(x_sample)[0],
        state_lru_h[:, None], jnp.stack([_to_time_major(s) for s in state_lru_conv]),
        jnp.stack([_to_time_major(s) for s in state_ffn_conv]),
        groups=1, stride=dec_batch, tiles=tiles, with_aug=False,
        attention_fn=sample_attention_fn)
    bm = lambda a: _from_time_major(a[None], dec_batch)
    y_sample = bm(y)
    k_sample = bm(k_s).reshape(dec_batch, dec_seq, n_heads, head_dim)
    v_sample = bm(v_s).reshape(dec_batch, dec_seq, n_heads, head_dim)
    logf_sample = bm(lf_s)
    lru_h_sample = jnp.stack([h[0] for h in h_s], axis=0)
    lru_conv_sample = jnp.stack([_from_time_major(c, dec_batch) for c in c_s], axis=0)
    ffn_conv_sample = jnp.stack([_from_time_major(f, dec_batch) for f in f_s], axis=0)

    return (y_prompt, y_sample, k_prompt, v_prompt, logf_prompt, lru_h_prompt, lru_conv_prompt,
            ffn_conv_prompt, k_sample, v_sample, logf_sample, lru_h_sample, lru_conv_sample,
            ffn_conv_sample)
```

```python
import functools
import math

import numpy as np
import jax
import jax.numpy as jnp
from jax import lax
from jax.experimental import pallas as pl
from jax.experimental.pallas import tpu as pltpu

F32 = jnp.float32
BF16 = jnp.bfloat16

LANES = 128
SUBLANES = 8
VMEM_LIMIT_BYTES = 56 * 1024 * 1024

RMS_EPS = 1e-6
LRU_C = 8.0
CONV_A_WIDTH = 4
CONV_F_WIDTH = 3
MASK_VALUE = -1e30
LOG2_E = 1.4426950408889634

AUG_LANES_PER_HEAD = 8
SUM_ROWS = 16
SAMPLE_PAGES_PER_STEP = 16
SAMPLE_SOFTMAX_CHAINS = 4


def _round_up(x, m):
    return (x + m - 1) // m * m


def _pick_tile(n, target):
    t = min(n, target)
    while n % t:
        t //= 2
    return t


def _rms_normalize(x):
    return x * lax.rsqrt(jnp.mean(x * x, axis=-1, keepdims=True) + RMS_EPS)


def _split3_bf16(x):
    hi = x.astype(BF16)
    r1 = x - hi.astype(F32)
    mid = r1.astype(BF16)
    lo = (r1 - mid.astype(F32)).astype(BF16)
    return hi, mid, lo


def _dot(a, b):
    return jnp.dot(a, b, preferred_element_type=F32)


def _dot_nt(a, b):
    return lax.dot_general(a, b, (((1,), (1,)), ((), ())), preferred_element_type=F32)


def _shifted_conv(buf_ref, w, b, *, width, hist, tile, stride):
    acc = None
    for j in range(width):
        back = (width - 1 - j) * stride
        term = buf_ref[hist - back:hist - back + tile, :] * w[j:j + 1, :]
        acc = term if acc is None else acc + term
    return acc + b


def _rglru_kernel(x_ref, conv0_ref, h0_ref, g_ref, win_ref, cw_ref, cb_ref, wg_ref, bg_ref,
                  lam_ref, wout_ref, perm_ref, y_ref, convn_ref, hn_ref, rec_buf, a_buf, b_buf,
                  h_carry, tail_buf, *, stride, tile):
    t = pl.program_id(1)
    d_rnn = rec_buf.shape[1]
    n_blocks, lru_block, _ = wg_ref.shape
    interleaved = stride == 1
    step_rows = SUBLANES if interleaved else stride
    hist = (CONV_A_WIDTH - 1) * step_rows
    seg = tile // SUBLANES
    sub = lax.broadcasted_iota(jnp.int32, (SUBLANES, d_rnn), 0)

    @pl.when(t == 0)
    def _():
        h_carry[...] = h0_ref[0]
        if interleaved:
            tail_buf[...] = conv0_ref[0]
        else:
            rec_buf[0:hist, :] = conv0_ref[0]

    x = x_ref[...]
    hn = (_rms_normalize(x) * g_ref[...]).astype(BF16)
    if interleaved:
        hn = _dot(perm_ref[0], hn).astype(BF16)
    else:
        @pl.when(t != 0)
        def _():
            rec_buf[0:hist, :] = rec_buf[tile:tile + hist, :]

    proj = _dot(hn, win_ref[...])
    rec_buf[hist:hist + tile, :] = proj[:, d_rnn:]
    if interleaved:
        for m in range(1, CONV_A_WIDTH):
            src = hist + (seg - m) * SUBLANES
            block = pltpu.roll(rec_buf[src:src + SUBLANES, :], 1, 0)
            prev = tail_buf[SUBLANES - m:SUBLANES - m + 1, :]
            dst = (CONV_A_WIDTH - 1 - m) * SUBLANES
            rec_buf[dst:dst + SUBLANES, :] = jnp.where(sub == 0, prev, block)
        for m in range(1, CONV_A_WIDTH):
            last = hist + (seg - m) * SUBLANES + SUBLANES - 1
            tail_buf[SUBLANES - m:SUBLANES - m + 1, :] = rec_buf[last:last + 1, :]
    xc = _shifted_conv(rec_buf, cw_ref[...], cb_ref[...], width=CONV_A_WIDTH, hist=hist,
                       tile=tile, stride=step_rows)
    xcb = xc.astype(BF16)
    log_sig_lam = jax.nn.log_sigmoid(lam_ref[...])
    for n in range(n_blocks):
        cols = slice(n * lru_block, (n + 1) * lru_block)
        gates = jax.nn.sigmoid(_dot(xcb[:, cols], wg_ref[n]) + bg_ref[n])
        r = gates[:, :lru_block]
        ig = gates[:, lru_block:]
        log_a = LRU_C * r * log_sig_lam[:, cols]
        a = jnp.exp(log_a)
        a_buf[:, cols] = a
        one_minus_a2 = -jnp.tanh(log_a) * (a * a + 1.0)
        b_buf[:, cols] = jnp.sqrt(one_minus_a2) * ig * xc[:, cols]

    if interleaved:
        h_loc = jnp.zeros((SUBLANES, d_rnn), F32)
        decay = jnp.ones((SUBLANES, d_rnn), F32)
        for j in range(seg):
            rows = slice(j * SUBLANES, (j + 1) * SUBLANES)
            a_j = a_buf[rows, :]
            h_loc = a_j * h_loc + b_buf[rows, :]
            decay = a_j * decay
            b_buf[rows, :] = h_loc
            a_buf[rows, :] = decay
        carry = h_carry[...]
        carry_in = jnp.zeros((SUBLANES, d_rnn), F32)
        for s in range(SUBLANES):
            carry_in = jnp.where(sub == s, carry, carry_in)
            carry = h_loc[s:s + 1, :] + decay[s:s + 1, :] * carry
        h_last = carry
        hs = (b_buf[...].reshape(seg, SUBLANES, d_rnn)
              + a_buf[...].reshape(seg, SUBLANES, d_rnn) * carry_in[None]).reshape(tile, d_rnn)
        convn_ref[0] = tail_buf[SUBLANES - (CONV_A_WIDTH - 1):SUBLANES, :]
    else:
        h_last = h_carry[...]
        for k in range(tile // stride):
            rows = slice(k * stride, (k + 1) * stride)
            h_last = a_buf[rows, :] * h_last + b_buf[rows, :]
            b_buf[rows, :] = h_last
        hs = b_buf[...]
        convn_ref[0] = rec_buf[tile:tile + hist, :]
    h_carry[...] = h_last
    hn_ref[0] = h_last

    gate_branch = proj[:, :d_rnn]
    yv = (hs * jax.nn.gelu(gate_branch, approximate=True)).astype(BF16)
    if interleaved:
        yv = _dot(perm_ref[1], yv).astype(BF16)
    y_ref[...] = x + _dot(yv, wout_ref[...])


def _rglru_layer(x2d, conv0, h0, g, w_in, conv_w, conv_b, w_gate, b_gate, lam, w_out,
                 *, groups, stride, tile):
    rows, d_model = x2d.shape
    d_rnn = w_out.shape[0]
    n_tiles = rows // groups // tile
    hist = conv0.shape[1]
    keep = (CONV_A_WIDTH - 1) * stride
    step_rows = SUBLANES if stride == 1 else stride
    const2 = lambda b, t: (0, 0)
    const3 = lambda b, t: (0, 0, 0)
    seg = tile // SUBLANES
    order = np.arange(tile).reshape(SUBLANES, seg).T.reshape(-1)
    gather = np.zeros((tile, tile), np.float32)
    gather[np.arange(tile), order] = 1.0
    perm = jnp.asarray(np.stack([gather, gather.T]), BF16)
    kern = functools.partial(_rglru_kernel, stride=stride, tile=tile)
    return pl.pallas_call(
        kern,
        grid=(groups, n_tiles),
        in_specs=[
            pl.BlockSpec((tile, d_model), lambda b, t: (b * n_tiles + t, 0)),
            pl.BlockSpec((1, hist, d_rnn), lambda b, t: (b, 0, 0)),
            pl.BlockSpec((1, stride, d_rnn), lambda b, t: (b, 0, 0)),
            pl.BlockSpec((1, d_model), const2),
            pl.BlockSpec(w_in.shape, const2),
            pl.BlockSpec(conv_w.shape, const2),
            pl.BlockSpec((1, d_rnn), const2),
            pl.BlockSpec(w_gate.shape, const3),
            pl.BlockSpec(b_gate.shape, const3),
            pl.BlockSpec((1, d_rnn), const2),
            pl.BlockSpec(w_out.shape, const2),
            pl.BlockSpec(perm.shape, const3),
        ],
        out_specs=[
            pl.BlockSpec((tile, d_model), lambda b, t: (b * n_tiles + t, 0)),
            pl.BlockSpec((1, keep, d_rnn), lambda b, t: (b, 0, 0)),
            pl.BlockSpec((1, stride, d_rnn), lambda b, t: (b, 0, 0)),
        ],
        out_shape=[
            jax.ShapeDtypeStruct((rows, d_model), F32),
            jax.ShapeDtypeStruct((groups, keep, d_rnn), F32),
            jax.ShapeDtypeStruct((groups, stride, d_rnn), F32),
        ],
        scratch_shapes=[
            pltpu.VMEM(((CONV_A_WIDTH - 1) * step_rows + tile, d_rnn), F32),
            pltpu.VMEM((tile, d_rnn), F32),
            pltpu.VMEM((tile, d_rnn), F32),
            pltpu.VMEM((stride, d_rnn), F32),
            pltpu.VMEM((SUBLANES, d_rnn), F32),
        ],
        compiler_params=pltpu.CompilerParams(
            dimension_semantics=("arbitrary", "arbitrary"),
            vmem_limit_bytes=VMEM_LIMIT_BYTES),
        name="rglru_layer",
    )(x2d, conv0, h0, g, w_in, conv_w, conv_b, w_gate, b_gate, lam, w_out, perm)


def _ffn_kernel(*refs, has_pre, has_final_norm, stride, hist, tile, ff_chunk):
    refs = list(refs)
    x_ref = refs.pop(0)
    if has_pre:
        o_ref = refs.pop(0)
        wo_ref = refs.pop(0)
    gf_ref, win_ref, cw_ref, cb_ref, wout_ref, g0_ref = refs[:6]
    refs = refs[6:]
    if has_final_norm:
        gout_ref = refs.pop(0)
    y_ref, convn_ref, gate_buf, hist_buf = refs
    t = pl.program_id(1)
    d_ff = wout_ref.shape[0]
    keep = (CONV_F_WIDTH - 1) * stride

    x1 = x_ref[...]
    if has_pre:
        x1 = x1 + _dot(o_ref[...], wo_ref[...])
    y_ref[...] = x1
    h = (_rms_normalize(x1) * gf_ref[...]).astype(BF16)

    for c in range(d_ff // ff_chunk):
        cols = slice(c * ff_chunk, (c + 1) * ff_chunk)
        up_cols = slice(d_ff + c * ff_chunk, d_ff + (c + 1) * ff_chunk)

        @pl.when(t == 0)
        def _():
            gate_buf[0:hist, :] = g0_ref[0, :, cols]

        @pl.when(t != 0)
        def _():
            gate_buf[0:hist, :] = hist_buf[c]

        gate_buf[hist:hist + tile, :] = _dot(h, win_ref[:, cols])
        up = _dot(h, win_ref[:, up_cols])
        gate = _shifted_conv(gate_buf, cw_ref[:, cols], cb_ref[:, cols], width=CONV_F_WIDTH,
                             hist=hist, tile=tile, stride=stride)
        hist_buf[c] = gate_buf[tile:tile + hist, :]
        convn_ref[0, :, cols] = gate_buf[hist + tile - keep:hist + tile, :]
        yv = (jax.nn.gelu(gate, approximate=True) * up).astype(BF16)
        y_ref[...] += _dot(yv, wout_ref[cols, :])

    if has_final_norm:
        y_ref[...] = _rms_normalize(y_ref[...]) * gout_ref[...]


def _ffn_layer(x2d, g0, gf, w_in, conv_w, conv_b, w_out, *, layer, groups, stride, tile,
               pre=None, final_gain=None):
    rows, d_model = x2d.shape
    d_ff = w_out.shape[1]
    ff_chunk = _pick_tile(d_ff, 1024)
    n_chunks = d_ff // ff_chunk
    n_tiles = rows // groups // tile
    hist = g0.shape[1]
    keep = (CONV_F_WIDTH - 1) * stride
    const2 = lambda b, t: (0, 0)
    row_map = lambda b, t: (b * n_tiles + t, 0)
    resident = functools.partial(pl.BlockSpec, index_map=const2, pipeline_mode=pl.Buffered(1))
    of_layer = lambda a: pl.BlockSpec((None,) + a.shape[1:], lambda b, t: (layer, 0, 0),
                                      pipeline_mode=pl.Buffered(1))
    operands = [x2d]
    in_specs = [pl.BlockSpec((tile, d_model), row_map)]
    if pre is not None:
        o2d, w_o = pre
        operands += [o2d, w_o]
        in_specs += [pl.BlockSpec((tile, o2d.shape[1]), row_map), resident(w_o.shape)]
    operands += [gf, w_in, conv_w, conv_b, w_out, g0]
    in_specs += [
        resident((1, d_model)),
        of_layer(w_in),
        resident(conv_w.shape),
        resident(conv_b.shape),
        of_layer(w_out),
        pl.BlockSpec((1, hist, d_ff), lambda b, t: (b, 0, 0)),
    ]
    if final_gain is not None:
        operands.append(final_gain)
        in_specs.append(resident((1, d_model)))
    kern = functools.partial(_ffn_kernel, has_pre=pre is not None,
                             has_final_norm=final_gain is not None, stride=stride, hist=hist,
                             tile=tile, ff_chunk=ff_chunk)
    return pl.pallas_call(
        kern,
        grid=(groups, n_tiles),
        in_specs=in_specs,
        out_specs=[
            pl.BlockSpec((tile, d_model), row_map),
            pl.BlockSpec((1, keep, d_ff), lambda b, t: (b, 0, 0)),
        ],
        out_shape=[
            jax.ShapeDtypeStruct((rows, d_model), F32),
            jax.ShapeDtypeStruct((groups, keep, d_ff), F32),
        ],
        scratch_shapes=[
            pltpu.VMEM((hist + tile, ff_chunk), F32),
            pltpu.VMEM((n_chunks, hist, ff_chunk), F32),
        ],
        compiler_params=pltpu.CompilerParams(
            dimension_semantics=("arbitrary", "arbitrary"),
            vmem_limit_bytes=VMEM_LIMIT_BYTES),
        name="conv_ffn",
    )(*operands)


def _kvq_kernel(*refs, n_heads, scale, with_aug, tile):
    (x_ref, gkv_ref, gb_ref, wk_ref, wv_ref, wf_ref, bf_ref, wq_ref) = refs[:8]
    if with_aug:
        tri_ref, pq_ref, pk_ref, oneq_ref, onek_ref = refs[8:13]
        kt_ref, vt_ref, lft_ref, q2t_ref, k2_ref, vtb_ref, f_carry = refs[13:]
    else:
        k_ref, v_ref, lf_ref, q_ref = refs[8:]
    n = _rms_normalize(x_ref[...])
    hk = (n * gkv_ref[...]).astype(BF16)
    hq = (n * gb_ref[...]).astype(BF16)
    k = _dot(hk, wk_ref[...])
    v = _dot(hk, wv_ref[...])
    z = _dot(hk, wf_ref[...]) + bf_ref[...]
    lane = lax.broadcasted_iota(jnp.int32, z.shape, 1)
    lf = jnp.where(lane < n_heads, jax.nn.log_sigmoid(z), 0.0)
    q = _dot(hq, wq_ref[...]) * scale
    if not with_aug:
        k_ref[...] = k
        v_ref[...] = v
        lf_ref[...] = lf[:, :n_heads]
        q_ref[...] = q
        return

    kt_ref[0] = k.T
    vt = v.T
    vt_ref[0] = vt
    vtb_ref[0, 0] = vt.astype(BF16)
    lft_ref[0] = lf.T[:n_heads, :]

    @pl.when(pl.program_id(1) == 0)
    def _():
        f_carry[...] = jnp.zeros_like(f_carry)

    sums = _dot(tri_ref[...], jnp.concatenate(_split3_bf16(lf), axis=1))
    f_cum = (sums[:, :LANES] + sums[:, LANES:2 * LANES] + sums[:, 2 * LANES:]) + f_carry[...]
    f_carry[...] = f_cum[tile - 1:tile, :]

    f_cat = jnp.concatenate(_split3_bf16(f_cum * LOG2_E), axis=1)
    aug_qt = (_dot(f_cat, pq_ref[...]) + oneq_ref[...]).T.astype(BF16)
    aug_k = (_dot(f_cat, pk_ref[...]) + onek_ref[...]).astype(BF16)
    qt = q.T.astype(BF16)
    kb = k.astype(BF16)
    for hp in range(n_heads // 2):
        src = slice(hp * LANES, (hp + 1) * LANES)
        q2t_ref[0, 0, 2 * hp * LANES:(2 * hp + 1) * LANES, :] = qt[src, :]
        q2t_ref[0, 0, (2 * hp + 1) * LANES:(2 * hp + 2) * LANES, :] = aug_qt[src, :]
        k2_ref[:, 2 * hp * LANES:(2 * hp + 1) * LANES] = kb[:, src]
        k2_ref[:, (2 * hp + 1) * LANES:(2 * hp + 2) * LANES] = aug_k[:, src]


def _aug_constants(n_heads, tile):
    n_pairs = n_heads // 2
    pq = np.zeros((3 * LANES, n_pairs * LANES), np.float32)
    pk = np.zeros((3 * LANES, n_pairs * LANES), np.float32)
    oneq = np.zeros((1, n_pairs * LANES), np.float32)
    onek = np.zeros((1, n_pairs * LANES), np.float32)
    for h in range(n_heads):
        base = (h // 2) * LANES + (h % 2) * AUG_LANES_PER_HEAD
        for piece in range(3):
            pq[piece * LANES + h, base + piece] = 1.0
            onek[0, base + piece] = 1.0
            pk[piece * LANES + h, base + 3 + piece] = -1.0
            oneq[0, base + 3 + piece] = 1.0
    tri = np.tril(np.ones((tile, tile), np.float32))
    return (jnp.asarray(tri, BF16), jnp.asarray(pq, BF16), jnp.asarray(pk, BF16),
            jnp.asarray(oneq), jnp.asarray(onek))


def _kvq_layer(x2d, gkv, gb, wk, wv, wf, bf, wq, *, n_heads, groups, tile, with_aug):
    rows, d_model = x2d.shape
    hd = wk.shape[1]
    n_tiles = rows // groups // tile
    scale = float(hd // n_heads) ** -0.5
    const2 = lambda b, t: (0, 0)
    row_map = lambda b, t: (b * n_tiles + t, 0)
    operands = [x2d, gkv, gb, wk, wv, wf, bf, wq]
    in_specs = [pl.BlockSpec((tile, d_model), row_map)]
    in_specs += [pl.BlockSpec(a.shape, const2) for a in operands[1:]]
    scratch = []
    if with_aug:
        scale *= LOG2_E
        seq = rows // groups
        col_map = lambda b, t: (b, 0, t)
        consts = _aug_constants(n_heads, tile)
        operands += list(consts)
        in_specs += [pl.BlockSpec(a.shape, const2) for a in consts]
        tile_map = lambda b, t: (b, t, 0, 0)
        out_shape = [jax.ShapeDtypeStruct((groups, hd, seq), F32),
                     jax.ShapeDtypeStruct((groups, hd, seq), F32),
                     jax.ShapeDtypeStruct((groups, n_heads, seq), F32),
                     jax.ShapeDtypeStruct((groups, n_tiles, 2 * hd, tile), BF16),
                     jax.ShapeDtypeStruct((rows, 2 * hd), BF16),
                     jax.ShapeDtypeStruct((groups, n_tiles, hd, tile), BF16)]
        out_specs = [pl.BlockSpec((1, hd, tile), col_map), pl.BlockSpec((1, hd, tile), col_map),
                     pl.BlockSpec((1, n_heads, tile), col_map),
                     pl.BlockSpec((1, 1, 2 * hd, tile), tile_map),
                     pl.BlockSpec((tile, 2 * hd), row_map),
                     pl.BlockSpec((1, 1, hd, tile), tile_map)]
        scratch = [pltpu.VMEM((1, LANES), F32)]
    else:
        out_shape = [jax.ShapeDtypeStruct((rows, hd), F32), jax.ShapeDtypeStruct((rows, hd), F32),
                     jax.ShapeDtypeStruct((rows, n_heads), F32),
                     jax.ShapeDtypeStruct((rows, hd), F32)]
        out_specs = [pl.BlockSpec((tile, hd), row_map), pl.BlockSpec((tile, hd), row_map),
                     pl.BlockSpec((tile, n_heads), row_map), pl.BlockSpec((tile, hd), row_map)]
    kern = functools.partial(_kvq_kernel, n_heads=n_heads, scale=scale, with_aug=with_aug,
                             tile=tile)
    return pl.pallas_call(
        kern,
        grid=(groups, n_tiles),
        in_specs=in_specs,
        out_specs=out_specs,
        out_shape=out_shape,
        scratch_shapes=scratch,
        compiler_params=pltpu.CompilerParams(
            dimension_semantics=("arbitrary", "arbitrary"),
            vmem_limit_bytes=VMEM_LIMIT_BYTES),
        name="kv_q_proj",
    )(*operands)


def _prompt_attn_kernel(q2t_ref, k2_ref, vt_ref, o_ref, qh_buf, s_buf, m_buf, acc_buf,
                        *, tq, tk, head_dim, n_tiles):
    def head_query(qi, j):
        q2t = q2t_ref[0, qi]
        chan = lax.broadcasted_iota(jnp.int32, q2t.shape, 0)
        aug = chan - LANES
        own = ((chan >= j * head_dim) & (chan < (j + 1) * head_dim)) | (
            (aug >= j * AUG_LANES_PER_HEAD) & (aug < (j + 1) * AUG_LANES_PER_HEAD))
        return jnp.where(own, q2t, jnp.zeros_like(q2t))

    def score(j, ki):
        k2 = k2_ref[0, pl.ds(pl.multiple_of(ki * tk, tk), tk), :]
        s_buf[j] = _dot(k2, qh_buf[j])

    def absorb(j, ki, masked):
        s = s_buf[j]
        if masked:
            causal = (lax.broadcasted_iota(jnp.int32, s.shape, 0)
                      <= lax.broadcasted_iota(jnp.int32, s.shape, 1))
            s = jnp.where(causal, s, MASK_VALUE)
        m_prev = m_buf[j]
        m_next = jnp.maximum(m_prev, jnp.max(s, axis=0, keepdims=True))
        alpha = jnp.exp2(m_prev - m_next)
        p = jnp.exp2(s - m_next)
        m_buf[j] = m_next
        vt = vt_ref[0, ki, j * head_dim:(j + 1) * head_dim, :]
        vt1 = jnp.concatenate([vt, jnp.ones((SUM_ROWS, tk), BF16)], axis=0)
        acc_buf[j] = alpha * acc_buf[j] + _dot(vt1, p.astype(BF16))

    def full_step(ki):
        score(1, ki)
        absorb(0, ki, masked=False)
        score(0, ki + 1)
        absorb(1, ki, masked=False)

    def two_steps(kp, carry):
        full_step(2 * kp)
        full_step(2 * kp + 1)
        return carry

    s_buf[0] = _dot(k2_ref[0, 0:tk, :], head_query(0, 0))

    def query_tile(qi, carry):
        for j in range(2):
            qh_buf[j] = head_query(qi, j)
        m_buf[...] = jnp.full_like(m_buf, MASK_VALUE)
        acc_buf[...] = jnp.zeros_like(acc_buf)

        lax.fori_loop(0, qi // 2, two_steps, 0)

        @pl.when(qi % 2 == 1)
        def _():
            full_step(qi - 1)

        score(1, qi)
        absorb(0, qi, masked=True)
        absorb(1, qi, masked=True)

        nxt = jnp.minimum(qi + 1, n_tiles - 1)
        s_buf[0] = _dot(k2_ref[0, 0:tk, :], head_query(nxt, 0))

        ot = jnp.concatenate(
            [acc_buf[j, 0:head_dim, :] / acc_buf[j, head_dim:head_dim + 1, :]
             for j in range(2)], axis=0)
        o_ref[pl.ds(pl.multiple_of(qi * tq, tq), tq), :] = ot.T.astype(o_ref.dtype)
        return carry

    lax.fori_loop(0, n_tiles, query_tile, 0)


def _prompt_attention(q2t, k2, vtb, *, n_heads, tq, tk):
    batch, n_tiles, hd, tile = vtb.shape
    assert tile == tq == tk, "attention tiles follow the projection kernel's row tile"
    seq = n_tiles * tile
    rows = batch * seq
    head_dim = hd // n_heads
    n_pairs = n_heads // 2
    kern = functools.partial(_prompt_attn_kernel, tq=tq, tk=tk, head_dim=head_dim,
                             n_tiles=n_tiles)
    return pl.pallas_call(
        kern,
        grid=(batch, n_pairs),
        in_specs=[
            pl.BlockSpec((1, n_tiles, 2 * LANES, tq), lambda b, hp: (b, 0, hp, 0)),
            pl.BlockSpec((1, seq, 2 * LANES), lambda b, hp: (b, 0, hp)),
            pl.BlockSpec((1, n_tiles, LANES, tk), lambda b, hp: (b, 0, hp, 0)),
        ],
        out_specs=pl.BlockSpec((seq, LANES), lambda b, hp: (b, hp)),
        out_shape=jax.ShapeDtypeStruct((rows, hd), BF16),
        scratch_shapes=[
            pltpu.VMEM((2, 2 * LANES, tq), BF16),
            pltpu.VMEM((2, tk, tq), F32),
            pltpu.VMEM((2, 1, tq), F32),
            pltpu.VMEM((2, head_dim + SUM_ROWS, tq), F32),
        ],
        compiler_params=pltpu.CompilerParams(
            dimension_semantics=("arbitrary", "arbitrary"),
            vmem_limit_bytes=VMEM_LIMIT_BYTES),
        name="prompt_attention",
    )(q2t, k2.reshape(batch, seq, 2 * hd), vtb)


def _sample_attn_kernel(pt_ref, *refs, n_heads, head_dim, dec_seq, pages_per_step, page_size,
                        n_chains):
    del pt_ref
    n_in = 3 * pages_per_step
    k_refs = refs[0:pages_per_step]
    v_refs = refs[pages_per_step:2 * pages_per_step]
    lf_refs = refs[2 * pages_per_step:n_in]
    q_ref, kn_ref, vn_ref, lfn_ref, sl_ref = refs[n_in:n_in + 5]
    o_ref = refs[n_in + 5]
    qrows_buf, cn_col, carry, m_buf, l_buf, acc_buf = refs[n_in + 6:]
    g = pl.program_id(1)
    n_rows = dec_seq * n_heads
    hd = n_heads * head_dim

    def head_mask(shape):
        r = lax.broadcasted_iota(jnp.int32, shape, 0) % n_heads
        lane = lax.broadcasted_iota(jnp.int32, shape, 1)
        return (lane >= r * head_dim) & (lane < (r + 1) * head_dim)

    def column_of(rowvec):
        wide = jnp.broadcast_to(rowvec, (n_rows, rowvec.shape[1]))
        r = lax.broadcasted_iota(jnp.int32, wide.shape, 0) % n_heads
        lane = lax.broadcasted_iota(jnp.int32, wide.shape, 1)
        return jnp.sum(jnp.where(lane == r, wide, 0.0), axis=1, keepdims=True)

    def online_update(c, s, weighted_values):
        m_prev = m_buf[c]
        m_next = jnp.maximum(m_prev, jnp.max(s, axis=1, keepdims=True))
        alpha = jnp.exp(m_prev - m_next)
        p = jnp.exp(s - m_next[:, 0:1])
        l_buf[c] = alpha * l_buf[c] + jnp.sum(p, axis=1, keepdims=True)
        acc_buf[c] = acc_buf[c] * alpha[:, 0:1] + weighted_values(p.astype(BF16))
        m_buf[c] = m_next

    @pl.when(g == 0)
    def _():
        q = q_ref[0]
        mask = head_mask((n_rows, hd))
        qrep = jnp.concatenate(
            [jnp.broadcast_to(q[t:t + 1, :], (n_heads, hd)) for t in range(dec_seq)], axis=0)
        qrows = jnp.where(mask, qrep, 0.0).astype(BF16)
        qrows_buf[...] = qrows
        m_buf[...] = jnp.full_like(m_buf, MASK_VALUE)
        l_buf[...] = jnp.zeros_like(l_buf)
        acc_buf[...] = jnp.zeros_like(acc_buf)
        carry[...] = jnp.zeros_like(carry)

        lfn = lfn_ref[0]
        cums = []
        run = None
        for t in range(dec_seq):
            run = lfn[t:t + 1, :] if run is None else run + lfn[t:t + 1, :]
            cums.append(run)
        r_t = lax.broadcasted_iota(jnp.int32, (n_rows, 1), 0) // n_heads
        cn = jnp.zeros((n_rows, 1), F32)
        for t in range(dec_seq):
            cn = jnp.where(r_t == t, column_of(cums[t]), cn)
        cn_col[...] = cn
        qrows_f = qrows.astype(F32)
        kn = kn_ref[0].astype(BF16).astype(F32)
        vn = vn_ref[0].astype(BF16).astype(F32)
        lane = lax.broadcasted_iota(jnp.int32, (n_rows, LANES), 1)
        s_new = jnp.full((n_rows, LANES), MASK_VALUE, F32)
        for s in range(dec_seq):
            qk = jnp.sum(qrows_f * kn[s:s + 1, :], axis=1, keepdims=True)
            s_new = jnp.where((lane == s) & (r_t >= s), qk + cn - column_of(cums[s]), s_new)

        def new_values(p):
            p = p.astype(F32)
            return sum(p[:, s:s + 1] * vn[s:s + 1, :] for s in range(dec_seq))

        online_update(0, s_new, new_values)

    sl = sl_ref[...]
    biases = [None] * pages_per_step
    run = carry[...]
    for j in reversed(range(pages_per_step)):
        lft = lf_refs[j][0]
        hi, mid, lo = _split3_bf16(lft)
        suffix = _dot(hi, sl) + _dot(mid, sl) + _dot(lo, sl) + run
        biases[j] = jnp.concatenate([suffix] * dec_seq, axis=0)
        run = run + jnp.sum(lft, axis=1, keepdims=True)
    carry[...] = run
    cn = cn_col[...]
    qrows = qrows_buf[...]
    per_chain = pages_per_step // n_chains
    chain_pages = [range(c * per_chain, (c + 1) * per_chain) for c in range(n_chains)]
    scores = []
    for pages in chain_pages:
        bias = jnp.concatenate([biases[j] for j in pages], axis=1) + cn
        kt = jnp.concatenate([k_refs[j][0].astype(BF16) for j in pages], axis=1)
        scores.append(_dot(qrows, kt) + bias)
    for c, pages in enumerate(chain_pages):
        vt = jnp.concatenate([v_refs[j][0].astype(BF16) for j in pages], axis=1)
        online_update(c, scores[c], lambda p, vt=vt: _dot_nt(p, vt))

    @pl.when(g == pl.num_programs(1) - 1)
    def _():
        m_all = m_buf[0]
        for c in range(1, n_chains):
            m_all = jnp.maximum(m_all, m_buf[c])
        l_all = jnp.zeros_like(m_all)
        acc_all = jnp.zeros((n_rows, hd), F32)
        for c in range(n_chains):
            w = jnp.exp(m_buf[c] - m_all)
            l_all = l_all + w * l_buf[c]
            acc_all = acc_all + w[:, 0:1] * acc_buf[c]
        out = jnp.where(head_mask((n_rows, hd)), acc_all / l_all[:, 0:1], 0.0)
        o_ref[0] = jnp.sum(out.reshape(dec_seq, n_heads, hd), axis=1).astype(o_ref.dtype)


def _sample_attention(page_table, cache_kt, cache_vt, cache_lft, q, k_new, v_new, lf_new,
                      *, n_heads, pages_per_step):
    dec_batch, dec_seq, hd = q.shape
    n_pages = page_table.shape[1]
    page_size = cache_kt.shape[2]
    head_dim = hd // n_heads
    n_groups = n_pages // pages_per_step
    n_rows = dec_seq * n_heads
    sl = jnp.asarray(np.tril(np.ones((page_size, page_size), np.float32), -1), BF16)

    def page_map(j):
        def index_map(b, g, pt):
            return (pt[b, (n_groups - 1 - g) * pages_per_step + j], 0, 0)
        return index_map

    seq_map = lambda b, g, pt: (b, 0, 0)
    in_specs = (
        [pl.BlockSpec((1, hd, page_size), page_map(j)) for j in range(pages_per_step)]
        + [pl.BlockSpec((1, hd, page_size), page_map(j)) for j in range(pages_per_step)]
        + [pl.BlockSpec((1, n_heads, page_size), page_map(j)) for j in range(pages_per_step)]
        + [pl.BlockSpec((1, dec_seq, hd), seq_map)] * 3
        + [pl.BlockSpec((1, dec_seq, n_heads), seq_map),
           pl.BlockSpec((page_size, page_size), lambda b, g, pt: (0, 0))])
    n_chains = math.gcd(pages_per_step, SAMPLE_SOFTMAX_CHAINS)
    kern = functools.partial(_sample_attn_kernel, n_heads=n_heads, head_dim=head_dim,
                             dec_seq=dec_seq, pages_per_step=pages_per_step, page_size=page_size,
                             n_chains=n_chains)
    grid_spec = pltpu.PrefetchScalarGridSpec(
        num_scalar_prefetch=1,
        grid=(dec_batch, n_groups),
        in_specs=in_specs,
        out_specs=pl.BlockSpec((1, dec_seq, hd), seq_map),
        scratch_shapes=[
            pltpu.VMEM((n_rows, hd), BF16),
            pltpu.VMEM((n_rows, 1), F32),
            pltpu.VMEM((n_heads, LANES), F32),
            pltpu.VMEM((n_chains, n_rows, LANES), F32),
            pltpu.VMEM((n_chains, n_rows, LANES), F32),
            pltpu.VMEM((n_chains, n_rows, hd), F32),
        ])
    operands = ([cache_kt] * pages_per_step + [cache_vt] * pages_per_step
                + [cache_lft] * pages_per_step + [q, k_new, v_new, lf_new, sl])
    return pl.pallas_call(
        kern,
        grid_spec=grid_spec,
        out_shape=jax.ShapeDtypeStruct((dec_batch, dec_seq, hd), BF16),
        compiler_params=pltpu.CompilerParams(
            dimension_semantics=("arbitrary", "arbitrary"),
            vmem_limit_bytes=VMEM_LIMIT_BYTES),
        name="sample_attention",
    )(page_table, *operands)


def _pad_history(state, hist):
    keep = state.shape[1]
    if keep == hist:
        return state
    return jnp.pad(state, ((0, 0), (hist - keep, 0), (0, 0)))


def _to_time_major(a):
    b, t = a.shape[:2]
    return jnp.swapaxes(a, 0, 1).reshape((1, t * b) + a.shape[2:])


def _from_time_major(a, b):
    t = a.shape[1] // b
    return jnp.swapaxes(a.reshape((t, b) + a.shape[2:]), 0, 1)


def kernel(x_prompt, x_sample, cache_k, cache_v, cache_logf, state_lru_h, state_lru_conv,
           state_ffn_conv, page_table, norm_a_g, w_a_in, conv_a_w, conv_a_b, w_a_gate, b_a_gate,
           lru_lambda, w_a_out, norm_f_g, w_f_in, conv_f_w, conv_f_b, w_f_out, norm_kv_g, w_kv,
           b_forget, norm_b_g, w_q, w_o, norm_out_g):
    batch, seq, d_model = x_prompt.shape
    dec_batch, dec_seq, _ = x_sample.shape
    n_phys, page_size, n_heads, head_dim = cache_k.shape
    hd = n_heads * head_dim
    n_a = w_a_in.shape[0]
    n_b = w_q.shape[0]
    d_rnn = w_a_out.shape[1]
    d_ff = w_f_out.shape[1]
    assert 2 * head_dim == LANES and n_heads % 2 == 0 and n_heads <= LANES
    assert dec_batch % SUBLANES == 0 and dec_seq >= CONV_A_WIDTH - 1

    row = lambda a: a.reshape(1, -1).astype(F32)
    w_a_in_b = w_a_in.astype(BF16)
    w_a_gate_b = w_a_gate.astype(BF16)
    w_a_out_b = w_a_out.astype(BF16)
    w_f_in_b = w_f_in.astype(BF16)
    w_f_out_b = w_f_out.astype(BF16)
    w_q_b = w_q.astype(BF16)
    w_o_b = w_o.astype(BF16)
    wk = w_kv[:, :hd].astype(BF16)
    wv = w_kv[:, hd:2 * hd].astype(BF16)
    wf = jnp.pad(w_kv[:, 2 * hd:], ((0, 0), (0, LANES - n_heads))).astype(BF16)
    bf = jnp.pad(b_forget.astype(F32), (0, LANES - n_heads)).reshape(1, LANES)

    def trunk(x2d, lru_h0, lru_conv0, ffn_conv0, *, groups, stride, tiles, with_aug,
              attention_fn):
        lru_h_new, lru_conv_new, ffn_conv_new = [], [], []
        layer = 0
        hist_a = _round_up((CONV_A_WIDTH - 1) * stride, SUBLANES)
        hist_f = _round_up((CONV_F_WIDTH - 1) * stride, SUBLANES)

        def ffn(x2d, layer, **kw):
            return _ffn_layer(x2d, _pad_history(ffn_conv0[layer], hist_f), row(norm_f_g[layer]),
                              w_f_in_b, conv_f_w[layer], row(conv_f_b[layer]),
                              w_f_out_b, layer=layer, groups=groups, stride=stride,
                              tile=tiles["ffn"], **kw)

        for i in range(n_a):
            x2d, c_new, h_new = _rglru_layer(
                x2d, _pad_history(lru_conv0[i], hist_a), lru_h0[i], row(norm_a_g[i]), w_a_in_b[i],
                conv_a_w[i], row(conv_a_b[i]), w_a_gate_b[i], b_a_gate[i][:, None, :],
                row(lru_lambda[i]), w_a_out_b[i], groups=groups, stride=stride,
                tile=tiles["rglru"])
            x2d, f_new = ffn(x2d, layer)
            lru_h_new.append(h_new)
            lru_conv_new.append(c_new)
            ffn_conv_new.append(f_new)
            layer += 1
        kvq = _kvq_layer(x2d, row(norm_kv_g), row(norm_b_g[0]), wk, wv, wf, bf, w_q_b[0],
                         n_heads=n_heads, groups=groups, tile=tiles["kvq"],
                         with_aug=with_aug)
        k_new, v_new, lf_new = kvq[:3]
        for j in range(n_b):
            assert j == 0, "one attention layer per shared K/V projection is supported"
            o2d = attention_fn(kvq)
            x2d, f_new = ffn(x2d, layer, pre=(o2d, w_o_b[j]),
                             final_gain=row(norm_out_g) if j == n_b - 1 else None)
            ffn_conv_new.append(f_new)
            layer += 1
        return x2d, k_new, v_new, lf_new, lru_h_new, lru_conv_new, ffn_conv_new

    zeros = lambda *s: jnp.zeros(s, F32)
    tiles = {"rglru": _pick_tile(seq, 256), "ffn": _pick_tile(seq, 512),
             "kvq": _pick_tile(seq, 512)}

    def prompt_attention_fn(kvq):
        q2t, k2, vtb = kvq[3:]
        return _prompt_attention(q2t, k2, vtb, n_heads=n_heads, tq=tiles["kvq"], tk=tiles["kvq"])

    (y, k_p, v_p, lf_p, h_p, c_p, f_p) = trunk(
        x_prompt.reshape(batch * seq, d_model),
        zeros(n_a, batch, 1, d_rnn), zeros(n_a, batch, CONV_A_WIDTH - 1, d_rnn),
        zeros(n_a + n_b, batch, CONV_F_WIDTH - 1, d_ff),
        groups=batch, stride=1, tiles=tiles, with_aug=True, attention_fn=prompt_attention_fn)
    y_prompt = y.reshape(batch, seq, d_model)
    k_prompt = k_p.reshape(batch, n_heads, head_dim, seq).transpose(0, 3, 1, 2)
    v_prompt = v_p.reshape(batch, n_heads, head_dim, seq).transpose(0, 3, 1, 2)
    logf_prompt = lf_p.transpose(0, 2, 1)
    lru_h_prompt = jnp.stack([h.reshape(batch, d_rnn) for h in h_p], axis=0)
    lru_conv_prompt = jnp.stack(c_p, axis=0)
    ffn_conv_prompt = jnp.stack(f_p, axis=0)

    cache_kt = cache_k.transpose(0, 2, 3, 1).reshape(n_phys, hd, page_size)
    cache_vt = cache_v.transpose(0, 2, 3, 1).reshape(n_phys, hd, page_size)
    cache_lft = jnp.swapaxes(cache_logf, 1, 2)
    n_pages = page_table.shape[1]

    def sample_attention_fn(kvq):
        k_t, v_t, lf_t, q_t = kvq
        bm = lambda a: _from_time_major(a[None], dec_batch)
        o = _sample_attention(page_table, cache_kt, cache_vt, cache_lft, bm(q_t), bm(k_t),
                              bm(v_t), bm(lf_t), n_heads=n_heads,
                              pages_per_step=_pick_tile(n_pages, SAMPLE_PAGES_PER_STEP))
        return _to_time_major(o)[0]

    n_rows = dec_batch * dec_seq
    tiles = {"rglru": n_rows, "ffn": n_rows, "kvq": n_rows}
    (y, k_s, v_s, lf_s, h_s, c_s, f_s) = trunk(
        _to_time_major(x_sample)[0],
        state_lru_h[:, None], jnp.stack([_to_time_major(s) for s in state_lru_conv]),
        jnp.stack([_to_time_major(s) for s in state_ffn_conv]),
        groups=1, stride=dec_batch, tiles=tiles, with_aug=False,
        attention_fn=sample_attention_fn)
    bm = lambda a: _from_time_major(a[None], dec_batch)
    y_sample = bm(y)
    k_sample = bm(k_s).reshape(dec_batch, dec_seq, n_heads, head_dim)
    v_sample = bm(v_s).reshape(dec_batch, dec_seq, n_heads, head_dim)
    logf_sample = bm(lf_s)
    lru_h_sample = jnp.stack([h[0] for h in h_s], axis=0)
    lru_conv_sample = jnp.stack([_from_time_major(c, dec_batch) for c in c_s], axis=0)
    ffn_conv_sample = jnp.stack([_from_time_major(f, dec_batch) for f in f_s], axis=0)

    return (y_prompt, y_sample, k_prompt, v_prompt, logf_prompt, lru_h_prompt, lru_conv_prompt,
            ffn_conv_prompt, k_sample, v_sample, logf_sample, lru_h_sample, lru_conv_sample,
            ffn_conv_sample)
```

```python
import functools
import math

import numpy as np
import jax
import jax.numpy as jnp
from jax import lax
from jax.experimental import pallas as pl
from jax.experimental.pallas import tpu as pltpu

F32 = jnp.float32
BF16 = jnp.bfloat16

LANES = 128
SUBLANES = 8
VMEM_LIMIT_BYTES = 56 * 1024 * 1024

RMS_EPS = 1e-6
LRU_C = 8.0
CONV_A_WIDTH = 4
CONV_F_WIDTH = 3
MASK_VALUE = -1e30
LOG2_E = 1.4426950408889634

AUG_LANES_PER_HEAD = 8
SUM_ROWS = 16
SAMPLE_PAGES_PER_STEP = 16
SAMPLE_SOFTMAX_CHAINS = 4


def _round_up(x, m):
    return (x + m - 1) // m * m


def _pick_tile(n, target):
    t = min(n, target)
    while n % t:
        t //= 2
    return t


def _rms_normalize(x):
    return x * lax.rsqrt(jnp.mean(x * x, axis=-1, keepdims=True) + RMS_EPS)


def _split3_bf16(x):
    hi = x.astype(BF16)
    r1 = x - hi.astype(F32)
    mid = r1.astype(BF16)
    lo = (r1 - mid.astype(F32)).astype(BF16)
    return hi, mid, lo


def _dot(a, b):
    return jnp.dot(a, b, preferred_element_type=F32)


def _dot_nt(a, b):
    return lax.dot_general(a, b, (((1,), (1,)), ((), ())), preferred_element_type=F32)


def _shifted_conv(buf_ref, w, b, *, width, hist, tile, stride):
    acc = None
    for j in range(width):
        back = (width - 1 - j) * stride
        term = buf_ref[hist - back:hist - back + tile, :] * w[j:j + 1, :]
        acc = term if acc is None else acc + term
    return acc + b


def _rglru_kernel(x_ref, conv0_ref, h0_ref, g_ref, win_ref, cw_ref, cb_ref, wg_ref, bg_ref,
                  lam_ref, wout_ref, perm_ref, y_ref, convn_ref, hn_ref, rec_buf, a_buf, b_buf,
                  h_carry, tail_buf, *, stride, sub_tile, n_sub):
    t = pl.program_id(1)
    d_rnn = rec_buf.shape[2]
    n_blocks, lru_block, _ = wg_ref.shape
    interleaved = stride == 1
    assert interleaved or n_sub == 1
    step_rows = SUBLANES if interleaved else stride
    hist = (CONV_A_WIDTH - 1) * step_rows
    seg = sub_tile // SUBLANES
    sub = lax.broadcasted_iota(jnp.int32, (SUBLANES, d_rnn), 0)

    @pl.when(t == 0)
    def _():
        h_carry[...] = h0_ref[0]
        if interleaved:
            tail_buf[...] = conv0_ref[0]
        else:
            rec_buf[0, 0:hist, :] = conv0_ref[0]

    def project_in(u):
        x = x_ref[u * sub_tile:(u + 1) * sub_tile, :]
        hn = (_rms_normalize(x) * g_ref[...]).astype(BF16)
        if interleaved:
            hn = _dot(perm_ref[0], hn).astype(BF16)
        else:
            @pl.when(t != 0)
            def _():
                rec_buf[u, 0:hist, :] = rec_buf[u, sub_tile:sub_tile + hist, :]

        proj = _dot(hn, win_ref[...])
        rec_buf[u, hist:hist + sub_tile, :] = proj[:, d_rnn:]
        return proj[:, :d_rnn]

    def recur(u):
        rec = rec_buf.at[u]
        if interleaved:
            for m in range(1, CONV_A_WIDTH):
                src = hist + (seg - m) * SUBLANES
                block = pltpu.roll(rec[src:src + SUBLANES, :], 1, 0)
                prev = tail_buf[SUBLANES - m:SUBLANES - m + 1, :]
                dst = (CONV_A_WIDTH - 1 - m) * SUBLANES
                rec[dst:dst + SUBLANES, :] = jnp.where(sub == 0, prev, block)
            for m in range(1, CONV_A_WIDTH):
                last = hist + (seg - m) * SUBLANES + SUBLANES - 1
                tail_buf[SUBLANES - m:SUBLANES - m + 1, :] = rec[last:last + 1, :]
        xc = _shifted_conv(rec, cw_ref[...], cb_ref[...], width=CONV_A_WIDTH, hist=hist,
                           tile=sub_tile, stride=step_rows)
        xcb = xc.astype(BF16)
        log_sig_lam = jax.nn.log_sigmoid(lam_ref[...])
        for n in range(n_blocks):
            cols = slice(n * lru_block, (n + 1) * lru_block)
            gates = jax.nn.sigmoid(_dot(xcb[:, cols], wg_ref[n]) + bg_ref[n])
            r = gates[:, :lru_block]
            ig = gates[:, lru_block:]
            log_a = LRU_C * r * log_sig_lam[:, cols]
            a = jnp.exp(log_a)
            a_buf[u, :, cols] = a
            one_minus_a2 = -jnp.tanh(log_a) * (a * a + 1.0)
            b_buf[u, :, cols] = jnp.sqrt(one_minus_a2) * ig * xc[:, cols]

        if interleaved:
            h_loc = jnp.zeros((SUBLANES, d_rnn), F32)
            decay = jnp.ones((SUBLANES, d_rnn), F32)
            for j in range(seg):
                rows = slice(j * SUBLANES, (j + 1) * SUBLANES)
                a_j = a_buf[u, rows, :]
                h_loc = a_j * h_loc + b_buf[u, rows, :]
                decay = a_j * decay
                b_buf[u, rows, :] = h_loc
                a_buf[u, rows, :] = decay
            carry = h_carry[...]
            carry_in = jnp.zeros((SUBLANES, d_rnn), F32)
            for s in range(SUBLANES):
                carry_in = jnp.where(sub == s, carry, carry_in)
                carry = h_loc[s:s + 1, :] + decay[s:s + 1, :] * carry
            h_carry[...] = carry
            return (b_buf[u].reshape(seg, SUBLANES, d_rnn)
                    + a_buf[u].reshape(seg, SUBLANES, d_rnn) * carry_in[None]
                    ).reshape(sub_tile, d_rnn)
        h_last = h_carry[...]
        for k in range(sub_tile // stride):
            rows = slice(k * stride, (k + 1) * stride)
            h_last = a_buf[u, rows, :] * h_last + b_buf[u, rows, :]
            b_buf[u, rows, :] = h_last
        h_carry[...] = h_last
        return b_buf[u]

    def project_out(u, gate_branch, hs):
        yv = (hs * jax.nn.gelu(gate_branch, approximate=True)).astype(BF16)
        if interleaved:
            yv = _dot(perm_ref[1], yv).astype(BF16)
        rows = slice(u * sub_tile, (u + 1) * sub_tile)
        y_ref[rows, :] = x_ref[rows, :] + _dot(yv, wout_ref[...])

    gate_branches = [project_in(u) for u in range(n_sub)]
    hidden = [recur(u) for u in range(n_sub)]
    for u in range(n_sub):
        project_out(u, gate_branches[u], hidden[u])

    hn_ref[0] = h_carry[...]
    if interleaved:
        convn_ref[0] = tail_buf[SUBLANES - (CONV_A_WIDTH - 1):SUBLANES, :]
    else:
        convn_ref[0] = rec_buf[0, sub_tile:sub_tile + hist, :]


def _rglru_layer(x2d, conv0, h0, g, w_in, conv_w, conv_b, w_gate, b_gate, lam, w_out,
                 *, groups, stride, tile, sub_tile):
    rows, d_model = x2d.shape
    d_rnn = w_out.shape[0]
    n_tiles = rows // groups // tile
    n_sub = tile // sub_tile
    hist = conv0.shape[1]
    keep = (CONV_A_WIDTH - 1) * stride
    step_rows = SUBLANES if stride == 1 else stride
    const2 = lambda b, t: (0, 0)
    const3 = lambda b, t: (0, 0, 0)
    seg = sub_tile // SUBLANES
    order = np.arange(sub_tile).reshape(SUBLANES, seg).T.reshape(-1)
    gather = np.zeros((sub_tile, sub_tile), np.float32)
    gather[np.arange(sub_tile), order] = 1.0
    perm = jnp.asarray(np.stack([gather, gather.T]), BF16)
    kern = functools.partial(_rglru_kernel, stride=stride, sub_tile=sub_tile, n_sub=n_sub)
    return pl.pallas_call(
        kern,
        grid=(groups, n_tiles),
        in_specs=[
            pl.BlockSpec((tile, d_model), lambda b, t: (b * n_tiles + t, 0)),
            pl.BlockSpec((1, hist, d_rnn), lambda b, t: (b, 0, 0)),
            pl.BlockSpec((1, stride, d_rnn), lambda b, t: (b, 0, 0)),
            pl.BlockSpec((1, d_model), const2),
            pl.BlockSpec(w_in.shape, const2),
            pl.BlockSpec(conv_w.shape, const2),
            pl.BlockSpec((1, d_rnn), const2),
            pl.BlockSpec(w_gate.shape, const3),
            pl.BlockSpec(b_gate.shape, const3),
            pl.BlockSpec((1, d_rnn), const2),
            pl.BlockSpec(w_out.shape, const2),
            pl.BlockSpec(perm.shape, const3),
        ],
        out_specs=[
            pl.BlockSpec((tile, d_model), lambda b, t: (b * n_tiles + t, 0)),
            pl.BlockSpec((1, keep, d_rnn), lambda b, t: (b, 0, 0)),
            pl.BlockSpec((1, stride, d_rnn), lambda b, t: (b, 0, 0)),
        ],
        out_shape=[
            jax.ShapeDtypeStruct((rows, d_model), F32),
            jax.ShapeDtypeStruct((groups, keep, d_rnn), F32),
            jax.ShapeDtypeStruct((groups, stride, d_rnn), F32),
        ],
        scratch_shapes=[
            pltpu.VMEM((n_sub, (CONV_A_WIDTH - 1) * step_rows + sub_tile, d_rnn), F32),
            pltpu.VMEM((n_sub, sub_tile, d_rnn), F32),
            pltpu.VMEM((n_sub, sub_tile, d_rnn), F32),
            pltpu.VMEM((stride, d_rnn), F32),
            pltpu.VMEM((SUBLANES, d_rnn), F32),
        ],
        compiler_params=pltpu.CompilerParams(
            dimension_semantics=("arbitrary", "arbitrary"),
            vmem_limit_bytes=VMEM_LIMIT_BYTES),
        name="rglru_layer",
    )(x2d, conv0, h0, g, w_in, conv_w, conv_b, w_gate, b_gate, lam, w_out, perm)


def _ffn_kernel(*refs, has_pre, has_final_norm, stride, hist, tile, ff_chunk):
    refs = list(refs)
    x_ref = refs.pop(0)
    if has_pre:
        o_ref = refs.pop(0)
        wo_ref = refs.pop(0)
    gf_ref, win_ref, cw_ref, cb_ref, wout_ref, g0_ref = refs[:6]
    refs = refs[6:]
    if has_final_norm:
        gout_ref = refs.pop(0)
    y_ref, convn_ref, gate_buf, hist_buf = refs
    t = pl.program_id(1)
    d_ff = wout_ref.shape[0]
    keep = (CONV_F_WIDTH - 1) * stride

    x1 = x_ref[...]
    if has_pre:
        x1 = x1 + _dot(o_ref[...], wo_ref[...])
    y_ref[...] = x1
    h = (_rms_normalize(x1) * gf_ref[...]).astype(BF16)

    for c in range(d_ff // ff_chunk):
        cols = slice(c * ff_chunk, (c + 1) * ff_chunk)
        up_cols = slice(d_ff + c * ff_chunk, d_ff + (c + 1) * ff_chunk)

        @pl.when(t == 0)
        def _():
            gate_buf[0:hist, :] = g0_ref[0, :, cols]

        @pl.when(t != 0)
        def _():
            gate_buf[0:hist, :] = hist_buf[c]

        gate_buf[hist:hist + tile, :] = _dot(h, win_ref[:, cols])
        up = _dot(h, win_ref[:, up_cols])
        gate = _shifted_conv(gate_buf, cw_ref[:, cols], cb_ref[:, cols], width=CONV_F_WIDTH,
                             hist=hist, tile=tile, stride=stride)
        hist_buf[c] = gate_buf[tile:tile + hist, :]
        convn_ref[0, :, cols] = gate_buf[hist + tile - keep:hist + tile, :]
        yv = (jax.nn.gelu(gate, approximate=True) * up).astype(BF16)
        y_ref[...] += _dot(yv, wout_ref[cols, :])

    if has_final_norm:
        y_ref[...] = _rms_normalize(y_ref[...]) * gout_ref[...]


def _ffn_layer(x2d, g0, gf, w_in, conv_w, conv_b, w_out, *, layer, groups, stride, tile,
               pre=None, final_gain=None):
    rows, d_model = x2d.shape
    d_ff = w_out.shape[1]
    ff_chunk = _pick_tile(d_ff, 1024)
    n_chunks = d_ff // ff_chunk
    n_tiles = rows // groups // tile
    hist = g0.shape[1]
    keep = (CONV_F_WIDTH - 1) * stride
    const2 = lambda b, t: (0, 0)
    row_map = lambda b, t: (b * n_tiles + t, 0)
    resident = functools.partial(pl.BlockSpec, index_map=const2, pipeline_mode=pl.Buffered(1))
    of_layer = lambda a: pl.BlockSpec((None,) + a.shape[1:], lambda b, t: (layer, 0, 0),
                                      pipeline_mode=pl.Buffered(1))
    operands = [x2d]
    in_specs = [pl.BlockSpec((tile, d_model), row_map)]
    if pre is not None:
        o2d, w_o = pre
        operands += [o2d, w_o]
        in_specs += [pl.BlockSpec((tile, o2d.shape[1]), row_map), resident(w_o.shape)]
    operands += [gf, w_in, conv_w, conv_b, w_out, g0]
    in_specs += [
        resident((1, d_model)),
        of_layer(w_in),
        resident(conv_w.shape),
        resident(conv_b.shape),
        of_layer(w_out),
        pl.BlockSpec((1, hist, d_ff), lambda b, t: (b, 0, 0)),
    ]
    if final_gain is not None:
        operands.append(final_gain)
        in_specs.append(resident((1, d_model)))
    kern = functools.partial(_ffn_kernel, has_pre=pre is not None,
                             has_final_norm=final_gain is not None, stride=stride, hist=hist,
                             tile=tile, ff_chunk=ff_chunk)
    return pl.pallas_call(
        kern,
        grid=(groups, n_tiles),
        in_specs=in_specs,
        out_specs=[
            pl.BlockSpec((tile, d_model), row_map),
            pl.BlockSpec((1, keep, d_ff), lambda b, t: (b, 0, 0)),
        ],
        out_shape=[
            jax.ShapeDtypeStruct((rows, d_model), F32),
            jax.ShapeDtypeStruct((groups, keep, d_ff), F32),
        ],
        scratch_shapes=[
            pltpu.VMEM((hist + tile, ff_chunk), F32),
            pltpu.VMEM((n_chunks, hist, ff_chunk), F32),
        ],
        compiler_params=pltpu.CompilerParams(
            dimension_semantics=("arbitrary", "arbitrary"),
            vmem_limit_bytes=VMEM_LIMIT_BYTES),
        name="conv_ffn",
    )(*operands)


def _kvq_kernel(*refs, n_heads, scale, with_aug, tile):
    (x_ref, gkv_ref, gb_ref, wk_ref, wv_ref, wf_ref, bf_ref, wq_ref) = refs[:8]
    if with_aug:
        tri_ref, pq_ref, pk_ref, oneq_ref, onek_ref = refs[8:13]
        kt_ref, vt_ref, lft_ref, q2t_ref, k2_ref, vtb_ref, f_carry = refs[13:]
    else:
        k_ref, v_ref, lf_ref, q_ref = refs[8:]
    n = _rms_normalize(x_ref[...])
    hk = (n * gkv_ref[...]).astype(BF16)
    hq = (n * gb_ref[...]).astype(BF16)
    k = _dot(hk, wk_ref[...])
    v = _dot(hk, wv_ref[...])
    z = _dot(hk, wf_ref[...]) + bf_ref[...]
    lane = lax.broadcasted_iota(jnp.int32, z.shape, 1)
    lf = jnp.where(lane < n_heads, jax.nn.log_sigmoid(z), 0.0)
    q = _dot(hq, wq_ref[...]) * scale
    if not with_aug:
        k_ref[...] = k
        v_ref[...] = v
        lf_ref[...] = lf[:, :n_heads]
        q_ref[...] = q
        return

    kt_ref[0] = k.T
    vt = v.T
    vt_ref[0] = vt
    vtb_ref[0, 0] = vt.astype(BF16)
    lft_ref[0] = lf.T[:n_heads, :]

    @pl.when(pl.program_id(1) == 0)
    def _():
        f_carry[...] = jnp.zeros_like(f_carry)

    sums = _dot(tri_ref[...], jnp.concatenate(_split3_bf16(lf), axis=1))
    f_cum = (sums[:, :LANES] + sums[:, LANES:2 * LANES] + sums[:, 2 * LANES:]) + f_carry[...]
    f_carry[...] = f_cum[tile - 1:tile, :]

    f_cat = jnp.concatenate(_split3_bf16(f_cum * LOG2_E), axis=1)
    aug_qt = (_dot(f_cat, pq_ref[...]) + oneq_ref[...]).T.astype(BF16)
    aug_k = (_dot(f_cat, pk_ref[...]) + onek_ref[...]).astype(BF16)
    qt = q.T.astype(BF16)
    kb = k.astype(BF16)
    for hp in range(n_heads // 2):
        src = slice(hp * LANES, (hp + 1) * LANES)
        q2t_ref[0, 0, 2 * hp * LANES:(2 * hp + 1) * LANES, :] = qt[src, :]
        q2t_ref[0, 0, (2 * hp + 1) * LANES:(2 * hp + 2) * LANES, :] = aug_qt[src, :]
        k2_ref[:, 2 * hp * LANES:(2 * hp + 1) * LANES] = kb[:, src]
        k2_ref[:, (2 * hp + 1) * LANES:(2 * hp + 2) * LANES] = aug_k[:, src]


def _aug_constants(n_heads, tile):
    n_pairs = n_heads // 2
    pq = np.zeros((3 * LANES, n_pairs * LANES), np.float32)
    pk = np.zeros((3 * LANES, n_pairs * LANES), np.float32)
    oneq = np.zeros((1, n_pairs * LANES), np.float32)
    onek = np.zeros((1, n_pairs * LANES), np.float32)
    for h in range(n_heads):
        base = (h // 2) * LANES + (h % 2) * AUG_LANES_PER_HEAD
        for piece in range(3):
            pq[piece * LANES + h, base + piece] = 1.0
            onek[0, base + piece] = 1.0
            pk[piece * LANES + h, base + 3 + piece] = -1.0
            oneq[0, base + 3 + piece] = 1.0
    tri = np.tril(np.ones((tile, tile), np.float32))
    return (jnp.asarray(tri, BF16), jnp.asarray(pq, BF16), jnp.asarray(pk, BF16),
            jnp.asarray(oneq), jnp.asarray(onek))


def _kvq_layer(x2d, gkv, gb, wk, wv, wf, bf, wq, *, n_heads, groups, tile, with_aug):
    rows, d_model = x2d.shape
    hd = wk.shape[1]
    n_tiles = rows // groups // tile
    scale = float(hd // n_heads) ** -0.5
    const2 = lambda b, t: (0, 0)
    row_map = lambda b, t: (b * n_tiles + t, 0)
    operands = [x2d, gkv, gb, wk, wv, wf, bf, wq]
    in_specs = [pl.BlockSpec((tile, d_model), row_map)]
    in_specs += [pl.BlockSpec(a.shape, const2) for a in operands[1:]]
    scratch = []
    if with_aug:
        scale *= LOG2_E
        seq = rows // groups
        col_map = lambda b, t: (b, 0, t)
        consts = _aug_constants(n_heads, tile)
        operands += list(consts)
        in_specs += [pl.BlockSpec(a.shape, const2) for a in consts]
        tile_map = lambda b, t: (b, t, 0, 0)
        out_shape = [jax.ShapeDtypeStruct((groups, hd, seq), F32),
                     jax.ShapeDtypeStruct((groups, hd, seq), F32),
                     jax.ShapeDtypeStruct((groups, n_heads, seq), F32),
                     jax.ShapeDtypeStruct((groups, n_tiles, 2 * hd, tile), BF16),
                     jax.ShapeDtypeStruct((rows, 2 * hd), BF16),
                     jax.ShapeDtypeStruct((groups, n_tiles, hd, tile), BF16)]
        out_specs = [pl.BlockSpec((1, hd, tile), col_map), pl.BlockSpec((1, hd, tile), col_map),
                     pl.BlockSpec((1, n_heads, tile), col_map),
                     pl.BlockSpec((1, 1, 2 * hd, tile), tile_map),
                     pl.BlockSpec((tile, 2 * hd), row_map),
                     pl.BlockSpec((1, 1, hd, tile), tile_map)]
        scratch = [pltpu.VMEM((1, LANES), F32)]
    else:
        out_shape = [jax.ShapeDtypeStruct((rows, hd), F32), jax.ShapeDtypeStruct((rows, hd), F32),
                     jax.ShapeDtypeStruct((rows, n_heads), F32),
                     jax.ShapeDtypeStruct((rows, hd), F32)]
        out_specs = [pl.BlockSpec((tile, hd), row_map), pl.BlockSpec((tile, hd), row_map),
                     pl.BlockSpec((tile, n_heads), row_map), pl.BlockSpec((tile, hd), row_map)]
    kern = functools.partial(_kvq_kernel, n_heads=n_heads, scale=scale, with_aug=with_aug,
                             tile=tile)
    return pl.pallas_call(
        kern,
        grid=(groups, n_tiles),
        in_specs=in_specs,
        out_specs=out_specs,
        out_shape=out_shape,
        scratch_shapes=scratch,
        compiler_params=pltpu.CompilerParams(
            dimension_semantics=("arbitrary", "arbitrary"),
            vmem_limit_bytes=VMEM_LIMIT_BYTES),
        name="kv_q_proj",
    )(*operands)


def _prompt_attn_kernel(q2t_ref, k2_ref, vt_ref, o_ref, qh_buf, s_buf, m_buf, acc_buf,
                        *, tq, tk, head_dim, n_tiles):
    def head_query(qi, j):
        q2t = q2t_ref[0, qi]
        chan = lax.broadcasted_iota(jnp.int32, q2t.shape, 0)
        aug = chan - LANES
        own = ((chan >= j * head_dim) & (chan < (j + 1) * head_dim)) | (
            (aug >= j * AUG_LANES_PER_HEAD) & (aug < (j + 1) * AUG_LANES_PER_HEAD))
        return jnp.where(own, q2t, jnp.zeros_like(q2t))

    def score(j, ki):
        k2 = k2_ref[0, pl.ds(pl.multiple_of(ki * tk, tk), tk), :]
        s_buf[j] = _dot(k2, qh_buf[j])

    def absorb(j, ki, masked):
        s = s_buf[j]
        if masked:
            causal = (lax.broadcasted_iota(jnp.int32, s.shape, 0)
                      <= lax.broadcasted_iota(jnp.int32, s.shape, 1))
            s = jnp.where(causal, s, MASK_VALUE)
        m_prev = m_buf[j]
        m_next = jnp.maximum(m_prev, jnp.max(s, axis=0, keepdims=True))
        alpha = jnp.exp2(m_prev - m_next)
        p = jnp.exp2(s - m_next)
        m_buf[j] = m_next
        vt = vt_ref[0, ki, j * head_dim:(j + 1) * head_dim, :]
        vt1 = jnp.concatenate([vt, jnp.ones((SUM_ROWS, tk), BF16)], axis=0)
        acc_buf[j] = alpha * acc_buf[j] + _dot(vt1, p.astype(BF16))

    def full_step(ki):
        score(1, ki)
        absorb(0, ki, masked=False)
        score(0, ki + 1)
        absorb(1, ki, masked=False)

    def two_steps(kp, carry):
        full_step(2 * kp)
        full_step(2 * kp + 1)
        return carry

    s_buf[0] = _dot(k2_ref[0, 0:tk, :], head_query(0, 0))

    def query_tile(qi, carry):
        for j in range(2):
            qh_buf[j] = head_query(qi, j)
        m_buf[...] = jnp.full_like(m_buf, MASK_VALUE)
        acc_buf[...] = jnp.zeros_like(acc_buf)

        lax.fori_loop(0, qi // 2, two_steps, 0)

        @pl.when(qi % 2 == 1)
        def _():
            full_step(qi - 1)

        score(1, qi)
        absorb(0, qi, masked=True)
        absorb(1, qi, masked=True)

        nxt = jnp.minimum(qi + 1, n_tiles - 1)
        s_buf[0] = _dot(k2_ref[0, 0:tk, :], head_query(nxt, 0))

        ot = jnp.concatenate(
            [acc_buf[j, 0:head_dim, :] / acc_buf[j, head_dim:head_dim + 1, :]
             for j in range(2)], axis=0)
        o_ref[pl.ds(pl.multiple_of(qi * tq, tq), tq), :] = ot.T.astype(o_ref.dtype)
        return carry

    lax.fori_loop(0, n_tiles, query_tile, 0)


def _prompt_attention(q2t, k2, vtb, *, n_heads, tq, tk):
    batch, n_tiles, hd, tile = vtb.shape
    assert tile == tq == tk, "attention tiles follow the projection kernel's row tile"
    seq = n_tiles * tile
    rows = batch * seq
    head_dim = hd // n_heads
    n_pairs = n_heads // 2
    kern = functools.partial(_prompt_attn_kernel, tq=tq, tk=tk, head_dim=head_dim,
                             n_tiles=n_tiles)
    return pl.pallas_call(
        kern,
        grid=(batch, n_pairs),
        in_specs=[
            pl.BlockSpec((1, n_tiles, 2 * LANES, tq), lambda b, hp: (b, 0, hp, 0)),
            pl.BlockSpec((1, seq, 2 * LANES), lambda b, hp: (b, 0, hp)),
            pl.BlockSpec((1, n_tiles, LANES, tk), lambda b, hp: (b, 0, hp, 0)),
        ],
        out_specs=pl.BlockSpec((seq, LANES), lambda b, hp: (b, hp)),
        out_shape=jax.ShapeDtypeStruct((rows, hd), BF16),
        scratch_shapes=[
            pltpu.VMEM((2, 2 * LANES, tq), BF16),
            pltpu.VMEM((2, tk, tq), F32),
            pltpu.VMEM((2, 1, tq), F32),
            pltpu.VMEM((2, head_dim + SUM_ROWS, tq), F32),
        ],
        compiler_params=pltpu.CompilerParams(
            dimension_semantics=("arbitrary", "arbitrary"),
            vmem_limit_bytes=VMEM_LIMIT_BYTES),
        name="prompt_attention",
    )(q2t, k2.reshape(batch, seq, 2 * hd), vtb)


def _sample_attn_kernel(pt_ref, *refs, n_heads, head_dim, dec_seq, pages_per_step, page_size,
                        n_chains):
    del pt_ref
    n_in = 3 * pages_per_step
    k_refs = refs[0:pages_per_step]
    v_refs = refs[pages_per_step:2 * pages_per_step]
    lf_refs = refs[2 * pages_per_step:n_in]
    q_ref, kn_ref, vn_ref, lfn_ref, sl_ref = refs[n_in:n_in + 5]
    o_ref = refs[n_in + 5]
    qrows_buf, cn_col, carry, m_buf, l_buf, acc_buf = refs[n_in + 6:]
    g = pl.program_id(1)
    n_rows = dec_seq * n_heads
    hd = n_heads * head_dim

    def head_mask(shape):
        r = lax.broadcasted_iota(jnp.int32, shape, 0) % n_heads
        lane = lax.broadcasted_iota(jnp.int32, shape, 1)
        return (lane >= r * head_dim) & (lane < (r + 1) * head_dim)

    def column_of(rowvec):
        wide = jnp.broadcast_to(rowvec, (n_rows, rowvec.shape[1]))
        r = lax.broadcasted_iota(jnp.int32, wide.shape, 0) % n_heads
        lane = lax.broadcasted_iota(jnp.int32, wide.shape, 1)
        return jnp.sum(jnp.where(lane == r, wide, 0.0), axis=1, keepdims=True)

    def online_update(c, s, weighted_values):
        m_prev = m_buf[c]
        m_next = jnp.maximum(m_prev, jnp.max(s, axis=1, keepdims=True))
        alpha = jnp.exp(m_prev - m_next)
        p = jnp.exp(s - m_next[:, 0:1])
        l_buf[c] = alpha * l_buf[c] + jnp.sum(p, axis=1, keepdims=True)
        acc_buf[c] = acc_buf[c] * alpha[:, 0:1] + weighted_values(p.astype(BF16))
        m_buf[c] = m_next

    @pl.when(g == 0)
    def _():
        q = q_ref[0]
        mask = head_mask((n_rows, hd))
        qrep = jnp.concatenate(
            [jnp.broadcast_to(q[t:t + 1, :], (n_heads, hd)) for t in range(dec_seq)], axis=0)
        qrows = jnp.where(mask, qrep, 0.0).astype(BF16)
        qrows_buf[...] = qrows
        m_buf[...] = jnp.full_like(m_buf, MASK_VALUE)
        l_buf[...] = jnp.zeros_like(l_buf)
        acc_buf[...] = jnp.zeros_like(acc_buf)
        carry[...] = jnp.zeros_like(carry)

        lfn = lfn_ref[0]
        cums = []
        run = None
        for t in range(dec_seq):
            run = lfn[t:t + 1, :] if run is None else run + lfn[t:t + 1, :]
            cums.append(run)
        r_t = lax.broadcasted_iota(jnp.int32, (n_rows, 1), 0) // n_heads
        cn = jnp.zeros((n_rows, 1), F32)
        for t in range(dec_seq):
            cn = jnp.where(r_t == t, column_of(cums[t]), cn)
        cn_col[...] = cn
        qrows_f = qrows.astype(F32)
        kn = kn_ref[0].astype(BF16).astype(F32)
        vn = vn_ref[0].astype(BF16).astype(F32)
        lane = lax.broadcasted_iota(jnp.int32, (n_rows, LANES), 1)
        s_new = jnp.full((n_rows, LANES), MASK_VALUE, F32)
        for s in range(dec_seq):
            qk = jnp.sum(qrows_f * kn[s:s + 1, :], axis=1, keepdims=True)
            s_new = jnp.where((lane == s) & (r_t >= s), qk + cn - column_of(cums[s]), s_new)

        def new_values(p):
            p = p.astype(F32)
            return sum(p[:, s:s + 1] * vn[s:s + 1, :] for s in range(dec_seq))

        online_update(0, s_new, new_values)

    sl = sl_ref[...]
    biases = [None] * pages_per_step
    run = carry[...]
    for j in reversed(range(pages_per_step)):
        lft = lf_refs[j][0]
        hi, mid, lo = _split3_bf16(lft)
        suffix = _dot(hi, sl) + _dot(mid, sl) + _dot(lo, sl) + run
        biases[j] = jnp.concatenate([suffix] * dec_seq, axis=0)
        run = run + jnp.sum(lft, axis=1, keepdims=True)
    carry[...] = run
    cn = cn_col[...]
    qrows = qrows_buf[...]
    per_chain = pages_per_step // n_chains
    chain_pages = [range(c * per_chain, (c + 1) * per_chain) for c in range(n_chains)]
    scores = []
    for pages in chain_pages:
        bias = jnp.concatenate([biases[j] for j in pages], axis=1) + cn
        kt = jnp.concatenate([k_refs[j][0].astype(BF16) for j in pages], axis=1)
        scores.append(_dot(qrows, kt) + bias)
    for c, pages in enumerate(chain_pages):
        vt = jnp.concatenate([v_refs[j][0].astype(BF16) for j in pages], axis=1)
        online_update(c, scores[c], lambda p, vt=vt: _dot_nt(p, vt))

    @pl.when(g == pl.num_programs(1) - 1)
    def _():
        m_all = m_buf[0]
        for c in range(1, n_chains):
            m_all = jnp.maximum(m_all, m_buf[c])
        l_all = jnp.zeros_like(m_all)
        acc_all = jnp.zeros((n_rows, hd), F32)
        for c in range(n_chains):
            w = jnp.exp(m_buf[c] - m_all)
            l_all = l_all + w * l_buf[c]
            acc_all = acc_all + w[:, 0:1] * acc_buf[c]
        out = jnp.where(head_mask((n_rows, hd)), acc_all / l_all[:, 0:1], 0.0)
        o_ref[0] = jnp.sum(out.reshape(dec_seq, n_heads, hd), axis=1).astype(o_ref.dtype)


def _sample_attention(page_table, cache_kt, cache_vt, cache_lft, q, k_new, v_new, lf_new,
                      *, n_heads, pages_per_step):
    dec_batch, dec_seq, hd = q.shape
    n_pages = page_table.shape[1]
    page_size = cache_kt.shape[2]
    head_dim = hd // n_heads
    n_groups = n_pages // pages_per_step
    n_rows = dec_seq * n_heads
    sl = jnp.asarray(np.tril(np.ones((page_size, page_size), np.float32), -1), BF16)

    def page_map(j):
        def index_map(b, g, pt):
            return (pt[b, (n_groups - 1 - g) * pages_per_step + j], 0, 0)
        return index_map

    seq_map = lambda b, g, pt: (b, 0, 0)
    in_specs = (
        [pl.BlockSpec((1, hd, page_size), page_map(j)) for j in range(pages_per_step)]
        + [pl.BlockSpec((1, hd, page_size), page_map(j)) for j in range(pages_per_step)]
        + [pl.BlockSpec((1, n_heads, page_size), page_map(j)) for j in range(pages_per_step)]
        + [pl.BlockSpec((1, dec_seq, hd), seq_map)] * 3
        + [pl.BlockSpec((1, dec_seq, n_heads), seq_map),
           pl.BlockSpec((page_size, page_size), lambda b, g, pt: (0, 0))])
    n_chains = math.gcd(pages_per_step, SAMPLE_SOFTMAX_CHAINS)
    kern = functools.partial(_sample_attn_kernel, n_heads=n_heads, head_dim=head_dim,
                             dec_seq=dec_seq, pages_per_step=pages_per_step, page_size=page_size,
                             n_chains=n_chains)
    grid_spec = pltpu.PrefetchScalarGridSpec(
        num_scalar_prefetch=1,
        grid=(dec_batch, n_groups),
        in_specs=in_specs,
        out_specs=pl.BlockSpec((1, dec_seq, hd), seq_map),
        scratch_shapes=[
            pltpu.VMEM((n_rows, hd), BF16),
            pltpu.VMEM((n_rows, 1), F32),
            pltpu.VMEM((n_heads, LANES), F32),
            pltpu.VMEM((n_chains, n_rows, LANES), F32),
            pltpu.VMEM((n_chains, n_rows, LANES), F32),
            pltpu.VMEM((n_chains, n_rows, hd), F32),
        ])
    operands = ([cache_kt] * pages_per_step + [cache_vt] * pages_per_step
                + [cache_lft] * pages_per_step + [q, k_new, v_new, lf_new, sl])
    return pl.pallas_call(
        kern,
        grid_spec=grid_spec,
        out_shape=jax.ShapeDtypeStruct((dec_batch, dec_seq, hd), BF16),
        compiler_params=pltpu.CompilerParams(
            dimension_semantics=("arbitrary", "arbitrary"),
            vmem_limit_bytes=VMEM_LIMIT_BYTES),
        name="sample_attention",
    )(page_table, *operands)


def _pad_history(state, hist):
    keep = state.shape[1]
    if keep == hist:
        return state
    return jnp.pad(state, ((0, 0), (hist - keep, 0), (0, 0)))


def _to_time_major(a):
    b, t = a.shape[:2]
    return jnp.swapaxes(a, 0, 1).reshape((1, t * b) + a.shape[2:])


def _from_time_major(a, b):
    t = a.shape[1] // b
    return jnp.swapaxes(a.reshape((t, b) + a.shape[2:]), 0, 1)


def kernel(x_prompt, x_sample, cache_k, cache_v, cache_logf, state_lru_h, state_lru_conv,
           state_ffn_conv, page_table, norm_a_g, w_a_in, conv_a_w, conv_a_b, w_a_gate, b_a_gate,
           lru_lambda, w_a_out, norm_f_g, w_f_in, conv_f_w, conv_f_b, w_f_out, norm_kv_g, w_kv,
           b_forget, norm_b_g, w_q, w_o, norm_out_g):
    batch, seq, d_model = x_prompt.shape
    dec_batch, dec_seq, _ = x_sample.shape
    n_phys, page_size, n_heads, head_dim = cache_k.shape
    hd = n_heads * head_dim
    n_a = w_a_in.shape[0]
    n_b = w_q.shape[0]
    d_rnn = w_a_out.shape[1]
    d_ff = w_f_out.shape[1]
    assert 2 * head_dim == LANES and n_heads % 2 == 0 and n_heads <= LANES
    assert dec_batch % SUBLANES == 0 and dec_seq >= CONV_A_WIDTH - 1

    row = lambda a: a.reshape(1, -1).astype(F32)
    w_a_in_b = w_a_in.astype(BF16)
    w_a_gate_b = w_a_gate.astype(BF16)
    w_a_out_b = w_a_out.astype(BF16)
    w_f_in_b = w_f_in.astype(BF16)
    w_f_out_b = w_f_out.astype(BF16)
    w_q_b = w_q.astype(BF16)
    w_o_b = w_o.astype(BF16)
    wk = w_kv[:, :hd].astype(BF16)
    wv = w_kv[:, hd:2 * hd].astype(BF16)
    wf = jnp.pad(w_kv[:, 2 * hd:], ((0, 0), (0, LANES - n_heads))).astype(BF16)
    bf = jnp.pad(b_forget.astype(F32), (0, LANES - n_heads)).reshape(1, LANES)

    def trunk(x2d, lru_h0, lru_conv0, ffn_conv0, *, groups, stride, tiles, with_aug,
              attention_fn):
        lru_h_new, lru_conv_new, ffn_conv_new = [], [], []
        layer = 0
        hist_a = _round_up((CONV_A_WIDTH - 1) * stride, SUBLANES)
        hist_f = _round_up((CONV_F_WIDTH - 1) * stride, SUBLANES)

        def ffn(x2d, layer, **kw):
            return _ffn_layer(x2d, _pad_history(ffn_conv0[layer], hist_f), row(norm_f_g[layer]),
                              w_f_in_b, conv_f_w[layer], row(conv_f_b[layer]),
                              w_f_out_b, layer=layer, groups=groups, stride=stride,
                              tile=tiles["ffn"], **kw)

        for i in range(n_a):
            x2d, c_new, h_new = _rglru_layer(
                x2d, _pad_history(lru_conv0[i], hist_a), lru_h0[i], row(norm_a_g[i]), w_a_in_b[i],
                conv_a_w[i], row(conv_a_b[i]), w_a_gate_b[i], b_a_gate[i][:, None, :],
                row(lru_lambda[i]), w_a_out_b[i], groups=groups, stride=stride,
                tile=tiles["rglru"], sub_tile=tiles["rglru_sub"])
            x2d, f_new = ffn(x2d, layer)
            lru_h_new.append(h_new)
            lru_conv_new.append(c_new)
            ffn_conv_new.append(f_new)
            layer += 1
        kvq = _kvq_layer(x2d, row(norm_kv_g), row(norm_b_g[0]), wk, wv, wf, bf, w_q_b[0],
                         n_heads=n_heads, groups=groups, tile=tiles["kvq"],
                         with_aug=with_aug)
        k_new, v_new, lf_new = kvq[:3]
        for j in range(n_b):
            assert j == 0, "one attention layer per shared K/V projection is supported"
            o2d = attention_fn(kvq)
            x2d, f_new = ffn(x2d, layer, pre=(o2d, w_o_b[j]),
                             final_gain=row(norm_out_g) if j == n_b - 1 else None)
            ffn_conv_new.append(f_new)
            layer += 1
        return x2d, k_new, v_new, lf_new, lru_h_new, lru_conv_new, ffn_conv_new

    zeros = lambda *s: jnp.zeros(s, F32)
    tiles = {"rglru": _pick_tile(seq, 512), "rglru_sub": _pick_tile(seq, 256),
             "ffn": _pick_tile(seq, 512), "kvq": _pick_tile(seq, 512)}

    def prompt_attention_fn(kvq):
        q2t, k2, vtb = kvq[3:]
        return _prompt_attention(q2t, k2, vtb, n_heads=n_heads, tq=tiles["kvq"], tk=tiles["kvq"])

    (y, k_p, v_p, lf_p, h_p, c_p, f_p) = trunk(
        x_prompt.reshape(batch * seq, d_model),
        zeros(n_a, batch, 1, d_rnn), zeros(n_a, batch, CONV_A_WIDTH - 1, d_rnn),
        zeros(n_a + n_b, batch, CONV_F_WIDTH - 1, d_ff),
        groups=batch, stride=1, tiles=tiles, with_aug=True, attention_fn=prompt_attention_fn)
    y_prompt = y.reshape(batch, seq, d_model)
    k_prompt = k_p.reshape(batch, n_heads, head_dim, seq).transpose(0, 3, 1, 2)
    v_prompt = v_p.reshape(batch, n_heads, head_dim, seq).transpose(0, 3, 1, 2)
    logf_prompt = lf_p.transpose(0, 2, 1)
    lru_h_prompt = jnp.stack([h.reshape(batch, d_rnn) for h in h_p], axis=0)
    lru_conv_prompt = jnp.stack(c_p, axis=0)
    ffn_conv_prompt = jnp.stack(f_p, axis=0)

    cache_kt = cache_k.transpose(0, 2, 3, 1).reshape(n_phys, hd, page_size)
    cache_vt = cache_v.transpose(0, 2, 3, 1).reshape(n_phys, hd, page_size)
    cache_lft = jnp.swapaxes(cache_logf, 1, 2)
    n_pages = page_table.shape[1]

    def sample_attention_fn(kvq):
        k_t, v_t, lf_t, q_t = kvq
        bm = lambda a: _from_time_major(a[None], dec_batch)
        o = _sample_attention(page_table, cache_kt, cache_vt, cache_lft, bm(q_t), bm(k_t),
                              bm(v_t), bm(lf_t), n_heads=n_heads,
                              pages_per_step=_pick_tile(n_pages, SAMPLE_PAGES_PER_STEP))
        return _to_time_major(o)[0]

    n_rows = dec_batch * dec_seq
    tiles = {"rglru": n_rows, "rglru_sub": n_rows, "ffn": n_rows, "kvq": n_rows}
    (y, k_s, v_s, lf_s, h_s, c_s, f_s) = trunk(
        _to_time_major(x_sample)[0],
        state_lru_h[:, None], jnp.stack([_to_time_major(s) for s in state_lru_conv]),
        jnp.stack([_to_time_major(s) for s in state_ffn_conv]),
        groups=1, stride=dec_batch, tiles=tiles, with_aug=False,
        attention_fn=sample_attention_fn)
    bm = lambda a: _from_time_major(a[None], dec_batch)
    y_sample = bm(y)
    k_sample = bm(k_s).reshape(dec_batch, dec_seq, n_heads, head_dim)
    v_sample = bm(v_s).reshape(dec_batch, dec_seq, n_heads, head_dim)
    logf_sample = bm(lf_s)
    lru_h_sample = jnp.stack([h[0] for h in h_s], axis=0)
    lru_conv_sample = jnp.stack([_from_time_major(c, dec_batch) for c in c_s], axis=0)
    ffn_conv_sample = jnp.stack([_from_time_major(f, dec_batch) for f in f_s], axis=0)

    return (y_prompt, y_sample, k_prompt, v_prompt, logf_prompt, lru_h_prompt, lru_conv_prompt,
            ffn_conv_prompt, k_sample, v_sample, logf_sample, lru_h_sample, lru_conv_sample,
            ffn_conv_sample)
```

```python
import functools
import math

import numpy as np
import jax
import jax.numpy as jnp
from jax import lax
from jax.experimental import pallas as pl
from jax.experimental.pallas import tpu as pltpu

F32 = jnp.float32
BF16 = jnp.bfloat16

LANES = 128
SUBLANES = 8
VMEM_LIMIT_BYTES = 56 * 1024 * 1024

RMS_EPS = 1e-6
LRU_C = 8.0
CONV_A_WIDTH = 4
CONV_F_WIDTH = 3
MASK_VALUE = -1e30
LOG2_E = 1.4426950408889634

AUG_LANES_PER_HEAD = 8
SUM_ROWS = 16
SAMPLE_PAGES_PER_STEP = 16
SAMPLE_SOFTMAX_CHAINS = 4
PROMPT_ROW_TILE = 512
RGLRU_SUB_TILE = 256
FF_CHUNK = 1024


def _round_up(x, m):
    return (x + m - 1) // m * m


def _pick_tile(n, target):
    t = min(n, target)
    while n % t:
        t //= 2
    return t


def _rms_normalize(x):
    return x * lax.rsqrt(jnp.mean(x * x, axis=-1, keepdims=True) + RMS_EPS)


def _split3_bf16(x):
    hi = x.astype(BF16)
    r1 = x - hi.astype(F32)
    mid = r1.astype(BF16)
    lo = (r1 - mid.astype(F32)).astype(BF16)
    return hi, mid, lo


def _dot(a, b):
    return jnp.dot(a, b, preferred_element_type=F32)


def _dot_nt(a, b):
    return lax.dot_general(a, b, (((1,), (1,)), ((), ())), preferred_element_type=F32)


def _shifted_conv(buf_ref, w, b, *, width, hist, tile, stride):
    acc = None
    for j in range(width):
        back = (width - 1 - j) * stride
        term = buf_ref[hist - back:hist - back + tile, :] * w[j:j + 1, :]
        acc = term if acc is None else acc + term
    return acc + b


def _rglru_kernel(x_ref, conv0_ref, h0_ref, g_ref, win_ref, cw_ref, cb_ref, wg_ref, bg_ref,
                  lam_ref, wout_ref, perm_ref, y_ref, convn_ref, hn_ref, rec_buf, a_buf, b_buf,
                  h_carry, tail_buf, *, stride, sub_tile, n_sub):
    t = pl.program_id(1)
    d_rnn = rec_buf.shape[2]
    n_blocks, lru_block, _ = wg_ref.shape
    interleaved = stride == 1
    assert interleaved or n_sub == 1
    step_rows = SUBLANES if interleaved else stride
    hist = (CONV_A_WIDTH - 1) * step_rows
    seg = sub_tile // SUBLANES
    sub = lax.broadcasted_iota(jnp.int32, (SUBLANES, d_rnn), 0)

    @pl.when(t == 0)
    def _():
        h_carry[...] = h0_ref[0]
        if interleaved:
            tail_buf[...] = conv0_ref[0]
        else:
            rec_buf[0, 0:hist, :] = conv0_ref[0]

    def project_in(u):
        x = x_ref[u * sub_tile:(u + 1) * sub_tile, :]
        hn = (_rms_normalize(x) * g_ref[...]).astype(BF16)
        if interleaved:
            hn = _dot(perm_ref[0], hn).astype(BF16)
        else:
            @pl.when(t != 0)
            def _():
                rec_buf[u, 0:hist, :] = rec_buf[u, sub_tile:sub_tile + hist, :]

        proj = _dot(hn, win_ref[...])
        rec_buf[u, hist:hist + sub_tile, :] = proj[:, d_rnn:]
        return proj[:, :d_rnn]

    def recur(u):
        rec = rec_buf.at[u]
        if interleaved:
            for m in range(1, CONV_A_WIDTH):
                src = hist + (seg - m) * SUBLANES
                block = pltpu.roll(rec[src:src + SUBLANES, :], 1, 0)
                prev = tail_buf[SUBLANES - m:SUBLANES - m + 1, :]
                dst = (CONV_A_WIDTH - 1 - m) * SUBLANES
                rec[dst:dst + SUBLANES, :] = jnp.where(sub == 0, prev, block)
            for m in range(1, CONV_A_WIDTH):
                last = hist + (seg - m) * SUBLANES + SUBLANES - 1
                tail_buf[SUBLANES - m:SUBLANES - m + 1, :] = rec[last:last + 1, :]
        xc = _shifted_conv(rec, cw_ref[...], cb_ref[...], width=CONV_A_WIDTH, hist=hist,
                           tile=sub_tile, stride=step_rows)
        xcb = xc.astype(BF16)
        log_sig_lam = jax.nn.log_sigmoid(lam_ref[...])
        for n in range(n_blocks):
            cols = slice(n * lru_block, (n + 1) * lru_block)
            gates = jax.nn.sigmoid(_dot(xcb[:, cols], wg_ref[n]) + bg_ref[n])
            r = gates[:, :lru_block]
            ig = gates[:, lru_block:]
            log_a = LRU_C * r * log_sig_lam[:, cols]
            a = jnp.exp(log_a)
            a_buf[u, :, cols] = a
            one_minus_a2 = -jnp.tanh(log_a) * (a * a + 1.0)
            b_buf[u, :, cols] = jnp.sqrt(one_minus_a2) * ig * xc[:, cols]

        if interleaved:
            h_loc = jnp.zeros((SUBLANES, d_rnn), F32)
            decay = jnp.ones((SUBLANES, d_rnn), F32)
            for j in range(seg):
                rows = slice(j * SUBLANES, (j + 1) * SUBLANES)
                a_j = a_buf[u, rows, :]
                h_loc = a_j * h_loc + b_buf[u, rows, :]
                decay = a_j * decay
                b_buf[u, rows, :] = h_loc
                a_buf[u, rows, :] = decay
            carry = h_carry[...]
            carry_in = jnp.zeros((SUBLANES, d_rnn), F32)
            for s in range(SUBLANES):
                carry_in = jnp.where(sub == s, carry, carry_in)
                carry = h_loc[s:s + 1, :] + decay[s:s + 1, :] * carry
            h_carry[...] = carry
            return (b_buf[u].reshape(seg, SUBLANES, d_rnn)
                    + a_buf[u].reshape(seg, SUBLANES, d_rnn) * carry_in[None]
                    ).reshape(sub_tile, d_rnn)
        h_last = h_carry[...]
        for k in range(sub_tile // stride):
            rows = slice(k * stride, (k + 1) * stride)
            h_last = a_buf[u, rows, :] * h_last + b_buf[u, rows, :]
            b_buf[u, rows, :] = h_last
        h_carry[...] = h_last
        return b_buf[u]

    def project_out(u, gate_branch, hs):
        yv = (hs * jax.nn.gelu(gate_branch, approximate=True)).astype(BF16)
        if interleaved:
            yv = _dot(perm_ref[1], yv).astype(BF16)
        rows = slice(u * sub_tile, (u + 1) * sub_tile)
        y_ref[rows, :] = x_ref[rows, :] + _dot(yv, wout_ref[...])

    gate_branches = [project_in(u) for u in range(n_sub)]
    hidden = [recur(u) for u in range(n_sub)]
    for u in range(n_sub):
        project_out(u, gate_branches[u], hidden[u])

    hn_ref[0] = h_carry[...]
    if interleaved:
        convn_ref[0] = tail_buf[SUBLANES - (CONV_A_WIDTH - 1):SUBLANES, :]
    else:
        convn_ref[0] = rec_buf[0, sub_tile:sub_tile + hist, :]


def _rglru_layer(x2d, conv0, h0, g, w_in, conv_w, conv_b, w_gate, b_gate, lam, w_out,
                 *, groups, stride, tile, sub_tile):
    rows, d_model = x2d.shape
    d_rnn = w_out.shape[0]
    n_tiles = rows // groups // tile
    n_sub = tile // sub_tile
    hist = conv0.shape[1]
    keep = (CONV_A_WIDTH - 1) * stride
    step_rows = SUBLANES if stride == 1 else stride
    const2 = lambda b, t: (0, 0)
    const3 = lambda b, t: (0, 0, 0)
    seg = sub_tile // SUBLANES
    order = np.arange(sub_tile).reshape(SUBLANES, seg).T.reshape(-1)
    gather = np.zeros((sub_tile, sub_tile), np.float32)
    gather[np.arange(sub_tile), order] = 1.0
    perm = jnp.asarray(np.stack([gather, gather.T]), BF16)
    kern = functools.partial(_rglru_kernel, stride=stride, sub_tile=sub_tile, n_sub=n_sub)
    return pl.pallas_call(
        kern,
        grid=(groups, n_tiles),
        in_specs=[
            pl.BlockSpec((tile, d_model), lambda b, t: (b * n_tiles + t, 0)),
            pl.BlockSpec((1, hist, d_rnn), lambda b, t: (b, 0, 0)),
            pl.BlockSpec((1, stride, d_rnn), lambda b, t: (b, 0, 0)),
            pl.BlockSpec((1, d_model), const2),
            pl.BlockSpec(w_in.shape, const2),
            pl.BlockSpec(conv_w.shape, const2),
            pl.BlockSpec((1, d_rnn), const2),
            pl.BlockSpec(w_gate.shape, const3),
            pl.BlockSpec(b_gate.shape, const3),
            pl.BlockSpec((1, d_rnn), const2),
            pl.BlockSpec(w_out.shape, const2),
            pl.BlockSpec(perm.shape, const3),
        ],
        out_specs=[
            pl.BlockSpec((tile, d_model), lambda b, t: (b * n_tiles + t, 0)),
            pl.BlockSpec((1, keep, d_rnn), lambda b, t: (b, 0, 0)),
            pl.BlockSpec((1, stride, d_rnn), lambda b, t: (b, 0, 0)),
        ],
        out_shape=[
            jax.ShapeDtypeStruct((rows, d_model), F32),
            jax.ShapeDtypeStruct((groups, keep, d_rnn), F32),
            jax.ShapeDtypeStruct((groups, stride, d_rnn), F32),
        ],
        scratch_shapes=[
            pltpu.VMEM((n_sub, (CONV_A_WIDTH - 1) * step_rows + sub_tile, d_rnn), F32),
            pltpu.VMEM((n_sub, sub_tile, d_rnn), F32),
            pltpu.VMEM((n_sub, sub_tile, d_rnn), F32),
            pltpu.VMEM((stride, d_rnn), F32),
            pltpu.VMEM((SUBLANES, d_rnn), F32),
        ],
        compiler_params=pltpu.CompilerParams(
            dimension_semantics=("arbitrary", "arbitrary"),
            vmem_limit_bytes=VMEM_LIMIT_BYTES),
        name="rglru_layer",
    )(x2d, conv0, h0, g, w_in, conv_w, conv_b, w_gate, b_gate, lam, w_out, perm)


def _ffn_kernel(*refs, has_pre, has_final_norm, stride, hist, tile, ff_chunk):
    refs = list(refs)
    x_ref = refs.pop(0)
    if has_pre:
        o_ref = refs.pop(0)
        wo_ref = refs.pop(0)
    gf_ref, win_ref, cw_ref, cb_ref, wout_ref, g0_ref = refs[:6]
    refs = refs[6:]
    if has_final_norm:
        gout_ref = refs.pop(0)
    y_ref, convn_ref, gate_buf, hist_buf = refs
    t = pl.program_id(1)
    d_ff = wout_ref.shape[0]
    keep = (CONV_F_WIDTH - 1) * stride

    x1 = x_ref[...]
    if has_pre:
        x1 = x1 + _dot(o_ref[...], wo_ref[...])
    y_ref[...] = x1
    h = (_rms_normalize(x1) * gf_ref[...]).astype(BF16)

    for c in range(d_ff // ff_chunk):
        cols = slice(c * ff_chunk, (c + 1) * ff_chunk)
        up_cols = slice(d_ff + c * ff_chunk, d_ff + (c + 1) * ff_chunk)

        @pl.when(t == 0)
        def _():
            gate_buf[0:hist, :] = g0_ref[0, :, cols]

        @pl.when(t != 0)
        def _():
            gate_buf[0:hist, :] = hist_buf[c]

        gate_buf[hist:hist + tile, :] = _dot(h, win_ref[:, cols])
        up = _dot(h, win_ref[:, up_cols])
        gate = _shifted_conv(gate_buf, cw_ref[:, cols], cb_ref[:, cols], width=CONV_F_WIDTH,
                             hist=hist, tile=tile, stride=stride)
        hist_buf[c] = gate_buf[tile:tile + hist, :]
        convn_ref[0, :, cols] = gate_buf[hist + tile - keep:hist + tile, :]
        yv = (jax.nn.gelu(gate, approximate=True) * up).astype(BF16)
        y_ref[...] += _dot(yv, wout_ref[cols, :])

    if has_final_norm:
        y_ref[...] = _rms_normalize(y_ref[...]) * gout_ref[...]


def _ffn_layer(x2d, g0, gf, w_in, conv_w, conv_b, w_out, *, layer, groups, stride, tile,
               pre=None, final_gain=None):
    rows, d_model = x2d.shape
    d_ff = w_out.shape[1]
    ff_chunk = _pick_tile(d_ff, FF_CHUNK)
    n_chunks = d_ff // ff_chunk
    n_tiles = rows // groups // tile
    hist = g0.shape[1]
    keep = (CONV_F_WIDTH - 1) * stride
    const2 = lambda b, t: (0, 0)
    row_map = lambda b, t: (b * n_tiles + t, 0)
    resident = functools.partial(pl.BlockSpec, index_map=const2, pipeline_mode=pl.Buffered(1))
    of_layer = lambda a: pl.BlockSpec((None,) + a.shape[1:], lambda b, t: (layer, 0, 0),
                                      pipeline_mode=pl.Buffered(1))
    operands = [x2d]
    in_specs = [pl.BlockSpec((tile, d_model), row_map)]
    if pre is not None:
        o2d, w_o = pre
        operands += [o2d, w_o]
        in_specs += [pl.BlockSpec((tile, o2d.shape[1]), row_map), resident(w_o.shape)]
    operands += [gf, w_in, conv_w, conv_b, w_out, g0]
    in_specs += [
        resident((1, d_model)),
        of_layer(w_in),
        resident(conv_w.shape),
        resident(conv_b.shape),
        of_layer(w_out),
        pl.BlockSpec((1, hist, d_ff), lambda b, t: (b, 0, 0)),
    ]
    if final_gain is not None:
        operands.append(final_gain)
        in_specs.append(resident((1, d_model)))
    kern = functools.partial(_ffn_kernel, has_pre=pre is not None,
                             has_final_norm=final_gain is not None, stride=stride, hist=hist,
                             tile=tile, ff_chunk=ff_chunk)
    return pl.pallas_call(
        kern,
        grid=(groups, n_tiles),
        in_specs=in_specs,
        out_specs=[
            pl.BlockSpec((tile, d_model), row_map),
            pl.BlockSpec((1, keep, d_ff), lambda b, t: (b, 0, 0)),
        ],
        out_shape=[
            jax.ShapeDtypeStruct((rows, d_model), F32),
            jax.ShapeDtypeStruct((groups, keep, d_ff), F32),
        ],
        scratch_shapes=[
            pltpu.VMEM((hist + tile, ff_chunk), F32),
            pltpu.VMEM((n_chunks, hist, ff_chunk), F32),
        ],
        compiler_params=pltpu.CompilerParams(
            dimension_semantics=("arbitrary", "arbitrary"),
            vmem_limit_bytes=VMEM_LIMIT_BYTES),
        name="conv_ffn",
    )(*operands)


def _kvq_kernel(*refs, n_heads, scale, with_aug, tile):
    (x_ref, gkv_ref, gb_ref, wkf_ref, wv_ref, bf_ref, wq_ref) = refs[:7]
    if with_aug:
        tri_ref, pq_ref, pk_ref, oneq_ref, onek_ref = refs[7:12]
        kt_ref, vt_ref, lft_ref, q2t_ref, k2_ref, vtb_ref, f_carry = refs[12:]
    else:
        k_ref, v_ref, lf_ref, q_ref = refs[7:]
    hd = wv_ref.shape[1]
    n = _rms_normalize(x_ref[...])
    hk = (n * gkv_ref[...]).astype(BF16)
    hq = (n * gb_ref[...]).astype(BF16)
    kz = _dot(hk, wkf_ref[...])
    k = kz[:, :hd]
    v = _dot(hk, wv_ref[...])
    z = kz[:, hd:] + bf_ref[...]
    lane = lax.broadcasted_iota(jnp.int32, z.shape, 1)
    lf = jnp.where(lane < n_heads, jax.nn.log_sigmoid(z), 0.0)
    q = _dot(hq, wq_ref[...]) * scale
    if not with_aug:
        k_ref[...] = k
        v_ref[...] = v
        lf_ref[...] = lf[:, :n_heads]
        q_ref[...] = q
        return

    kt_ref[0] = k.T
    vt = v.T
    vt_ref[0] = vt
    vtb_ref[0, 0] = vt.astype(BF16)
    lft_ref[0] = lf.T[:n_heads, :]

    @pl.when(pl.program_id(1) == 0)
    def _():
        f_carry[...] = jnp.zeros_like(f_carry)

    sums = _dot(tri_ref[...], jnp.concatenate(_split3_bf16(lf), axis=1))
    f_cum = (sums[:, :LANES] + sums[:, LANES:2 * LANES] + sums[:, 2 * LANES:]) + f_carry[...]
    f_carry[...] = f_cum[tile - 1:tile, :]

    f_cat = jnp.concatenate(_split3_bf16(f_cum * LOG2_E), axis=1)
    aug_qt = (_dot(f_cat, pq_ref[...]) + oneq_ref[...]).T.astype(BF16)
    aug_k = (_dot(f_cat, pk_ref[...]) + onek_ref[...]).astype(BF16)
    qt = q.T.astype(BF16)
    kb = k.astype(BF16)
    for hp in range(n_heads // 2):
        src = slice(hp * LANES, (hp + 1) * LANES)
        q2t_ref[0, 0, 2 * hp * LANES:(2 * hp + 1) * LANES, :] = qt[src, :]
        q2t_ref[0, 0, (2 * hp + 1) * LANES:(2 * hp + 2) * LANES, :] = aug_qt[src, :]
        k2_ref[:, 2 * hp * LANES:(2 * hp + 1) * LANES] = kb[:, src]
        k2_ref[:, (2 * hp + 1) * LANES:(2 * hp + 2) * LANES] = aug_k[:, src]


def _aug_constants(n_heads, tile):
    n_pairs = n_heads // 2
    pq = np.zeros((3 * LANES, n_pairs * LANES), np.float32)
    pk = np.zeros((3 * LANES, n_pairs * LANES), np.float32)
    oneq = np.zeros((1, n_pairs * LANES), np.float32)
    onek = np.zeros((1, n_pairs * LANES), np.float32)
    for h in range(n_heads):
        base = (h // 2) * LANES + (h % 2) * AUG_LANES_PER_HEAD
        for piece in range(3):
            pq[piece * LANES + h, base + piece] = 1.0
            onek[0, base + piece] = 1.0
            pk[piece * LANES + h, base + 3 + piece] = -1.0
            oneq[0, base + 3 + piece] = 1.0
    tri = np.tril(np.ones((tile, tile), np.float32))
    return (jnp.asarray(tri, BF16), jnp.asarray(pq, BF16), jnp.asarray(pk, BF16),
            jnp.asarray(oneq), jnp.asarray(onek))


def _kvq_layer(x2d, gkv, gb, wkf, wv, bf, wq, *, n_heads, groups, tile, with_aug):
    rows, d_model = x2d.shape
    hd = wv.shape[1]
    n_tiles = rows // groups // tile
    scale = float(hd // n_heads) ** -0.5
    const2 = lambda b, t: (0, 0)
    row_map = lambda b, t: (b * n_tiles + t, 0)
    operands = [x2d, gkv, gb, wkf, wv, bf, wq]
    in_specs = [pl.BlockSpec((tile, d_model), row_map)]
    in_specs += [pl.BlockSpec(a.shape, const2) for a in operands[1:]]
    scratch = []
    if with_aug:
        scale *= LOG2_E
        seq = rows // groups
        col_map = lambda b, t: (b, 0, t)
        consts = _aug_constants(n_heads, tile)
        operands += list(consts)
        in_specs += [pl.BlockSpec(a.shape, const2) for a in consts]
        tile_map = lambda b, t: (b, t, 0, 0)
        out_shape = [jax.ShapeDtypeStruct((groups, hd, seq), F32),
                     jax.ShapeDtypeStruct((groups, hd, seq), F32),
                     jax.ShapeDtypeStruct((groups, n_heads, seq), F32),
                     jax.ShapeDtypeStruct((groups, n_tiles, 2 * hd, tile), BF16),
                     jax.ShapeDtypeStruct((rows, 2 * hd), BF16),
                     jax.ShapeDtypeStruct((groups, n_tiles, hd, tile), BF16)]
        out_specs = [pl.BlockSpec((1, hd, tile), col_map), pl.BlockSpec((1, hd, tile), col_map),
                     pl.BlockSpec((1, n_heads, tile), col_map),
                     pl.BlockSpec((1, 1, 2 * hd, tile), tile_map),
                     pl.BlockSpec((tile, 2 * hd), row_map),
                     pl.BlockSpec((1, 1, hd, tile), tile_map)]
        scratch = [pltpu.VMEM((1, LANES), F32)]
    else:
        out_shape = [jax.ShapeDtypeStruct((rows, hd), F32), jax.ShapeDtypeStruct((rows, hd), F32),
                     jax.ShapeDtypeStruct((rows, n_heads), F32),
                     jax.ShapeDtypeStruct((rows, hd), F32)]
        out_specs = [pl.BlockSpec((tile, hd), row_map), pl.BlockSpec((tile, hd), row_map),
                     pl.BlockSpec((tile, n_heads), row_map), pl.BlockSpec((tile, hd), row_map)]
    kern = functools.partial(_kvq_kernel, n_heads=n_heads, scale=scale, with_aug=with_aug,
                             tile=tile)
    return pl.pallas_call(
        kern,
        grid=(groups, n_tiles),
        in_specs=in_specs,
        out_specs=out_specs,
        out_shape=out_shape,
        scratch_shapes=scratch,
        compiler_params=pltpu.CompilerParams(
            dimension_semantics=("arbitrary", "arbitrary"),
            vmem_limit_bytes=VMEM_LIMIT_BYTES),
        name="kv_q_proj",
    )(*operands)


def _prompt_attn_kernel(q2t_ref, k2_ref, vt_ref, o_ref, qh_buf, s_buf, m_buf, acc_buf,
                        *, tq, tk, head_dim, n_tiles):
    def head_query(qi, j):
        q2t = q2t_ref[0, qi]
        chan = lax.broadcasted_iota(jnp.int32, q2t.shape, 0)
        aug = chan - LANES
        own = ((chan >= j * head_dim) & (chan < (j + 1) * head_dim)) | (
            (aug >= j * AUG_LANES_PER_HEAD) & (aug < (j + 1) * AUG_LANES_PER_HEAD))
        return jnp.where(own, q2t, jnp.zeros_like(q2t))

    def score(j, ki):
        k2 = k2_ref[0, pl.ds(pl.multiple_of(ki * tk, tk), tk), :]
        s_buf[j] = _dot(k2, qh_buf[j])

    def absorb(j, ki, masked):
        s = s_buf[j]
        if masked:
            causal = (lax.broadcasted_iota(jnp.int32, s.shape, 0)
                      <= lax.broadcasted_iota(jnp.int32, s.shape, 1))
            s = jnp.where(causal, s, MASK_VALUE)
        m_prev = m_buf[j]
        m_next = jnp.maximum(m_prev, jnp.max(s, axis=0, keepdims=True))
        alpha = jnp.exp2(m_prev - m_next)
        p = jnp.exp2(s - m_next)
        m_buf[j] = m_next
        vt = vt_ref[0, ki, j * head_dim:(j + 1) * head_dim, :]
        vt1 = jnp.concatenate([vt, jnp.ones((SUM_ROWS, tk), BF16)], axis=0)
        acc_buf[j] = alpha * acc_buf[j] + _dot(vt1, p.astype(BF16))

    def full_step(ki):
        score(1, ki)
        absorb(0, ki, masked=False)
        score(0, ki + 1)
        absorb(1, ki, masked=False)

    def two_steps(kp, carry):
        full_step(2 * kp)
        full_step(2 * kp + 1)
        return carry

    s_buf[0] = _dot(k2_ref[0, 0:tk, :], head_query(0, 0))

    def query_tile(qi, carry):
        for j in range(2):
            qh_buf[j] = head_query(qi, j)
        m_buf[...] = jnp.full_like(m_buf, MASK_VALUE)
        acc_buf[...] = jnp.zeros_like(acc_buf)

        lax.fori_loop(0, qi // 2, two_steps, 0)

        @pl.when(qi % 2 == 1)
        def _():
            full_step(qi - 1)

        score(1, qi)
        absorb(0, qi, masked=True)
        absorb(1, qi, masked=True)

        nxt = jnp.minimum(qi + 1, n_tiles - 1)
        s_buf[0] = _dot(k2_ref[0, 0:tk, :], head_query(nxt, 0))

        ot = jnp.concatenate(
            [acc_buf[j, 0:head_dim, :] / acc_buf[j, head_dim:head_dim + 1, :]
             for j in range(2)], axis=0)
        o_ref[pl.ds(pl.multiple_of(qi * tq, tq), tq), :] = ot.T.astype(o_ref.dtype)
        return carry

    lax.fori_loop(0, n_tiles, query_tile, 0)


def _prompt_attention(q2t, k2, vtb, *, n_heads, tq, tk):
    batch, n_tiles, hd, tile = vtb.shape
    assert tile == tq == tk, "attention tiles follow the projection kernel's row tile"
    seq = n_tiles * tile
    rows = batch * seq
    head_dim = hd // n_heads
    n_pairs = n_heads // 2
    kern = functools.partial(_prompt_attn_kernel, tq=tq, tk=tk, head_dim=head_dim,
                             n_tiles=n_tiles)
    return pl.pallas_call(
        kern,
        grid=(batch, n_pairs),
        in_specs=[
            pl.BlockSpec((1, n_tiles, 2 * LANES, tq), lambda b, hp: (b, 0, hp, 0)),
            pl.BlockSpec((1, seq, 2 * LANES), lambda b, hp: (b, 0, hp)),
            pl.BlockSpec((1, n_tiles, LANES, tk), lambda b, hp: (b, 0, hp, 0)),
        ],
        out_specs=pl.BlockSpec((seq, LANES), lambda b, hp: (b, hp)),
        out_shape=jax.ShapeDtypeStruct((rows, hd), BF16),
        scratch_shapes=[
            pltpu.VMEM((2, 2 * LANES, tq), BF16),
            pltpu.VMEM((2, tk, tq), F32),
            pltpu.VMEM((2, 1, tq), F32),
            pltpu.VMEM((2, head_dim + SUM_ROWS, tq), F32),
        ],
        compiler_params=pltpu.CompilerParams(
            dimension_semantics=("arbitrary", "arbitrary"),
            vmem_limit_bytes=VMEM_LIMIT_BYTES),
        name="prompt_attention",
    )(q2t, k2.reshape(batch, seq, 2 * hd), vtb)


def _sample_attn_kernel(pt_ref, *refs, n_heads, head_dim, dec_seq, pages_per_step, page_size,
                        n_chains):
    del pt_ref
    n_in = 3 * pages_per_step
    k_refs = refs[0:pages_per_step]
    v_refs = refs[pages_per_step:2 * pages_per_step]
    lf_refs = refs[2 * pages_per_step:n_in]
    q_ref, kn_ref, vn_ref, lfn_ref, sl_ref = refs[n_in:n_in + 5]
    o_ref = refs[n_in + 5]
    qrows_buf, cn_col, carry, m_buf, l_buf, acc_buf = refs[n_in + 6:]
    g = pl.program_id(1)
    n_rows = dec_seq * n_heads
    hd = n_heads * head_dim

    def head_mask(shape):
        r = lax.broadcasted_iota(jnp.int32, shape, 0) % n_heads
        lane = lax.broadcasted_iota(jnp.int32, shape, 1)
        return (lane >= r * head_dim) & (lane < (r + 1) * head_dim)

    def column_of(rowvec):
        wide = jnp.broadcast_to(rowvec, (n_rows, rowvec.shape[1]))
        r = lax.broadcasted_iota(jnp.int32, wide.shape, 0) % n_heads
        lane = lax.broadcasted_iota(jnp.int32, wide.shape, 1)
        return jnp.sum(jnp.where(lane == r, wide, 0.0), axis=1, keepdims=True)

    def online_update(c, s, weighted_values):
        m_prev = m_buf[c]
        m_next = jnp.maximum(m_prev, jnp.max(s, axis=1, keepdims=True))
        alpha = jnp.exp(m_prev - m_next)
        p = jnp.exp(s - m_next[:, 0:1])
        l_buf[c] = alpha * l_buf[c] + jnp.sum(p, axis=1, keepdims=True)
        acc_buf[c] = acc_buf[c] * alpha[:, 0:1] + weighted_values(p.astype(BF16))
        m_buf[c] = m_next

    @pl.when(g == 0)
    def _():
        q = q_ref[0]
        mask = head_mask((n_rows, hd))
        qrep = jnp.concatenate(
            [jnp.broadcast_to(q[t:t + 1, :], (n_heads, hd)) for t in range(dec_seq)], axis=0)
        qrows = jnp.where(mask, qrep, 0.0).astype(BF16)
        qrows_buf[...] = qrows
        m_buf[...] = jnp.full_like(m_buf, MASK_VALUE)
        l_buf[...] = jnp.zeros_like(l_buf)
        acc_buf[...] = jnp.zeros_like(acc_buf)
        carry[...] = jnp.zeros_like(carry)

        lfn = lfn_ref[0]
        cums = []
        run = None
        for t in range(dec_seq):
            run = lfn[t:t + 1, :] if run is None else run + lfn[t:t + 1, :]
            cums.append(run)
        r_t = lax.broadcasted_iota(jnp.int32, (n_rows, 1), 0) // n_heads
        cn = jnp.zeros((n_rows, 1), F32)
        for t in range(dec_seq):
            cn = jnp.where(r_t == t, column_of(cums[t]), cn)
        cn_col[...] = cn
        qrows_f = qrows.astype(F32)
        kn = kn_ref[0].astype(BF16).astype(F32)
        vn = vn_ref[0].astype(BF16).astype(F32)
        lane = lax.broadcasted_iota(jnp.int32, (n_rows, LANES), 1)
        s_new = jnp.full((n_rows, LANES), MASK_VALUE, F32)
        for s in range(dec_seq):
            qk = jnp.sum(qrows_f * kn[s:s + 1, :], axis=1, keepdims=True)
            s_new = jnp.where((lane == s) & (r_t >= s), qk + cn - column_of(cums[s]), s_new)

        def new_values(p):
            p = p.astype(F32)
            return sum(p[:, s:s + 1] * vn[s:s + 1, :] for s in range(dec_seq))

        online_update(0, s_new, new_values)

    sl = sl_ref[...]
    biases = [None] * pages_per_step
    run = carry[...]
    for j in reversed(range(pages_per_step)):
        lft = lf_refs[j][0]
        hi, mid, lo = _split3_bf16(lft)
        suffix = _dot(hi, sl) + _dot(mid, sl) + _dot(lo, sl) + run
        biases[j] = jnp.concatenate([suffix] * dec_seq, axis=0)
        run = run + jnp.sum(lft, axis=1, keepdims=True)
    carry[...] = run
    cn = cn_col[...]
    qrows = qrows_buf[...]
    per_chain = pages_per_step // n_chains
    chain_pages = [range(c * per_chain, (c + 1) * per_chain) for c in range(n_chains)]
    scores = []
    for pages in chain_pages:
        bias = jnp.concatenate([biases[j] for j in pages], axis=1) + cn
        kt = jnp.concatenate([k_refs[j][0].astype(BF16) for j in pages], axis=1)
        scores.append(_dot(qrows, kt) + bias)
    for c, pages in enumerate(chain_pages):
        vt = jnp.concatenate([v_refs[j][0].astype(BF16) for j in pages], axis=1)
        online_update(c, scores[c], lambda p, vt=vt: _dot_nt(p, vt))

    @pl.when(g == pl.num_programs(1) - 1)
    def _():
        m_all = m_buf[0]
        for c in range(1, n_chains):
            m_all = jnp.maximum(m_all, m_buf[c])
        l_all = jnp.zeros_like(m_all)
        acc_all = jnp.zeros((n_rows, hd), F32)
        for c in range(n_chains):
            w = jnp.exp(m_buf[c] - m_all)
            l_all = l_all + w * l_buf[c]
            acc_all = acc_all + w[:, 0:1] * acc_buf[c]
        out = jnp.where(head_mask((n_rows, hd)), acc_all / l_all[:, 0:1], 0.0)
        o_ref[0] = jnp.sum(out.reshape(dec_seq, n_heads, hd), axis=1).astype(o_ref.dtype)


def _sample_attention(page_table, cache_kt, cache_vt, cache_lft, q, k_new, v_new, lf_new,
                      *, n_heads, pages_per_step):
    dec_batch, dec_seq, hd = q.shape
    n_pages = page_table.shape[1]
    page_size = cache_kt.shape[2]
    head_dim = hd // n_heads
    n_groups = n_pages // pages_per_step
    n_rows = dec_seq * n_heads
    sl = jnp.asarray(np.tril(np.ones((page_size, page_size), np.float32), -1), BF16)

    def page_map(j):
        def index_map(b, g, pt):
            return (pt[b, (n_groups - 1 - g) * pages_per_step + j], 0, 0)
        return index_map

    seq_map = lambda b, g, pt: (b, 0, 0)
    in_specs = (
        [pl.BlockSpec((1, hd, page_size), page_map(j)) for j in range(pages_per_step)]
        + [pl.BlockSpec((1, hd, page_size), page_map(j)) for j in range(pages_per_step)]
        + [pl.BlockSpec((1, n_heads, page_size), page_map(j)) for j in range(pages_per_step)]
        + [pl.BlockSpec((1, dec_seq, hd), seq_map)] * 3
        + [pl.BlockSpec((1, dec_seq, n_heads), seq_map),
           pl.BlockSpec((page_size, page_size), lambda b, g, pt: (0, 0))])
    n_chains = math.gcd(pages_per_step, SAMPLE_SOFTMAX_CHAINS)
    kern = functools.partial(_sample_attn_kernel, n_heads=n_heads, head_dim=head_dim,
                             dec_seq=dec_seq, pages_per_step=pages_per_step, page_size=page_size,
                             n_chains=n_chains)
    grid_spec = pltpu.PrefetchScalarGridSpec(
        num_scalar_prefetch=1,
        grid=(dec_batch, n_groups),
        in_specs=in_specs,
        out_specs=pl.BlockSpec((1, dec_seq, hd), seq_map),
        scratch_shapes=[
            pltpu.VMEM((n_rows, hd), BF16),
            pltpu.VMEM((n_rows, 1), F32),
            pltpu.VMEM((n_heads, LANES), F32),
            pltpu.VMEM((n_chains, n_rows, LANES), F32),
            pltpu.VMEM((n_chains, n_rows, LANES), F32),
            pltpu.VMEM((n_chains, n_rows, hd), F32),
        ])
    operands = ([cache_kt] * pages_per_step + [cache_vt] * pages_per_step
                + [cache_lft] * pages_per_step + [q, k_new, v_new, lf_new, sl])
    return pl.pallas_call(
        kern,
        grid_spec=grid_spec,
        out_shape=jax.ShapeDtypeStruct((dec_batch, dec_seq, hd), BF16),
        compiler_params=pltpu.CompilerParams(
            dimension_semantics=("arbitrary", "arbitrary"),
            vmem_limit_bytes=VMEM_LIMIT_BYTES),
        name="sample_attention",
    )(page_table, *operands)


def _pad_history(state, hist):
    keep = state.shape[1]
    if keep == hist:
        return state
    return jnp.pad(state, ((0, 0), (hist - keep, 0), (0, 0)))


def _to_time_major(a):
    b, t = a.shape[:2]
    return jnp.swapaxes(a, 0, 1).reshape((1, t * b) + a.shape[2:])


def _from_time_major(a, b):
    t = a.shape[1] // b
    return jnp.swapaxes(a.reshape((t, b) + a.shape[2:]), 0, 1)


def kernel(x_prompt, x_sample, cache_k, cache_v, cache_logf, state_lru_h, state_lru_conv,
           state_ffn_conv, page_table, norm_a_g, w_a_in, conv_a_w, conv_a_b, w_a_gate, b_a_gate,
           lru_lambda, w_a_out, norm_f_g, w_f_in, conv_f_w, conv_f_b, w_f_out, norm_kv_g, w_kv,
           b_forget, norm_b_g, w_q, w_o, norm_out_g):
    batch, seq, d_model = x_prompt.shape
    dec_batch, dec_seq, _ = x_sample.shape
    n_phys, page_size, n_heads, head_dim = cache_k.shape
    hd = n_heads * head_dim
    n_a = w_a_in.shape[0]
    n_b = w_q.shape[0]
    d_rnn = w_a_out.shape[1]
    d_ff = w_f_out.shape[1]
    assert 2 * head_dim == LANES and n_heads % 2 == 0 and n_heads <= LANES
    assert dec_batch % SUBLANES == 0 and dec_seq >= CONV_A_WIDTH - 1

    row = lambda a: a.reshape(1, -1).astype(F32)
    w_a_in_b = w_a_in.astype(BF16)
    w_a_gate_b = w_a_gate.astype(BF16)
    w_a_out_b = w_a_out.astype(BF16)
    w_f_in_b = w_f_in.astype(BF16)
    w_f_out_b = w_f_out.astype(BF16)
    w_q_b = w_q.astype(BF16)
    w_o_b = w_o.astype(BF16)
    wkf = jnp.concatenate(
        [w_kv[:, :hd], jnp.pad(w_kv[:, 2 * hd:], ((0, 0), (0, LANES - n_heads)))],
        axis=1).astype(BF16)
    wv = w_kv[:, hd:2 * hd].astype(BF16)
    bf = jnp.pad(b_forget.astype(F32), (0, LANES - n_heads)).reshape(1, LANES)

    def trunk(x2d, lru_h0, lru_conv0, ffn_conv0, *, groups, stride, tiles, with_aug,
              attention_fn):
        lru_h_new, lru_conv_new, ffn_conv_new = [], [], []
        layer = 0
        hist_a = _round_up((CONV_A_WIDTH - 1) * stride, SUBLANES)
        hist_f = _round_up((CONV_F_WIDTH - 1) * stride, SUBLANES)

        def ffn(x2d, layer, **kw):
            return _ffn_layer(x2d, _pad_history(ffn_conv0[layer], hist_f), row(norm_f_g[layer]),
                              w_f_in_b, conv_f_w[layer], row(conv_f_b[layer]),
                              w_f_out_b, layer=layer, groups=groups, stride=stride,
                              tile=tiles["ffn"], **kw)

        for i in range(n_a):
            x2d, c_new, h_new = _rglru_layer(
                x2d, _pad_history(lru_conv0[i], hist_a), lru_h0[i], row(norm_a_g[i]), w_a_in_b[i],
                conv_a_w[i], row(conv_a_b[i]), w_a_gate_b[i], b_a_gate[i][:, None, :],
                row(lru_lambda[i]), w_a_out_b[i], groups=groups, stride=stride,
                tile=tiles["rglru"], sub_tile=tiles["rglru_sub"])
            x2d, f_new = ffn(x2d, layer)
            lru_h_new.append(h_new)
            lru_conv_new.append(c_new)
            ffn_conv_new.append(f_new)
            layer += 1
        kvq = _kvq_layer(x2d, row(norm_kv_g), row(norm_b_g[0]), wkf, wv, bf, w_q_b[0],
                         n_heads=n_heads, groups=groups, tile=tiles["kvq"],
                         with_aug=with_aug)
        k_new, v_new, lf_new = kvq[:3]
        for j in range(n_b):
            assert j == 0, "one attention layer per shared K/V projection is supported"
            o2d = attention_fn(kvq)
            x2d, f_new = ffn(x2d, layer, pre=(o2d, w_o_b[j]),
                             final_gain=row(norm_out_g) if j == n_b - 1 else None)
            ffn_conv_new.append(f_new)
            layer += 1
        return x2d, k_new, v_new, lf_new, lru_h_new, lru_conv_new, ffn_conv_new

    zeros = lambda *s: jnp.zeros(s, F32)
    row_tile = _pick_tile(seq, PROMPT_ROW_TILE)
    tiles = {"rglru": row_tile, "rglru_sub": _pick_tile(row_tile, RGLRU_SUB_TILE),
             "ffn": row_tile, "kvq": row_tile}

    def prompt_attention_fn(kvq):
        q2t, k2, vtb = kvq[3:]
        return _prompt_attention(q2t, k2, vtb, n_heads=n_heads, tq=tiles["kvq"], tk=tiles["kvq"])

    (y, k_p, v_p, lf_p, h_p, c_p, f_p) = trunk(
        x_prompt.reshape(batch * seq, d_model),
        zeros(n_a, batch, 1, d_rnn), zeros(n_a, batch, CONV_A_WIDTH - 1, d_rnn),
        zeros(n_a + n_b, batch, CONV_F_WIDTH - 1, d_ff),
        groups=batch, stride=1, tiles=tiles, with_aug=True, attention_fn=prompt_attention_fn)
    y_prompt = y.reshape(batch, seq, d_model)
    k_prompt = k_p.reshape(batch, n_heads, head_dim, seq).transpose(0, 3, 1, 2)
    v_prompt = v_p.reshape(batch, n_heads, head_dim, seq).transpose(0, 3, 1, 2)
    logf_prompt = lf_p.transpose(0, 2, 1)
    lru_h_prompt = jnp.stack([h.reshape(batch, d_rnn) for h in h_p], axis=0)
    lru_conv_prompt = jnp.stack(c_p, axis=0)
    ffn_conv_prompt = jnp.stack(f_p, axis=0)

    cache_kt = cache_k.transpose(0, 2, 3, 1).reshape(n_phys, hd, page_size)
    cache_vt = cache_v.transpose(0, 2, 3, 1).reshape(n_phys, hd, page_size)
    cache_lft = jnp.swapaxes(cache_logf, 1, 2)
    n_pages = page_table.shape[1]

    def sample_attention_fn(kvq):
        k_t, v_t, lf_t, q_t = kvq
        bm = lambda a: _from_time_major(a[None], dec_batch)
        o = _sample_attention(page_table, cache_kt, cache_vt, cache_lft, bm(q_t), bm(k_t),
                              bm(v_t), bm(lf_t), n_heads=n_heads,
                              pages_per_step=_pick_tile(n_pages, SAMPLE_PAGES_PER_STEP))
        return _to_time_major(o)[0]

    n_rows = dec_batch * dec_seq
    tiles = {"rglru": n_rows, "rglru_sub": n_rows, "ffn": n_rows, "kvq": n_rows}
    (y, k_s, v_s, lf_s, h_s, c_s, f_s) = trunk(
        _to_time_major(x_sample)[0],
        state_lru_h[:, None], jnp.stack([_to_time_major(s) for s in state_lru_conv]),
        jnp.stack([_to_time_major(s) for s in state_ffn_conv]),
        groups=1, stride=dec_batch, tiles=tiles, with_aug=False,
        attention_fn=sample_attention_fn)
    bm = lambda a: _from_time_major(a[None], dec_batch)
    y_sample = bm(y)
    k_sample = bm(k_s).reshape(dec_batch, dec_seq, n_heads, head_dim)
    v_sample = bm(v_s).reshape(dec_batch, dec_seq, n_heads, head_dim)
    logf_sample = bm(lf_s)
    lru_h_sample = jnp.stack([h[0] for h in h_s], axis=0)
    lru_conv_sample = jnp.stack([_from_time_major(c, dec_batch) for c in c_s], axis=0)
    ffn_conv_sample = jnp.stack([_from_time_major(f, dec_batch) for f in f_s], axis=0)

    return (y_prompt, y_sample, k_prompt, v_prompt, logf_prompt, lru_h_prompt, lru_conv_prompt,
            ffn_conv_prompt, k_sample, v_sample, logf_sample, lru_h_sample, lru_conv_sample,
            ffn_conv_sample)
```

```python
import functools
import math

import numpy as np
import jax
import jax.numpy as jnp
from jax import lax
from jax.experimental import pallas as pl
from jax.experimental.pallas import tpu as pltpu

F32 = jnp.float32
BF16 = jnp.bfloat16

LANES = 128
SUBLANES = 8
VMEM_LIMIT_BYTES = 56 * 1024 * 1024

RMS_EPS = 1e-6
LRU_C = 8.0
CONV_A_WIDTH = 4
CONV_F_WIDTH = 3
MASK_VALUE = -1e30
LOG2_E = 1.4426950408889634

AUG_LANES_PER_HEAD = 8
PIECE_LANES = 32
SUM_ROWS = 16
SAMPLE_PAGES_PER_STEP = 16
SAMPLE_SOFTMAX_CHAINS = 4
PROMPT_ROW_TILE = 512
RGLRU_SUB_TILE = 256
FF_CHUNK = 1024


def _round_up(x, m):
    return (x + m - 1) // m * m


def _pick_tile(n, target):
    t = min(n, target)
    while n % t:
        t //= 2
    return t


def _rms_normalize(x):
    return x * lax.rsqrt(jnp.mean(x * x, axis=-1, keepdims=True) + RMS_EPS)


def _split3_bf16(x):
    hi = x.astype(BF16)
    r1 = x - hi.astype(F32)
    mid = r1.astype(BF16)
    lo = (r1 - mid.astype(F32)).astype(BF16)
    return hi, mid, lo


def _dot(a, b):
    return jnp.dot(a, b, preferred_element_type=F32)


def _dot_nt(a, b):
    return lax.dot_general(a, b, (((1,), (1,)), ((), ())), preferred_element_type=F32)


def _shifted_conv(buf_ref, w, b, *, width, hist, tile, stride):
    acc = None
    for j in range(width):
        back = (width - 1 - j) * stride
        term = buf_ref[hist - back:hist - back + tile, :] * w[j:j + 1, :]
        acc = term if acc is None else acc + term
    return acc + b


def _rglru_kernel(x_ref, conv0_ref, h0_ref, g_ref, win_ref, cw_ref, cb_ref, wg_ref, bg_ref,
                  lam_ref, wout_ref, perm_ref, y_ref, convn_ref, hn_ref, rec_buf, a_buf, b_buf,
                  h_carry, tail_buf, *, stride, sub_tile, n_sub):
    t = pl.program_id(1)
    d_rnn = rec_buf.shape[2]
    n_blocks, lru_block, _ = wg_ref.shape
    interleaved = stride == 1
    assert interleaved or n_sub == 1
    step_rows = SUBLANES if interleaved else stride
    hist = (CONV_A_WIDTH - 1) * step_rows
    seg = sub_tile // SUBLANES
    sub = lax.broadcasted_iota(jnp.int32, (SUBLANES, d_rnn), 0)

    @pl.when(t == 0)
    def _():
        h_carry[...] = h0_ref[0]
        if interleaved:
            tail_buf[...] = conv0_ref[0]
        else:
            rec_buf[0, 0:hist, :] = conv0_ref[0]

    def project_in(u):
        x = x_ref[u * sub_tile:(u + 1) * sub_tile, :]
        hn = (_rms_normalize(x) * g_ref[...]).astype(BF16)
        if interleaved:
            hn = _dot(perm_ref[0], hn).astype(BF16)
        else:
            @pl.when(t != 0)
            def _():
                rec_buf[u, 0:hist, :] = rec_buf[u, sub_tile:sub_tile + hist, :]

        proj = _dot(hn, win_ref[...])
        rec_buf[u, hist:hist + sub_tile, :] = proj[:, d_rnn:]
        return proj[:, :d_rnn]

    def recur(u):
        rec = rec_buf.at[u]
        if interleaved:
            for m in range(1, CONV_A_WIDTH):
                src = hist + (seg - m) * SUBLANES
                block = pltpu.roll(rec[src:src + SUBLANES, :], 1, 0)
                prev = tail_buf[SUBLANES - m:SUBLANES - m + 1, :]
                dst = (CONV_A_WIDTH - 1 - m) * SUBLANES
                rec[dst:dst + SUBLANES, :] = jnp.where(sub == 0, prev, block)
            for m in range(1, CONV_A_WIDTH):
                last = hist + (seg - m) * SUBLANES + SUBLANES - 1
                tail_buf[SUBLANES - m:SUBLANES - m + 1, :] = rec[last:last + 1, :]
        xc = _shifted_conv(rec, cw_ref[...], cb_ref[...], width=CONV_A_WIDTH, hist=hist,
                           tile=sub_tile, stride=step_rows)
        xcb = xc.astype(BF16)
        log_sig_lam = jax.nn.log_sigmoid(lam_ref[...])
        for n in range(n_blocks):
            cols = slice(n * lru_block, (n + 1) * lru_block)
            gates = jax.nn.sigmoid(_dot(xcb[:, cols], wg_ref[n]) + bg_ref[n])
            r = gates[:, :lru_block]
            ig = gates[:, lru_block:]
            log_a = LRU_C * r * log_sig_lam[:, cols]
            a = jnp.exp(log_a)
            a_buf[u, :, cols] = a
            one_minus_a2 = -jnp.tanh(log_a) * (a * a + 1.0)
            b_buf[u, :, cols] = jnp.sqrt(one_minus_a2) * ig * xc[:, cols]

        if interleaved:
            h_loc = jnp.zeros((SUBLANES, d_rnn), F32)
            decay = jnp.ones((SUBLANES, d_rnn), F32)
            for j in range(seg):
                rows = slice(j * SUBLANES, (j + 1) * SUBLANES)
                a_j = a_buf[u, rows, :]
                h_loc = a_j * h_loc + b_buf[u, rows, :]
                decay = a_j * decay
                b_buf[u, rows, :] = h_loc
                a_buf[u, rows, :] = decay
            carry = h_carry[...]
            carry_in = jnp.zeros((SUBLANES, d_rnn), F32)
            for s in range(SUBLANES):
                carry_in = jnp.where(sub == s, carry, carry_in)
                carry = h_loc[s:s + 1, :] + decay[s:s + 1, :] * carry
            h_carry[...] = carry
            return (b_buf[u].reshape(seg, SUBLANES, d_rnn)
                    + a_buf[u].reshape(seg, SUBLANES, d_rnn) * carry_in[None]
                    ).reshape(sub_tile, d_rnn)
        h_last = h_carry[...]
        for k in range(sub_tile // stride):
            rows = slice(k * stride, (k + 1) * stride)
            h_last = a_buf[u, rows, :] * h_last + b_buf[u, rows, :]
            b_buf[u, rows, :] = h_last
        h_carry[...] = h_last
        return b_buf[u]

    def project_out(u, gate_branch, hs):
        yv = (hs * jax.nn.gelu(gate_branch, approximate=True)).astype(BF16)
        if interleaved:
            yv = _dot(perm_ref[1], yv).astype(BF16)
        rows = slice(u * sub_tile, (u + 1) * sub_tile)
        y_ref[rows, :] = x_ref[rows, :] + _dot(yv, wout_ref[...])

    gate_branches = [project_in(u) for u in range(n_sub)]
    hidden = [recur(u) for u in range(n_sub)]
    for u in range(n_sub):
        project_out(u, gate_branches[u], hidden[u])

    hn_ref[0] = h_carry[...]
    if interleaved:
        convn_ref[0] = tail_buf[SUBLANES - (CONV_A_WIDTH - 1):SUBLANES, :]
    else:
        convn_ref[0] = rec_buf[0, sub_tile:sub_tile + hist, :]


def _rglru_layer(x2d, conv0, h0, g, w_in, conv_w, conv_b, w_gate, b_gate, lam, w_out,
                 *, groups, stride, tile, sub_tile):
    rows, d_model = x2d.shape
    d_rnn = w_out.shape[0]
    n_tiles = rows // groups // tile
    n_sub = tile // sub_tile
    hist = conv0.shape[1]
    keep = (CONV_A_WIDTH - 1) * stride
    step_rows = SUBLANES if stride == 1 else stride
    const2 = lambda b, t: (0, 0)
    const3 = lambda b, t: (0, 0, 0)
    seg = sub_tile // SUBLANES
    order = np.arange(sub_tile).reshape(SUBLANES, seg).T.reshape(-1)
    gather = np.zeros((sub_tile, sub_tile), np.float32)
    gather[np.arange(sub_tile), order] = 1.0
    perm = jnp.asarray(np.stack([gather, gather.T]), BF16)
    kern = functools.partial(_rglru_kernel, stride=stride, sub_tile=sub_tile, n_sub=n_sub)
    return pl.pallas_call(
        kern,
        grid=(groups, n_tiles),
        in_specs=[
            pl.BlockSpec((tile, d_model), lambda b, t: (b * n_tiles + t, 0)),
            pl.BlockSpec((1, hist, d_rnn), lambda b, t: (b, 0, 0)),
            pl.BlockSpec((1, stride, d_rnn), lambda b, t: (b, 0, 0)),
            pl.BlockSpec((1, d_model), const2),
            pl.BlockSpec(w_in.shape, const2),
            pl.BlockSpec(conv_w.shape, const2),
            pl.BlockSpec((1, d_rnn), const2),
            pl.BlockSpec(w_gate.shape, const3),
            pl.BlockSpec(b_gate.shape, const3),
            pl.BlockSpec((1, d_rnn), const2),
            pl.BlockSpec(w_out.shape, const2),
            pl.BlockSpec(perm.shape, const3),
        ],
        out_specs=[
            pl.BlockSpec((tile, d_model), lambda b, t: (b * n_tiles + t, 0)),
            pl.BlockSpec((1, keep, d_rnn), lambda b, t: (b, 0, 0)),
            pl.BlockSpec((1, stride, d_rnn), lambda b, t: (b, 0, 0)),
        ],
        out_shape=[
            jax.ShapeDtypeStruct((rows, d_model), F32),
            jax.ShapeDtypeStruct((groups, keep, d_rnn), F32),
            jax.ShapeDtypeStruct((groups, stride, d_rnn), F32),
        ],
        scratch_shapes=[
            pltpu.VMEM((n_sub, (CONV_A_WIDTH - 1) * step_rows + sub_tile, d_rnn), F32),
            pltpu.VMEM((n_sub, sub_tile, d_rnn), F32),
            pltpu.VMEM((n_sub, sub_tile, d_rnn), F32),
            pltpu.VMEM((stride, d_rnn), F32),
            pltpu.VMEM((SUBLANES, d_rnn), F32),
        ],
        compiler_params=pltpu.CompilerParams(
            dimension_semantics=("arbitrary", "arbitrary"),
            vmem_limit_bytes=VMEM_LIMIT_BYTES),
        name="rglru_layer",
    )(x2d, conv0, h0, g, w_in, conv_w, conv_b, w_gate, b_gate, lam, w_out, perm)


def _ffn_kernel(*refs, has_pre, has_final_norm, stride, hist, tile, ff_chunk):
    refs = list(refs)
    x_ref = refs.pop(0)
    if has_pre:
        o_ref = refs.pop(0)
        wo_ref = refs.pop(0)
    gf_ref, win_ref, cw_ref, cb_ref, wout_ref, g0_ref = refs[:6]
    refs = refs[6:]
    if has_final_norm:
        gout_ref = refs.pop(0)
    y_ref, convn_ref, gate_buf, hist_buf = refs
    t = pl.program_id(1)
    d_ff = wout_ref.shape[0]
    keep = (CONV_F_WIDTH - 1) * stride

    x1 = x_ref[...]
    if has_pre:
        x1 = x1 + _dot(o_ref[...], wo_ref[...])
    y_ref[...] = x1
    h = (_rms_normalize(x1) * gf_ref[...]).astype(BF16)

    for c in range(d_ff // ff_chunk):
        cols = slice(c * ff_chunk, (c + 1) * ff_chunk)
        up_cols = slice(d_ff + c * ff_chunk, d_ff + (c + 1) * ff_chunk)

        @pl.when(t == 0)
        def _():
            gate_buf[0:hist, :] = g0_ref[0, :, cols]

        @pl.when(t != 0)
        def _():
            gate_buf[0:hist, :] = hist_buf[c]

        gate_buf[hist:hist + tile, :] = _dot(h, win_ref[:, cols])
        up = _dot(h, win_ref[:, up_cols])
        gate = _shifted_conv(gate_buf, cw_ref[:, cols], cb_ref[:, cols], width=CONV_F_WIDTH,
                             hist=hist, tile=tile, stride=stride)
        hist_buf[c] = gate_buf[tile:tile + hist, :]
        convn_ref[0, :, cols] = gate_buf[hist + tile - keep:hist + tile, :]
        yv = (jax.nn.gelu(gate, approximate=True) * up).astype(BF16)
        y_ref[...] += _dot(yv, wout_ref[cols, :])

    if has_final_norm:
        y_ref[...] = _rms_normalize(y_ref[...]) * gout_ref[...]


def _ffn_layer(x2d, g0, gf, w_in, conv_w, conv_b, w_out, *, layer, groups, stride, tile,
               pre=None, final_gain=None):
    rows, d_model = x2d.shape
    d_ff = w_out.shape[1]
    ff_chunk = _pick_tile(d_ff, FF_CHUNK)
    n_chunks = d_ff // ff_chunk
    n_tiles = rows // groups // tile
    hist = g0.shape[1]
    keep = (CONV_F_WIDTH - 1) * stride
    const2 = lambda b, t: (0, 0)
    row_map = lambda b, t: (b * n_tiles + t, 0)
    resident = functools.partial(pl.BlockSpec, index_map=const2, pipeline_mode=pl.Buffered(1))
    of_layer = lambda a: pl.BlockSpec((None,) + a.shape[1:], lambda b, t: (layer, 0, 0),
                                      pipeline_mode=pl.Buffered(1))
    operands = [x2d]
    in_specs = [pl.BlockSpec((tile, d_model), row_map)]
    if pre is not None:
        o2d, w_o = pre
        operands += [o2d, w_o]
        in_specs += [pl.BlockSpec((tile, o2d.shape[1]), row_map), resident(w_o.shape)]
    operands += [gf, w_in, conv_w, conv_b, w_out, g0]
    in_specs += [
        resident((1, d_model)),
        of_layer(w_in),
        resident(conv_w.shape),
        resident(conv_b.shape),
        of_layer(w_out),
        pl.BlockSpec((1, hist, d_ff), lambda b, t: (b, 0, 0)),
    ]
    if final_gain is not None:
        operands.append(final_gain)
        in_specs.append(resident((1, d_model)))
    kern = functools.partial(_ffn_kernel, has_pre=pre is not None,
                             has_final_norm=final_gain is not None, stride=stride, hist=hist,
                             tile=tile, ff_chunk=ff_chunk)
    return pl.pallas_call(
        kern,
        grid=(groups, n_tiles),
        in_specs=in_specs,
        out_specs=[
            pl.BlockSpec((tile, d_model), row_map),
            pl.BlockSpec((1, keep, d_ff), lambda b, t: (b, 0, 0)),
        ],
        out_shape=[
            jax.ShapeDtypeStruct((rows, d_model), F32),
            jax.ShapeDtypeStruct((groups, keep, d_ff), F32),
        ],
        scratch_shapes=[
            pltpu.VMEM((hist + tile, ff_chunk), F32),
            pltpu.VMEM((n_chunks, hist, ff_chunk), F32),
        ],
        compiler_params=pltpu.CompilerParams(
            dimension_semantics=("arbitrary", "arbitrary"),
            vmem_limit_bytes=VMEM_LIMIT_BYTES),
        name="conv_ffn",
    )(*operands)


def _kvq_kernel(*refs, n_heads, scale, with_aug, tile):
    (x_ref, gkv_ref, gb_ref, wkf_ref, wv_ref, bf_ref, wq_ref) = refs[:7]
    if with_aug:
        tri_ref, pq_ref, pk_ref, oneq_ref, onek_ref = refs[7:12]
        kt_ref, vt_ref, lft_ref, q2t_ref, k2_ref, vtb_ref, f_carry = refs[12:]
    else:
        k_ref, v_ref, lf_ref, q_ref = refs[7:]
    hd = wv_ref.shape[1]
    n = _rms_normalize(x_ref[...])
    hk = (n * gkv_ref[...]).astype(BF16)
    hq = (n * gb_ref[...]).astype(BF16)
    kz = _dot(hk, wkf_ref[...])
    k = kz[:, :hd]
    v = _dot(hk, wv_ref[...])
    z = kz[:, hd:] + bf_ref[...]
    lane = lax.broadcasted_iota(jnp.int32, z.shape, 1)
    lf = jnp.where(lane < n_heads, jax.nn.log_sigmoid(z), 0.0)
    q = _dot(hq, wq_ref[...]) * scale
    if not with_aug:
        k_ref[...] = k
        v_ref[...] = v
        lf_ref[...] = lf[:, :n_heads]
        q_ref[...] = q
        return

    kt_ref[0] = k.T
    vt = v.T
    vt_ref[0] = vt
    vtb_ref[0, 0] = vt.astype(BF16)
    lft_ref[0] = lf.T[:n_heads, :]

    @pl.when(pl.program_id(1) == 0)
    def _():
        f_carry[...] = jnp.zeros_like(f_carry)

    sums = _dot(tri_ref[...], jnp.concatenate(_split3_bf16(lf), axis=1))
    f_cum = (sums[:, :LANES] + sums[:, LANES:2 * LANES] + sums[:, 2 * LANES:]) + f_carry[...]
    f_carry[...] = f_cum[tile - 1:tile, :]

    hi, mid, lo = (piece.astype(F32) for piece in _split3_bf16(f_cum * LOG2_E))
    f_cat = (hi + pltpu.roll(mid, PIECE_LANES, 1)
             + pltpu.roll(lo, 2 * PIECE_LANES, 1)).astype(BF16)
    aug_qt = (_dot(f_cat, pq_ref[...]) + oneq_ref[...]).T.astype(BF16)
    aug_k = (_dot(f_cat, pk_ref[...]) + onek_ref[...]).astype(BF16)
    qt = q.T.astype(BF16)
    kb = k.astype(BF16)
    for hp in range(n_heads // 2):
        src = slice(hp * LANES, (hp + 1) * LANES)
        q2t_ref[0, 0, 2 * hp * LANES:(2 * hp + 1) * LANES, :] = qt[src, :]
        q2t_ref[0, 0, (2 * hp + 1) * LANES:(2 * hp + 2) * LANES, :] = aug_qt[src, :]
        k2_ref[:, 2 * hp * LANES:(2 * hp + 1) * LANES] = kb[:, src]
        k2_ref[:, (2 * hp + 1) * LANES:(2 * hp + 2) * LANES] = aug_k[:, src]


def _aug_constants(n_heads, tile):
    n_pairs = n_heads // 2
    assert n_heads <= PIECE_LANES
    pq = np.zeros((LANES, n_pairs * LANES), np.float32)
    pk = np.zeros((LANES, n_pairs * LANES), np.float32)
    oneq = np.zeros((1, n_pairs * LANES), np.float32)
    onek = np.zeros((1, n_pairs * LANES), np.float32)
    for h in range(n_heads):
        base = (h // 2) * LANES + (h % 2) * AUG_LANES_PER_HEAD
        for piece in range(3):
            pq[piece * PIECE_LANES + h, base + piece] = 1.0
            onek[0, base + piece] = 1.0
            pk[piece * PIECE_LANES + h, base + 3 + piece] = -1.0
            oneq[0, base + 3 + piece] = 1.0
    tri = np.tril(np.ones((tile, tile), np.float32))
    return (jnp.asarray(tri, BF16), jnp.asarray(pq, BF16), jnp.asarray(pk, BF16),
            jnp.asarray(oneq), jnp.asarray(onek))


def _kvq_layer(x2d, gkv, gb, wkf, wv, bf, wq, *, n_heads, groups, tile, with_aug):
    rows, d_model = x2d.shape
    hd = wv.shape[1]
    n_tiles = rows // groups // tile
    scale = float(hd // n_heads) ** -0.5
    const2 = lambda b, t: (0, 0)
    row_map = lambda b, t: (b * n_tiles + t, 0)
    operands = [x2d, gkv, gb, wkf, wv, bf, wq]
    in_specs = [pl.BlockSpec((tile, d_model), row_map)]
    in_specs += [pl.BlockSpec(a.shape, const2) for a in operands[1:]]
    scratch = []
    if with_aug:
        scale *= LOG2_E
        seq = rows // groups
        col_map = lambda b, t: (b, 0, t)
        consts = _aug_constants(n_heads, tile)
        operands += list(consts)
        in_specs += [pl.BlockSpec(a.shape, const2) for a in consts]
        tile_map = lambda b, t: (b, t, 0, 0)
        out_shape = [jax.ShapeDtypeStruct((groups, hd, seq), F32),
                     jax.ShapeDtypeStruct((groups, hd, seq), F32),
                     jax.ShapeDtypeStruct((groups, n_heads, seq), F32),
                     jax.ShapeDtypeStruct((groups, n_tiles, 2 * hd, tile), BF16),
                     jax.ShapeDtypeStruct((rows, 2 * hd), BF16),
                     jax.ShapeDtypeStruct((groups, n_tiles, hd, tile), BF16)]
        out_specs = [pl.BlockSpec((1, hd, tile), col_map), pl.BlockSpec((1, hd, tile), col_map),
                     pl.BlockSpec((1, n_heads, tile), col_map),
                     pl.BlockSpec((1, 1, 2 * hd, tile), tile_map),
                     pl.BlockSpec((tile, 2 * hd), row_map),
                     pl.BlockSpec((1, 1, hd, tile), tile_map)]
        scratch = [pltpu.VMEM((1, LANES), F32)]
    else:
        out_shape = [jax.ShapeDtypeStruct((rows, hd), F32), jax.ShapeDtypeStruct((rows, hd), F32),
                     jax.ShapeDtypeStruct((rows, n_heads), F32),
                     jax.ShapeDtypeStruct((rows, hd), F32)]
        out_specs = [pl.BlockSpec((tile, hd), row_map), pl.BlockSpec((tile, hd), row_map),
                     pl.BlockSpec((tile, n_heads), row_map), pl.BlockSpec((tile, hd), row_map)]
    kern = functools.partial(_kvq_kernel, n_heads=n_heads, scale=scale, with_aug=with_aug,
                             tile=tile)
    return pl.pallas_call(
        kern,
        grid=(groups, n_tiles),
        in_specs=in_specs,
        out_specs=out_specs,
        out_shape=out_shape,
        scratch_shapes=scratch,
        compiler_params=pltpu.CompilerParams(
            dimension_semantics=("arbitrary", "arbitrary"),
            vmem_limit_bytes=VMEM_LIMIT_BYTES),
        name="kv_q_proj",
    )(*operands)


def _prompt_attn_kernel(q2t_ref, k2_ref, vt_ref, o_ref, qh_buf, s_buf, m_buf, acc_buf,
                        *, tq, tk, head_dim, n_tiles):
    def head_query(qi, j):
        q2t = q2t_ref[0, qi]
        chan = lax.broadcasted_iota(jnp.int32, q2t.shape, 0)
        aug = chan - LANES
        own = ((chan >= j * head_dim) & (chan < (j + 1) * head_dim)) | (
            (aug >= j * AUG_LANES_PER_HEAD) & (aug < (j + 1) * AUG_LANES_PER_HEAD))
        return jnp.where(own, q2t, jnp.zeros_like(q2t))

    def score(j, ki):
        k2 = k2_ref[0, pl.ds(pl.multiple_of(ki * tk, tk), tk), :]
        s_buf[j] = _dot(k2, qh_buf[j])

    def absorb(j, ki, masked):
        s = s_buf[j]
        if masked:
            causal = (lax.broadcasted_iota(jnp.int32, s.shape, 0)
                      <= lax.broadcasted_iota(jnp.int32, s.shape, 1))
            s = jnp.where(causal, s, MASK_VALUE)
        m_prev = m_buf[j]
        m_next = jnp.maximum(m_prev, jnp.max(s, axis=0, keepdims=True))
        alpha = jnp.exp2(m_prev - m_next)
        p = jnp.exp2(s - m_next)
        m_buf[j] = m_next
        vt = vt_ref[0, ki, j * head_dim:(j + 1) * head_dim, :]
        vt1 = jnp.concatenate([vt, jnp.ones((SUM_ROWS, tk), BF16)], axis=0)
        acc_buf[j] = alpha * acc_buf[j] + _dot(vt1, p.astype(BF16))

    def full_step(ki):
        score(1, ki)
        absorb(0, ki, masked=False)
        score(0, ki + 1)
        absorb(1, ki, masked=False)

    def two_steps(kp, carry):
        full_step(2 * kp)
        full_step(2 * kp + 1)
        return carry

    s_buf[0] = _dot(k2_ref[0, 0:tk, :], head_query(0, 0))

    def query_tile(qi, carry):
        for j in range(2):
            qh_buf[j] = head_query(qi, j)
        m_buf[...] = jnp.full_like(m_buf, MASK_VALUE)
        acc_buf[...] = jnp.zeros_like(acc_buf)

        lax.fori_loop(0, qi // 2, two_steps, 0)

        @pl.when(qi % 2 == 1)
        def _():
            full_step(qi - 1)

        score(1, qi)
        absorb(0, qi, masked=True)
        absorb(1, qi, masked=True)

        nxt = jnp.minimum(qi + 1, n_tiles - 1)
        s_buf[0] = _dot(k2_ref[0, 0:tk, :], head_query(nxt, 0))

        ot = jnp.concatenate(
            [acc_buf[j, 0:head_dim, :] / acc_buf[j, head_dim:head_dim + 1, :]
             for j in range(2)], axis=0)
        o_ref[pl.ds(pl.multiple_of(qi * tq, tq), tq), :] = ot.T.astype(o_ref.dtype)
        return carry

    lax.fori_loop(0, n_tiles, query_tile, 0)


def _prompt_attention(q2t, k2, vtb, *, n_heads, tq, tk):
    batch, n_tiles, hd, tile = vtb.shape
    assert tile == tq == tk, "attention tiles follow the projection kernel's row tile"
    seq = n_tiles * tile
    rows = batch * seq
    head_dim = hd // n_heads
    n_pairs = n_heads // 2
    kern = functools.partial(_prompt_attn_kernel, tq=tq, tk=tk, head_dim=head_dim,
                             n_tiles=n_tiles)
    return pl.pallas_call(
        kern,
        grid=(batch, n_pairs),
        in_specs=[
            pl.BlockSpec((1, n_tiles, 2 * LANES, tq), lambda b, hp: (b, 0, hp, 0)),
            pl.BlockSpec((1, seq, 2 * LANES), lambda b, hp: (b, 0, hp)),
            pl.BlockSpec((1, n_tiles, LANES, tk), lambda b, hp: (b, 0, hp, 0)),
        ],
        out_specs=pl.BlockSpec((seq, LANES), lambda b, hp: (b, hp)),
        out_shape=jax.ShapeDtypeStruct((rows, hd), BF16),
        scratch_shapes=[
            pltpu.VMEM((2, 2 * LANES, tq), BF16),
            pltpu.VMEM((2, tk, tq), F32),
            pltpu.VMEM((2, 1, tq), F32),
            pltpu.VMEM((2, head_dim + SUM_ROWS, tq), F32),
        ],
        compiler_params=pltpu.CompilerParams(
            dimension_semantics=("arbitrary", "arbitrary"),
            vmem_limit_bytes=VMEM_LIMIT_BYTES),
        name="prompt_attention",
    )(q2t, k2.reshape(batch, seq, 2 * hd), vtb)


def _sample_attn_kernel(pt_ref, *refs, n_heads, head_dim, dec_seq, pages_per_step, page_size,
                        n_chains):
    del pt_ref
    n_in = 3 * pages_per_step
    k_refs = refs[0:pages_per_step]
    v_refs = refs[pages_per_step:2 * pages_per_step]
    lf_refs = refs[2 * pages_per_step:n_in]
    q_ref, kn_ref, vn_ref, lfn_ref, sl_ref = refs[n_in:n_in + 5]
    o_ref = refs[n_in + 5]
    qrows_buf, cn_col, carry, m_buf, l_buf, acc_buf = refs[n_in + 6:]
    g = pl.program_id(1)
    n_rows = dec_seq * n_heads
    hd = n_heads * head_dim

    def head_mask(shape):
        r = lax.broadcasted_iota(jnp.int32, shape, 0) % n_heads
        lane = lax.broadcasted_iota(jnp.int32, shape, 1)
        return (lane >= r * head_dim) & (lane < (r + 1) * head_dim)

    def column_of(rowvec):
        wide = jnp.broadcast_to(rowvec, (n_rows, rowvec.shape[1]))
        r = lax.broadcasted_iota(jnp.int32, wide.shape, 0) % n_heads
        lane = lax.broadcasted_iota(jnp.int32, wide.shape, 1)
        return jnp.sum(jnp.where(lane == r, wide, 0.0), axis=1, keepdims=True)

    def online_update(c, s, weighted_values):
        m_prev = m_buf[c]
        m_next = jnp.maximum(m_prev, jnp.max(s, axis=1, keepdims=True))
        alpha = jnp.exp(m_prev - m_next)
        p = jnp.exp(s - m_next[:, 0:1])
        l_buf[c] = alpha * l_buf[c] + jnp.sum(p, axis=1, keepdims=True)
        acc_buf[c] = acc_buf[c] * alpha[:, 0:1] + weighted_values(p.astype(BF16))
        m_buf[c] = m_next

    @pl.when(g == 0)
    def _():
        q = q_ref[0]
        mask = head_mask((n_rows, hd))
        qrep = jnp.concatenate(
            [jnp.broadcast_to(q[t:t + 1, :], (n_heads, hd)) for t in range(dec_seq)], axis=0)
        qrows = jnp.where(mask, qrep, 0.0).astype(BF16)
        qrows_buf[...] = qrows
        m_buf[...] = jnp.full_like(m_buf, MASK_VALUE)
        l_buf[...] = jnp.zeros_like(l_buf)
        acc_buf[...] = jnp.zeros_like(acc_buf)
        carry[...] = jnp.zeros_like(carry)

        lfn = lfn_ref[0]
        cums = []
        run = None
        for t in range(dec_seq):
            run = lfn[t:t + 1, :] if run is None else run + lfn[t:t + 1, :]
            cums.append(run)
        r_t = lax.broadcasted_iota(jnp.int32, (n_rows, 1), 0) // n_heads
        cn = jnp.zeros((n_rows, 1), F32)
        for t in range(dec_seq):
            cn = jnp.where(r_t == t, column_of(cums[t]), cn)
        cn_col[...] = cn
        qrows_f = qrows.astype(F32)
        kn = kn_ref[0].astype(BF16).astype(F32)
        vn = vn_ref[0].astype(BF16).astype(F32)
        lane = lax.broadcasted_iota(jnp.int32, (n_rows, LANES), 1)
        s_new = jnp.full((n_rows, LANES), MASK_VALUE, F32)
        for s in range(dec_seq):
            qk = jnp.sum(qrows_f * kn[s:s + 1, :], axis=1, keepdims=True)
            s_new = jnp.where((lane == s) & (r_t >= s), qk + cn - column_of(cums[s]), s_new)

        def new_values(p):
            p = p.astype(F32)
            return sum(p[:, s:s + 1] * vn[s:s + 1, :] for s in range(dec_seq))

        online_update(0, s_new, new_values)

    sl = sl_ref[...]
    biases = [None] * pages_per_step
    run = carry[...]
    for j in reversed(range(pages_per_step)):
        lft = lf_refs[j][0]
        hi, mid, lo = _split3_bf16(lft)
        suffix = _dot(hi, sl) + _dot(mid, sl) + _dot(lo, sl) + run
        biases[j] = jnp.concatenate([suffix] * dec_seq, axis=0)
        run = run + jnp.sum(lft, axis=1, keepdims=True)
    carry[...] = run
    cn = cn_col[...]
    qrows = qrows_buf[...]
    per_chain = pages_per_step // n_chains
    chain_pages = [range(c * per_chain, (c + 1) * per_chain) for c in range(n_chains)]
    scores = []
    for pages in chain_pages:
        bias = jnp.concatenate([biases[j] for j in pages], axis=1) + cn
        kt = jnp.concatenate([k_refs[j][0].astype(BF16) for j in pages], axis=1)
        scores.append(_dot(qrows, kt) + bias)
    for c, pages in enumerate(chain_pages):
        vt = jnp.concatenate([v_refs[j][0].astype(BF16) for j in pages], axis=1)
        online_update(c, scores[c], lambda p, vt=vt: _dot_nt(p, vt))

    @pl.when(g == pl.num_programs(1) - 1)
    def _():
        m_all = m_buf[0]
        for c in range(1, n_chains):
            m_all = jnp.maximum(m_all, m_buf[c])
        l_all = jnp.zeros_like(m_all)
        acc_all = jnp.zeros((n_rows, hd), F32)
        for c in range(n_chains):
            w = jnp.exp(m_buf[c] - m_all)
            l_all = l_all + w * l_buf[c]
            acc_all = acc_all + w[:, 0:1] * acc_buf[c]
        out = jnp.where(head_mask((n_rows, hd)), acc_all / l_all[:, 0:1], 0.0)
        o_ref[0] = jnp.sum(out.reshape(dec_seq, n_heads, hd), axis=1).astype(o_ref.dtype)


def _sample_attention(page_table, cache_kt, cache_vt, cache_lft, q, k_new, v_new, lf_new,
                      *, n_heads, pages_per_step):
    dec_batch, dec_seq, hd = q.shape
    n_pages = page_table.shape[1]
    page_size = cache_kt.shape[2]
    head_dim = hd // n_heads
    n_groups = n_pages // pages_per_step
    n_rows = dec_seq * n_heads
    sl = jnp.asarray(np.tril(np.ones((page_size, page_size), np.float32), -1), BF16)

    def page_map(j):
        def index_map(b, g, pt):
            return (pt[b, (n_groups - 1 - g) * pages_per_step + j], 0, 0)
        return index_map

    seq_map = lambda b, g, pt: (b, 0, 0)
    in_specs = (
        [pl.BlockSpec((1, hd, page_size), page_map(j)) for j in range(pages_per_step)]
        + [pl.BlockSpec((1, hd, page_size), page_map(j)) for j in range(pages_per_step)]
        + [pl.BlockSpec((1, n_heads, page_size), page_map(j)) for j in range(pages_per_step)]
        + [pl.BlockSpec((1, dec_seq, hd), seq_map)] * 3
        + [pl.BlockSpec((1, dec_seq, n_heads), seq_map),
           pl.BlockSpec((page_size, page_size), lambda b, g, pt: (0, 0))])
    n_chains = math.gcd(pages_per_step, SAMPLE_SOFTMAX_CHAINS)
    kern = functools.partial(_sample_attn_kernel, n_heads=n_heads, head_dim=head_dim,
                             dec_seq=dec_seq, pages_per_step=pages_per_step, page_size=page_size,
                             n_chains=n_chains)
    grid_spec = pltpu.PrefetchScalarGridSpec(
        num_scalar_prefetch=1,
        grid=(dec_batch, n_groups),
        in_specs=in_specs,
        out_specs=pl.BlockSpec((1, dec_seq, hd), seq_map),
        scratch_shapes=[
            pltpu.VMEM((n_rows, hd), BF16),
            pltpu.VMEM((n_rows, 1), F32),
            pltpu.VMEM((n_heads, LANES), F32),
            pltpu.VMEM((n_chains, n_rows, LANES), F32),
            pltpu.VMEM((n_chains, n_rows, LANES), F32),
            pltpu.VMEM((n_chains, n_rows, hd), F32),
        ])
    operands = ([cache_kt] * pages_per_step + [cache_vt] * pages_per_step
                + [cache_lft] * pages_per_step + [q, k_new, v_new, lf_new, sl])
    return pl.pallas_call(
        kern,
        grid_spec=grid_spec,
        out_shape=jax.ShapeDtypeStruct((dec_batch, dec_seq, hd), BF16),
        compiler_params=pltpu.CompilerParams(
            dimension_semantics=("arbitrary", "arbitrary"),
            vmem_limit_bytes=VMEM_LIMIT_BYTES),
        name="sample_attention",
    )(page_table, *operands)


def _pad_history(state, hist):
    keep = state.shape[1]
    if keep == hist:
        return state
    return jnp.pad(state, ((0, 0), (hist - keep, 0), (0, 0)))


def _to_time_major(a):
    b, t = a.shape[:2]
    return jnp.swapaxes(a, 0, 1).reshape((1, t * b) + a.shape[2:])


def _from_time_major(a, b):
    t = a.shape[1] // b
    return jnp.swapaxes(a.reshape((t, b) + a.shape[2:]), 0, 1)


def kernel(x_prompt, x_sample, cache_k, cache_v, cache_logf, state_lru_h, state_lru_conv,
           state_ffn_conv, page_table, norm_a_g, w_a_in, conv_a_w, conv_a_b, w_a_gate, b_a_gate,
           lru_lambda, w_a_out, norm_f_g, w_f_in, conv_f_w, conv_f_b, w_f_out, norm_kv_g, w_kv,
           b_forget, norm_b_g, w_q, w_o, norm_out_g):
    batch, seq, d_model = x_prompt.shape
    dec_batch, dec_seq, _ = x_sample.shape
    n_phys, page_size, n_heads, head_dim = cache_k.shape
    hd = n_heads * head_dim
    n_a = w_a_in.shape[0]
    n_b = w_q.shape[0]
    d_rnn = w_a_out.shape[1]
    d_ff = w_f_out.shape[1]
    assert 2 * head_dim == LANES and n_heads % 2 == 0 and n_heads <= LANES
    assert dec_batch % SUBLANES == 0 and dec_seq >= CONV_A_WIDTH - 1

    row = lambda a: a.reshape(1, -1).astype(F32)
    w_a_in_b = w_a_in.astype(BF16)
    w_a_gate_b = w_a_gate.astype(BF16)
    w_a_out_b = w_a_out.astype(BF16)
    w_f_in_b = w_f_in.astype(BF16)
    w_f_out_b = w_f_out.astype(BF16)
    w_q_b = w_q.astype(BF16)
    w_o_b = w_o.astype(BF16)
    wkf = jnp.concatenate(
        [w_kv[:, :hd], jnp.pad(w_kv[:, 2 * hd:], ((0, 0), (0, LANES - n_heads)))],
        axis=1).astype(BF16)
    wv = w_kv[:, hd:2 * hd].astype(BF16)
    bf = jnp.pad(b_forget.astype(F32), (0, LANES - n_heads)).reshape(1, LANES)

    def trunk(x2d, lru_h0, lru_conv0, ffn_conv0, *, groups, stride, tiles, with_aug,
              attention_fn):
        lru_h_new, lru_conv_new, ffn_conv_new = [], [], []
        layer = 0
        hist_a = _round_up((CONV_A_WIDTH - 1) * stride, SUBLANES)
        hist_f = _round_up((CONV_F_WIDTH - 1) * stride, SUBLANES)

        def ffn(x2d, layer, **kw):
            return _ffn_layer(x2d, _pad_history(ffn_conv0[layer], hist_f), row(norm_f_g[layer]),
                              w_f_in_b, conv_f_w[layer], row(conv_f_b[layer]),
                              w_f_out_b, layer=layer, groups=groups, stride=stride,
                              tile=tiles["ffn"], **kw)

        for i in range(n_a):
            x2d, c_new, h_new = _rglru_layer(
                x2d, _pad_history(lru_conv0[i], hist_a), lru_h0[i], row(norm_a_g[i]), w_a_in_b[i],
                conv_a_w[i], row(conv_a_b[i]), w_a_gate_b[i], b_a_gate[i][:, None, :],
                row(lru_lambda[i]), w_a_out_b[i], groups=groups, stride=stride,
                tile=tiles["rglru"], sub_tile=tiles["rglru_sub"])
            x2d, f_new = ffn(x2d, layer)
            lru_h_new.append(h_new)
            lru_conv_new.append(c_new)
            ffn_conv_new.append(f_new)
            layer += 1
        kvq = _kvq_layer(x2d, row(norm_kv_g), row(norm_b_g[0]), wkf, wv, bf, w_q_b[0],
                         n_heads=n_heads, groups=groups, tile=tiles["kvq"],
                         with_aug=with_aug)
        k_new, v_new, lf_new = kvq[:3]
        for j in range(n_b):
            assert j == 0, "one attention layer per shared K/V projection is supported"
            o2d = attention_fn(kvq)
            x2d, f_new = ffn(x2d, layer, pre=(o2d, w_o_b[j]),
                             final_gain=row(norm_out_g) if j == n_b - 1 else None)
            ffn_conv_new.append(f_new)
            layer += 1
        return x2d, k_new, v_new, lf_new, lru_h_new, lru_conv_new, ffn_conv_new

    zeros = lambda *s: jnp.zeros(s, F32)
    row_tile = _pick_tile(seq, PROMPT_ROW_TILE)
    tiles = {"rglru": row_tile, "rglru_sub": _pick_tile(row_tile, RGLRU_SUB_TILE),
             "ffn": row_tile, "kvq": row_tile}

    def prompt_attention_fn(kvq):
        q2t, k2, vtb = kvq[3:]
        return _prompt_attention(q2t, k2, vtb, n_heads=n_heads, tq=tiles["kvq"], tk=tiles["kvq"])

    (y, k_p, v_p, lf_p, h_p, c_p, f_p) = trunk(
        x_prompt.reshape(batch * seq, d_model),
        zeros(n_a, batch, 1, d_rnn), zeros(n_a, batch, CONV_A_WIDTH - 1, d_rnn),
        zeros(n_a + n_b, batch, CONV_F_WIDTH - 1, d_ff),
        groups=batch, stride=1, tiles=tiles, with_aug=True, attention_fn=prompt_attention_fn)
    y_prompt = y.reshape(batch, seq, d_model)
    k_prompt = k_p.reshape(batch, n_heads, head_dim, seq).transpose(0, 3, 1, 2)
    v_prompt = v_p.reshape(batch, n_heads, head_dim, seq).transpose(0, 3, 1, 2)
    logf_prompt = lf_p.transpose(0, 2, 1)
    lru_h_prompt = jnp.stack([h.reshape(batch, d_rnn) for h in h_p], axis=0)
    lru_conv_prompt = jnp.stack(c_p, axis=0)
    ffn_conv_prompt = jnp.stack(f_p, axis=0)

    cache_kt = cache_k.transpose(0, 2, 3, 1).reshape(n_phys, hd, page_size)
    cache_vt = cache_v.transpose(0, 2, 3, 1).reshape(n_phys, hd, page_size)
    cache_lft = jnp.swapaxes(cache_logf, 1, 2)
    n_pages = page_table.shape[1]

    def sample_attention_fn(kvq):
        k_t, v_t, lf_t, q_t = kvq
        bm = lambda a: _from_time_major(a[None], dec_batch)
        o = _sample_attention(page_table, cache_kt, cache_vt, cache_lft, bm(q_t), bm(k_t),
                              bm(v_t), bm(lf_t), n_heads=n_heads,
                              pages_per_step=_pick_tile(n_pages, SAMPLE_PAGES_PER_STEP))
        return _to_time_major(o)[0]

    n_rows = dec_batch * dec_seq
    tiles = {"rglru": n_rows, "rglru_sub": n_rows, "ffn": n_rows, "kvq": n_rows}
    (y, k_s, v_s, lf_s, h_s, c_s, f_s) = trunk(
        _to_time_major(x_sample)[0],
        state_lru_h[:, None], jnp.stack([_to_time_major(s) for s in state_lru_conv]),
        jnp.stack([_to_time_major(s) for s in state_ffn_conv]),
        groups=1, stride=dec_batch, tiles=tiles, with_aug=False,
        attention_fn=sample_attention_fn)
    bm = lambda a: _from_time_major(a[None], dec_batch)
    y_sample = bm(y)
    k_sample = bm(k_s).reshape(dec_batch, dec_seq, n_heads, head_dim)
    v_sample = bm(v_s).reshape(dec_batch, dec_seq, n_heads, head_dim)
    logf_sample = bm(lf_s)
    lru_h_sample = jnp.stack([h[0] for h in h_s], axis=0)
    lru_conv_sample = jnp.stack([_from_time_major(c, dec_batch) for c in c_s], axis=0)
    ffn_conv_sample = jnp.stack([_from_time_major(f, dec_batch) for f in f_s], axis=0)

    return (y_prompt, y_sample, k_prompt, v_prompt, logf_prompt, lru_h_prompt, lru_conv_prompt,
            ffn_conv_prompt, k_sample, v_sample, logf_sample, lru_h_sample, lru_conv_sample,
            ffn_conv_sample)
```
